```python
import jax
import jax.numpy as jnp
from jax import lax
import numpy as np

D_MODEL = 1024
BATCH = 2
SEQ = 8192
DEPTH = 4
DEC_BATCH = 128
DEC_SEQ = 1
PAST_LEN = 2048
PAGE_SIZE = 128

N_MIXERS = 3
HEAD_DIM = 64
N_HEADS = D_MODEL // HEAD_DIM
NSA_KV_HEADS = 4
NSA_GROUP = N_HEADS // NSA_KV_HEADS
NSA_BLOCK = 64
NSA_TOPN = 16
NSA_WINDOW = 512
FOX_HEADS = D_MODEL // HEAD_DIM
Q_BLOCK = 128
POOL_WINDOWS = (2, 4, 8, 16)
POOL_GROUP_DIM = D_MODEL // len(POOL_WINDOWS)
POOL_STATE = max(POOL_WINDOWS) - 1
D_FF = 2816
CONV_W = 3
N_MOD = 6
RMS_EPS = 1e-6
NEG = -1e30
BIG = 1e30
ATTN_SCALE = HEAD_DIM ** -0.5

kernel_name = 'hybrid_nsa_fox_pool_convffn_step'


def _rms(x, g):
    xf = x.astype(jnp.float32)
    y = xf * lax.rsqrt(jnp.mean(xf * xf, axis=-1, keepdims=True) + RMS_EPS)
    return (y * g.astype(jnp.float32)).astype(x.dtype)


def _modulation(c, w, b):
    m = jax.nn.silu(c) @ w + b
    return [t[:, None, :] for t in jnp.split(m, N_MOD, axis=-1)]


def _masked_softmax(s, mask):
    p = jax.nn.softmax(jnp.where(mask, s, NEG), axis=-1)
    return jnp.where(mask, p, 0.0)


def _alibi_slopes():
    h = np.arange(1, N_HEADS + 1, dtype=np.float32)
    s = np.exp2(-8.0 * h / N_HEADS).astype(np.float32)
    return jnp.asarray(s).reshape(NSA_KV_HEADS, NSA_GROUP)


def _nsa_project(h, w_in):
    b, t, _ = h.shape
    kvw = NSA_KV_HEADS * HEAD_DIM
    p = h @ w_in
    q = p[..., :D_MODEL].reshape(b, t, NSA_KV_HEADS, NSA_GROUP, HEAD_DIM)
    kv = p[..., D_MODEL:D_MODEL + 6 * kvw].reshape(b, t, 3, 2, NSA_KV_HEADS, HEAD_DIM)
    gates = jax.nn.sigmoid(p[..., D_MODEL + 6 * kvw:].astype(jnp.float32))
    gates = gates.reshape(b, t, NSA_KV_HEADS, NSA_GROUP, 3)
    return q, kv[:, :, 0], kv[:, :, 1], kv[:, :, 2], gates


def _nsa_compress(kv, w_ck, w_cv):
    b, l = kv.shape[:2]
    nb = l // NSA_BLOCK
    blocks = kv.reshape(b, nb, NSA_BLOCK, 2, NSA_KV_HEADS, HEAD_DIM).transpose(0, 1, 3, 4, 2, 5)
    blocks = blocks.reshape(b, nb, 2, NSA_KV_HEADS, NSA_BLOCK * HEAD_DIM)
    return blocks[:, :, 0] @ w_ck, blocks[:, :, 1] @ w_cv


def _attn_shared(q, k, v, qpos, kpos, mask, slopes):
    s = jnp.einsum('bqkgd,bnkd->bkgqn', q, k).astype(jnp.float32) * ATTN_SCALE
    dist = (qpos[:, None] - kpos[None, :]).astype(jnp.float32)
    s = s - slopes[:, :, None, None] * dist
    p = _masked_softmax(s, mask)
    o = jnp.einsum('bkgqn,bnkd->bqkgd', p.astype(v.dtype), v)
    return o, p


def _attn_sel(q, k, v, qpos, kpos, mask, slopes):
    s = jnp.einsum('bqkgd,bkqnd->bkgqn', q, k).astype(jnp.float32) * ATTN_SCALE
    dist = (qpos[None, None, :, None] - kpos).astype(jnp.float32)
    s = s - slopes[None, :, :, None, None] * dist[:, :, None]
    p = _masked_softmax(s, mask[:, :, None])
    return jnp.einsum('bkgqn,bkqnd->bqkgd', p.astype(v.dtype), v)


def _nsa_select(imp, qpos):
    nb = imp.shape[-1]
    blk = jnp.arange(nb)[None, :]
    cur = (qpos // NSA_BLOCK)[:, None]
    started = blk * NSA_BLOCK <= qpos[:, None]
    forced = (blk == 0) | (blk == cur) | (blk == cur - 1)
    score = jnp.where(forced, BIG, jnp.where(started, imp, NEG))
    vals, idx = lax.top_k(score, min(NSA_TOPN, nb))
    return idx, vals > 0.5 * NEG


def _sel_positions(idx, ok, qpos):
    pos = idx[..., None] * NSA_BLOCK + jnp.arange(NSA_BLOCK)
    kpos = pos.reshape(idx.shape[:3] + (-1,))
    mask = jnp.repeat(ok, NSA_BLOCK, axis=-1) & (kpos <= qpos[None, None, :, None])
    return pos, kpos, mask


def _nsa_mix(o_c, o_s, o_w, gates):
    o = gates[..., 0:1] * o_c + gates[..., 1:2] * o_s + gates[..., 2:3] * o_w
    b, t = o.shape[:2]
    return o.astype(o_c.dtype).reshape(b, t, D_MODEL)


def _nsa_prompt(h, w_in, w_ck, w_cv, w_o):
    b, t, _ = h.shape
    q, kv_c, kv_s, kv_w, gates = _nsa_project(h, w_in)
    slopes = _alibi_slopes()
    kc, vc = _nsa_compress(kv_c, w_ck, w_cv)
    nb = t // NSA_BLOCK
    ends = jnp.arange(nb) * NSA_BLOCK + NSA_BLOCK - 1
    sel_blocks = kv_s.reshape(b, nb, NSA_BLOCK, 2, NSA_KV_HEADS, HEAD_DIM)
    win_pad = jnp.pad(kv_w, ((0, 0), (NSA_WINDOW, 0), (0, 0), (0, 0), (0, 0)))
    b_idx = jnp.arange(b)[:, None, None, None]
    k_idx = jnp.arange(NSA_KV_HEADS)[None, :, None, None]

    def q_block(j):
        q0 = j * Q_BLOCK
        qpos = q0 + jnp.arange(Q_BLOCK)
        qb = lax.dynamic_slice_in_dim(q, q0, Q_BLOCK, axis=1)
        gb = lax.dynamic_slice_in_dim(gates, q0, Q_BLOCK, axis=1)
        o_c, p_c = _attn_shared(qb, kc, vc, qpos, ends, ends[None, :] <= qpos[:, None], slopes)
        idx, ok = _nsa_select(p_c.sum(axis=2), qpos)
        _, kpos, smask = _sel_positions(idx, ok, qpos)
        rows = sel_blocks[b_idx, idx, :, :, k_idx]
        rows = rows.reshape(b, NSA_KV_HEADS, Q_BLOCK, -1, 2, HEAD_DIM)
        o_s = _attn_sel(qb, rows[..., 0, :], rows[..., 1, :], qpos, kpos, smask, slopes)
        win = lax.dynamic_slice_in_dim(win_pad, q0, NSA_WINDOW + Q_BLOCK, axis=1)
        wpos = q0 - NSA_WINDOW + jnp.arange(NSA_WINDOW + Q_BLOCK)
        wmask = ((wpos[None, :] >= 0) & (wpos[None, :] <= qpos[:, None])
                 & (wpos[None, :] > qpos[:, None] - NSA_WINDOW))
        o_w, _ = _attn_shared(qb, win[:, :, 0], win[:, :, 1], qpos, wpos, wmask, slopes)
        return _nsa_mix(o_c, o_s, o_w, gb)

    out = lax.map(q_block, jnp.arange(t // Q_BLOCK))
    out = out.transpose(1, 0, 2, 3).reshape(b, t, D_MODEL)
    return out @ w_o, kv_c, kv_s, kv_w[:, -min(NSA_WINDOW, t):]


def _nsa_sample(h, cache_c, cache_s, win_buf, page_table, w_in, w_ck, w_cv, w_o):
    b, s_new, _ = h.shape
    n_pages = page_table.shape[1]
    q, kv_c, kv_s, kv_w, gates = _nsa_project(h, w_in)
    slopes = _alibi_slopes()
    qpos = PAST_LEN + jnp.arange(s_new)
    total = PAST_LEN + s_new
    nb = -(-total // NSA_BLOCK)
    past_c = cache_c[page_table].reshape(b, PAST_LEN, 2, NSA_KV_HEADS, HEAD_DIM)
    full_c = jnp.concatenate([past_c, kv_c.astype(past_c.dtype)], axis=1)
    full_c = jnp.pad(full_c, ((0, 0), (0, nb * NSA_BLOCK - total), (0, 0), (0, 0), (0, 0)))
    kc, vc = _nsa_compress(full_c, w_ck, w_cv)
    ends = jnp.arange(nb) * NSA_BLOCK + NSA_BLOCK - 1
    o_c, p_c = _attn_shared(q, kc, vc, qpos, ends, ends[None, :] <= qpos[:, None], slopes)
    idx, ok = _nsa_select(p_c.sum(axis=2), qpos)
    pos, kpos, smask = _sel_positions(idx, ok, qpos)
    b5 = jnp.arange(b)[:, None, None, None, None]
    k5 = jnp.arange(NSA_KV_HEADS)[None, :, None, None, None]
    page = page_table[b5, jnp.clip(pos // PAGE_SIZE, 0, n_pages - 1)]
    past_rows = cache_s[page, pos % PAGE_SIZE, :, k5]
    new_rows = kv_s[b5, jnp.clip(pos - PAST_LEN, 0, s_new - 1), :, k5]
    rows = jnp.where((pos < PAST_LEN)[..., None, None], past_rows, new_rows)
    rows = rows.reshape(b, NSA_KV_HEADS, s_new, -1, 2, HEAD_DIM)
    o_s = _attn_sel(q, rows[..., 0, :], rows[..., 1, :], qpos, kpos, smask, slopes)
    wb = win_buf.shape[1]
    ext = jnp.concatenate([win_buf, kv_w.astype(win_buf.dtype)], axis=1)
    wpos = PAST_LEN - wb + jnp.arange(wb + s_new)
    wmask = (wpos[None, :] <= qpos[:, None]) & (wpos[None, :] > qpos[:, None] - NSA_WINDOW)
    o_w, _ = _attn_shared(q, ext[:, :, 0], ext[:, :, 1], qpos, wpos, wmask, slopes)
    out = _nsa_mix(o_c, o_s, o_w, gates)
    return out @ w_o, kv_c, kv_s, ext[:, -wb:]


def _fox_project(h, w_in, b_f):
    b, t, _ = h.shape
    p = h @ w_in
    q = p[..., :D_MODEL].reshape(b, t, FOX_HEADS, HEAD_DIM)
    k = p[..., D_MODEL:2 * D_MODEL].reshape(b, t, FOX_HEADS, HEAD_DIM)
    v = p[..., 2 * D_MODEL:3 * D_MODEL].reshape(b, t, FOX_HEADS, HEAD_DIM)
    logf = jax.nn.log_sigmoid(p[..., 3 * D_MODEL:].astype(jnp.float32) + b_f.astype(jnp.float32))
    return q, k, v, logf


def _fox_attend(q, k, v, cum_q, cum_k, qpos, kpos):
    s = jnp.einsum('bqhd,bkhd->bhqk', q, k).astype(jnp.float32) * ATTN_SCALE
    s = s + (cum_q[..., :, None] - cum_k[..., None, :])
    p = _masked_softmax(s, kpos[None, :] <= qpos[:, None])
    o = jnp.einsum('bhqk,bkhd->bqhd', p.astype(v.dtype), v)
    return o.reshape(o.shape[0], o.shape[1], D_MODEL)


def _fox_prompt(h, w_in, b_f, w_o):
    b, t, _ = h.shape
    q, k, v, logf = _fox_project(h, w_in, b_f)
    cum = jnp.cumsum(logf, axis=1).transpose(0, 2, 1)
    kpos = jnp.arange(t)

    def q_block(j):
        q0 = j * Q_BLOCK
        qb = lax.dynamic_slice_in_dim(q, q0, Q_BLOCK, axis=1)
        cq = lax.dynamic_slice_in_dim(cum, q0, Q_BLOCK, axis=2)
        return _fox_attend(qb, k, v, cq, cum, q0 + jnp.arange(Q_BLOCK), kpos)

    out = lax.map(q_block, jnp.arange(t // Q_BLOCK))
    out = out.transpose(1, 0, 2, 3).reshape(b, t, D_MODEL)
    return out @ w_o, jnp.stack([k, v], axis=2), logf


def _fox_sample(h, cache_kv, cache_logf, page_table, w_in, b_f, w_o):
    b, s_new, _ = h.shape
    q, k, v, logf = _fox_project(h, w_in, b_f)
    past_k = cache_kv[page_table, :, 0].reshape(b, PAST_LEN, FOX_HEADS, HEAD_DIM)
    past_v = cache_kv[page_table, :, 1].reshape(b, PAST_LEN, FOX_HEADS, HEAD_DIM)
    past_lf = cache_logf[page_table].reshape(b, PAST_LEN, FOX_HEADS).astype(jnp.float32)
    keys = jnp.concatenate([past_k, k.astype(past_k.dtype)], axis=1)
    vals = jnp.concatenate([past_v, v.astype(past_v.dtype)], axis=1)
    cum = jnp.cumsum(jnp.concatenate([past_lf, logf], axis=1), axis=1).transpose(0, 2, 1)
    out = _fox_attend(q, keys, vals, cum[:, :, PAST_LEN:], cum,
                      PAST_LEN + jnp.arange(s_new), jnp.arange(PAST_LEN + s_new))
    return out @ w_o, jnp.stack([k, v], axis=2), logf


def _pool_mix(h_ext, pos_ext, n_out, w_g, b_g, scale):
    b, n, _ = h_ext.shape
    hf = h_ext.astype(jnp.float32)
    csum = jnp.concatenate([jnp.zeros((b, 1, D_MODEL), jnp.float32), jnp.cumsum(hf, axis=1)], axis=1)
    rows = jnp.arange(n - n_out, n)
    outs = []
    for gi, w in enumerate(POOL_WINDOWS):
        c0 = gi * POOL_GROUP_DIM
        cs = csum[:, :, c0:c0 + POOL_GROUP_DIM]
        cnt = jnp.minimum(w, pos_ext[rows] + 1).astype(jnp.float32)
        mean = (cs[:, rows + 1] - cs[:, jnp.maximum(rows + 1 - w, 0)]) / cnt[None, :, None]
        mixed = mean - hf[:, rows, c0:c0 + POOL_GROUP_DIM]
        outs.append(mixed @ w_g[gi].astype(jnp.float32) + b_g[gi].astype(jnp.float32))
    y = jnp.concatenate(outs, axis=-1) * scale.astype(jnp.float32)
    return y.astype(h_ext.dtype)


def _pool_prompt(h, w_g, b_g, scale):
    t = h.shape[1]
    return _pool_mix(h, jnp.arange(t), t, w_g, b_g, scale), h[:, -POOL_STATE:]


def _pool_sample(h, state, w_g, b_g, scale):
    s_new = h.shape[1]
    ext = jnp.concatenate([state.astype(h.dtype), h], axis=1)
    pos_ext = PAST_LEN - POOL_STATE + jnp.arange(POOL_STATE + s_new)
    return _pool_mix(ext, pos_ext, s_new, w_g, b_g, scale), ext[:, -POOL_STATE:]


def _conv_ffn(h, g_prev, w_gu, conv_w, conv_b, w_d):
    t = h.shape[1]
    gu = h @ w_gu
    g, u = gu[..., :D_FF], gu[..., D_FF:]
    ext = jnp.concatenate([g_prev.astype(g.dtype), g], axis=1)
    a = conv_b
    for j in range(CONV_W):
        a = a + conv_w[j] * ext[:, j:j + t]
    return (jax.nn.silu(a) * u) @ w_d, ext[:, -(CONV_W - 1):]


def setup_inputs(seed: int = 0) -> dict:
    key = jax.random.key(seed)
    keys = iter(jax.random.split(key, 64))

    def nrm(shape, scale):
        return jax.random.normal(next(keys), shape, jnp.float32) * scale

    n_pages = PAST_LEN // PAGE_SIZE
    n_used = DEC_BATCH * n_pages
    n_pool = n_used + max(1, n_used // 4)
    win_buf = min(NSA_WINDOW, PAST_LEN)
    kvw = NSA_KV_HEADS * HEAD_DIM
    nsa_cols = D_MODEL + 6 * kvw + 3 * N_HEADS
    fox_cols = 3 * D_MODEL + FOX_HEADS
    inv_d = D_MODEL ** -0.5
    inv_blk = (NSA_BLOCK * HEAD_DIM) ** -0.5
    fox_bias = 4.0
    nsa_page = (n_pool, PAGE_SIZE, 2, NSA_KV_HEADS, HEAD_DIM)
    win_shape = (DEC_BATCH, win_buf, 2, NSA_KV_HEADS, HEAD_DIM)
    n_grp = len(POOL_WINDOWS)
    page_table = jax.random.permutation(next(keys), n_pool)[:n_used].reshape(DEC_BATCH, n_pages).astype(jnp.int32)
    return {
        'x_prompt': nrm((BATCH, SEQ, D_MODEL), 1.0),
        'x_sample': nrm((DEC_BATCH, DEC_SEQ, D_MODEL), 1.0),
        'cache_l0_cmp_kv': nrm(nsa_page, 1.0),
        'cache_l0_sel_kv': nrm(nsa_page, 1.0),
        'state_l0_win_kv': nrm(win_shape, 1.0),
        'cache_l1_kv': nrm((n_pool, PAGE_SIZE, 2, FOX_HEADS, HEAD_DIM), 1.0),
        'cache_l1_logf': jax.nn.log_sigmoid(fox_bias + nrm((n_pool, PAGE_SIZE, FOX_HEADS), 1.0)),
        'state_l2_pool': nrm((DEC_BATCH, POOL_STATE, D_MODEL), 1.0),
        'cache_l3_cmp_kv': nrm(nsa_page, 1.0),
        'cache_l3_sel_kv': nrm(nsa_page, 1.0),
        'state_l3_win_kv': nrm(win_shape, 1.0),
        'state_ffn_conv': nrm((DEPTH, DEC_BATCH, CONV_W - 1, D_FF), 1.0),
        'page_table': page_table,
        'c_prompt': nrm((BATCH, D_MODEL), 1.0),
        'c_sample': nrm((DEC_BATCH, D_MODEL), 1.0),
        'mod_w': nrm((DEPTH, D_MODEL, N_MOD * D_MODEL), 0.5 * inv_d),
        'mod_b': nrm((DEPTH, N_MOD * D_MODEL), 0.02),
        'norm_g': 1.0 + nrm((DEPTH, 4, D_MODEL), 0.1),
        'l0_nsa_w_in': nrm((D_MODEL, nsa_cols), inv_d),
        'l0_nsa_w_ck': nrm((NSA_BLOCK * HEAD_DIM, HEAD_DIM), inv_blk),
        'l0_nsa_w_cv': nrm((NSA_BLOCK * HEAD_DIM, HEAD_DIM), inv_blk),
        'l0_nsa_w_o': nrm((D_MODEL, D_MODEL), inv_d),
        'l1_fox_w_in': nrm((D_MODEL, fox_cols), inv_d),
        'l1_fox_b_f': fox_bias + nrm((FOX_HEADS,), 0.5),
        'l1_fox_w_o': nrm((D_MODEL, D_MODEL), inv_d),
        'l2_pool_w': nrm((n_grp, POOL_GROUP_DIM, POOL_GROUP_DIM), POOL_GROUP_DIM ** -0.5),
        'l2_pool_b': nrm((n_grp, POOL_GROUP_DIM), 0.02),
        'l2_pool_scale': 1.0 + nrm((D_MODEL,), 0.1),
        'l3_nsa_w_in': nrm((D_MODEL, nsa_cols), inv_d),
        'l3_nsa_w_ck': nrm((NSA_BLOCK * HEAD_DIM, HEAD_DIM), inv_blk),
        'l3_nsa_w_cv': nrm((NSA_BLOCK * HEAD_DIM, HEAD_DIM), inv_blk),
        'l3_nsa_w_o': nrm((D_MODEL, D_MODEL), inv_d),
        'ffn_w_gu': nrm((DEPTH, D_MODEL, 2 * D_FF), inv_d),
        'ffn_conv_w': nrm((DEPTH, CONV_W, D_FF), CONV_W ** -0.5),
        'ffn_conv_b': nrm((DEPTH, D_FF), 0.02),
        'ffn_w_d': nrm((DEPTH, D_FF, D_MODEL), D_FF ** -0.5),
    }


def reference(x_prompt, x_sample, cache_l0_cmp_kv, cache_l0_sel_kv, state_l0_win_kv,
              cache_l1_kv, cache_l1_logf, state_l2_pool,
              cache_l3_cmp_kv, cache_l3_sel_kv, state_l3_win_kv, state_ffn_conv,
              page_table, c_prompt, c_sample,
              mod_w, mod_b, norm_g,
              l0_nsa_w_in, l0_nsa_w_ck, l0_nsa_w_cv, l0_nsa_w_o,
              l1_fox_w_in, l1_fox_b_f, l1_fox_w_o,
              l2_pool_w, l2_pool_b, l2_pool_scale,
              l3_nsa_w_in, l3_nsa_w_ck, l3_nsa_w_cv, l3_nsa_w_o,
              ffn_w_gu, ffn_conv_w, ffn_conv_b, ffn_w_d):
    nsa = {0: (cache_l0_cmp_kv, cache_l0_sel_kv, state_l0_win_kv, l0_nsa_w_in, l0_nsa_w_ck, l0_nsa_w_cv, l0_nsa_w_o),
           3: (cache_l3_cmp_kv, cache_l3_sel_kv, state_l3_win_kv, l3_nsa_w_in, l3_nsa_w_ck, l3_nsa_w_cv, l3_nsa_w_o)}
    fox = {1: (cache_l1_kv, cache_l1_logf, l1_fox_w_in, l1_fox_b_f, l1_fox_w_o)}
    pool = {2: (state_l2_pool, l2_pool_w, l2_pool_b, l2_pool_scale)}
    xp, xs = x_prompt, x_sample
    st = {}
    conv_p, conv_s = [], []
    for i in range(DEPTH):
        mp = _modulation(c_prompt, mod_w[i], mod_b[i])
        ms = _modulation(c_sample, mod_w[i], mod_b[i])
        hp = _rms(xp, norm_g[i, 0]) * (1 + mp[1]) + mp[0]
        hs = _rms(xs, norm_g[i, 0]) * (1 + ms[1]) + ms[0]
        kind = i % N_MIXERS
        if kind == 0:
            c_c, c_s, s_w, w_in, w_ck, w_cv, w_o = nsa[i]
            op, st['l%d_cmp_p' % i], st['l%d_sel_p' % i], st['l%d_win_p' % i] = _nsa_prompt(hp, w_in, w_ck, w_cv, w_o)
            os_, st['l%d_cmp_s' % i], st['l%d_sel_s' % i], st['l%d_win_s' % i] = _nsa_sample(
                hs, c_c, c_s, s_w, page_table, w_in, w_ck, w_cv, w_o)
        elif kind == 1:
            c_kv, c_lf, w_in, b_f, w_o = fox[i]
            op, st['l%d_kv_p' % i], st['l%d_logf_p' % i] = _fox_prompt(hp, w_in, b_f, w_o)
            os_, st['l%d_kv_s' % i], st['l%d_logf_s' % i] = _fox_sample(hs, c_kv, c_lf, page_table, w_in, b_f, w_o)
        else:
            s_pool, w_g, b_g, scale = pool[i]
            op, st['l%d_pool_p' % i] = _pool_prompt(hp, w_g, b_g, scale)
            os_, st['l%d_pool_s' % i] = _pool_sample(hs, s_pool, w_g, b_g, scale)
        xp = xp + mp[2] * _rms(op, norm_g[i, 1])
        xs = xs + ms[2] * _rms(os_, norm_g[i, 1])
        hp = _rms(xp, norm_g[i, 2]) * (1 + mp[4]) + mp[3]
        hs = _rms(xs, norm_g[i, 2]) * (1 + ms[4]) + ms[3]
        fp, cp = _conv_ffn(hp, jnp.zeros((hp.shape[0], CONV_W - 1, D_FF), hp.dtype),
                           ffn_w_gu[i], ffn_conv_w[i], ffn_conv_b[i], ffn_w_d[i])
        fs, cs = _conv_ffn(hs, state_ffn_conv[i], ffn_w_gu[i], ffn_conv_w[i], ffn_conv_b[i], ffn_w_d[i])
        conv_p.append(cp)
        conv_s.append(cs)
        xp = xp + mp[5] * _rms(fp, norm_g[i, 3])
        xs = xs + ms[5] * _rms(fs, norm_g[i, 3])
    ffn_conv_prompt = jnp.stack(conv_p)
    ffn_conv_sample = jnp.stack(conv_s)
    return (xp, xs,
            st['l0_cmp_p'], st['l0_cmp_s'], st['l0_sel_p'], st['l0_sel_s'], st['l0_win_p'], st['l0_win_s'],
            st['l1_kv_p'], st['l1_kv_s'], st['l1_logf_p'], st['l1_logf_s'],
            st['l2_pool_p'], st['l2_pool_s'],
            st['l3_cmp_p'], st['l3_cmp_s'], st['l3_sel_p'], st['l3_sel_s'], st['l3_win_p'], st['l3_win_s'],
            ffn_conv_prompt, ffn_conv_sample)
```

```python
import functools

import jax
import jax.numpy as jnp
import numpy as np
from jax import lax
from jax.experimental import pallas as pl
from jax.experimental.pallas import tpu as pltpu

D_MODEL = 1024
DEPTH = 4
PAGE_SIZE = 128
HEAD_DIM = 64
N_HEADS = D_MODEL // HEAD_DIM
NSA_KV_HEADS = 4
NSA_GROUP = N_HEADS // NSA_KV_HEADS
NSA_BLOCK = 64
NSA_TOPN = 16
NSA_WINDOW = 512
FOX_HEADS = D_MODEL // HEAD_DIM
POOL_WINDOWS = (2, 4, 8, 16)
POOL_GROUP_DIM = D_MODEL // len(POOL_WINDOWS)
POOL_STATE = max(POOL_WINDOWS) - 1
D_FF = 2816
CONV_W = 3
N_MOD = 6
RMS_EPS = 1e-6
NEG = -1e30
BIG = 1e30
ATTN_SCALE = HEAD_DIM ** -0.5

LANES = 128
SUBLANES = 8
VMEM_LIMIT = 56 * 1024 * 1024

KVW = NSA_KV_HEADS * HEAD_DIM
NSA_Q0, NSA_C0, NSA_S0, NSA_W0, NSA_G0 = 0, D_MODEL, D_MODEL + 2 * KVW, D_MODEL + 4 * KVW, D_MODEL + 6 * KVW
NSA_COLS = NSA_G0 + NSA_KV_HEADS * LANES
FOX_LF0 = 3 * D_MODEL
FOX_COLS = FOX_LF0 + LANES

F32 = jnp.float32
BF16 = jnp.bfloat16


def _cparams(*sem):
    return pltpu.CompilerParams(dimension_semantics=sem, vmem_limit_bytes=VMEM_LIMIT)


def _bdot(a, b):
    return jnp.dot(a, b, preferred_element_type=F32)


def _dot_nt(a, b):
    return lax.dot_general(a, b, (((1,), (1,)), ((), ())), preferred_element_type=F32)


def _rms_rows(x, g):
    return x * lax.rsqrt(jnp.mean(x * x, axis=-1, keepdims=True) + RMS_EPS) * g


def _split3(x):
    a = x.astype(BF16)
    r = x - a.astype(F32)
    b = r.astype(BF16)
    c = (r - b.astype(F32)).astype(BF16)
    return a, b, c


def _mod_kernel(c_ref, w_ref, b_ref, o_ref):
    c = c_ref[...]
    a = (c * jax.nn.sigmoid(c)).astype(BF16)
    o_ref[...] = _bdot(a, w_ref[...].astype(BF16)) + b_ref[...]


def _modulation(c_all, mod_w, mod_b):
    rows = c_all.shape[0]
    n = N_MOD * D_MODEL
    tn = D_MODEL
    return pl.pallas_call(
        _mod_kernel,
        grid=(DEPTH, n // tn),
        in_specs=[pl.BlockSpec((rows, D_MODEL), lambda i, j: (0, 0)),
                  pl.BlockSpec((None, D_MODEL, tn), lambda i, j: (i, 0, j)),
                  pl.BlockSpec((None, 1, tn), lambda i, j: (i, 0, j))],
        out_specs=pl.BlockSpec((None, rows, tn), lambda i, j: (i, 0, j)),
        out_shape=jax.ShapeDtypeStruct((DEPTH, rows, n), F32),
        compiler_params=_cparams("parallel", "parallel"),
        name="modulation",
    )(c_all, mod_w, mod_b.reshape(DEPTH, 1, n))


def _norm_h(x, g, shift, scale):
    return _rms_rows(x, g) * (1.0 + scale) + shift


class _Rows:
    def __init__(self, nseq, t, tm):
        assert t % tm == 0 or t == 1
        self.nseq, self.t = nseq, t
        self.decode = t == 1
        self.tm = nseq if self.decode else tm
        self.m = nseq * t
        self.tiles_per_seq = 1 if self.decode else t // tm
        self.ntiles = self.m // self.tm

    def mod_spec(self, chunk):
        if self.decode:
            return pl.BlockSpec((self.tm, D_MODEL), lambda i, *_: (0, chunk))
        tps = self.tiles_per_seq
        return pl.BlockSpec((None, 1, D_MODEL), lambda i, *_: (i // tps, 0, chunk))

    def mod_arr(self, mod):
        return mod if self.decode else mod.reshape(self.nseq, 1, N_MOD * D_MODEL)

    def row_spec(self, width, col=0):
        return pl.BlockSpec((self.tm, width), lambda i, *_: (i, col))


def _proj_kernel(outs, x_ref, g_ref, sh_ref, sc_ref, w_ref, bias_ref, *o_refs):
    h = _norm_h(x_ref[...], g_ref[...], sh_ref[...], sc_ref[...])
    hb = h.astype(BF16)
    refs = iter(o_refs)
    for c0, c1, kind, dtypes in outs:
        if kind == "h":
            p = h
        else:
            p = _bdot(hb, w_ref[:, c0:c1])
        if kind == "scale":
            p = p * ATTN_SCALE
        elif kind == "sigmoid":
            p = jax.nn.sigmoid(p)
        elif kind == "logsigmoid":
            p = jax.nn.log_sigmoid(p + bias_ref[...])
        for dt in dtypes:
            next(refs)[...] = p.astype(dt)


def _norm_proj(rows, x, g, mod, chunks, w, bias, outs):
    ncols = w.shape[1]
    kern = functools.partial(_proj_kernel, outs)
    full = lambda shape: pl.BlockSpec(shape, lambda i: (0,) * len(shape))
    marr = rows.mod_arr(mod)
    flat = [(c1 - c0, dt) for c0, c1, _, dts in outs for dt in dts]
    return pl.pallas_call(
        kern,
        grid=(rows.ntiles,),
        in_specs=[rows.row_spec(D_MODEL), full((1, D_MODEL)),
                  rows.mod_spec(chunks[0]), rows.mod_spec(chunks[1]),
                  full((D_MODEL, ncols)), full((1, LANES))],
        out_specs=[rows.row_spec(wd) for wd, _ in flat],
        out_shape=[jax.ShapeDtypeStruct((rows.m, wd), dt) for wd, dt in flat],
        compiler_params=_cparams("parallel"),
        name="norm_proj",
    )(x, g.reshape(1, D_MODEL), marr, marr, w, bias)


def _oproj_kernel(o_ref, w_ref, x_ref, g_ref, gate_ref, y_ref):
    y = _bdot(o_ref[...], w_ref[...])
    y_ref[...] = x_ref[...] + gate_ref[...] * _rms_rows(y, g_ref[...])


def _out_proj_residual(rows, o, w, x, g, mod, gate_chunk):
    k = w.shape[0]
    full = lambda shape: pl.BlockSpec(shape, lambda i: (0,) * len(shape))
    return pl.pallas_call(
        _oproj_kernel,
        grid=(rows.ntiles,),
        in_specs=[rows.row_spec(k), full((k, D_MODEL)), rows.row_spec(D_MODEL),
                  full((1, D_MODEL)), rows.mod_spec(gate_chunk)],
        out_specs=rows.row_spec(D_MODEL),
        out_shape=jax.ShapeDtypeStruct((rows.m, D_MODEL), F32),
        compiler_params=_cparams("parallel"),
        name="out_proj_residual",
    )(o, w, x, g.reshape(1, D_MODEL), rows.mod_arr(mod))


FFN_TF = 256
FFN_NJ = D_FF // FFN_TF
HALO = SUBLANES


def _ffn_kernel(decode, tiles_per_seq, x_ref, g2_ref, sh_ref, sc_ref, wg_ref, wu_ref, cw_ref, cb_ref,
                wd_ref, g3_ref, gate_ref, p0_ref, p1_ref, y_ref, st_ref, h_scr, acc_scr, gs_scr, carry_scr):
    i, j = pl.program_id(0), pl.program_id(1)
    tm = x_ref.shape[0]

    @pl.when(j == 0)
    def _():
        h_scr[...] = _norm_h(x_ref[...], g2_ref[...], sh_ref[...], sc_ref[...]).astype(BF16)
        acc_scr[...] = jnp.zeros_like(acc_scr)

    h = h_scr[...]
    gcol = _bdot(h, wg_ref[...])
    ucol = _bdot(h, wu_ref[...])
    cw = cw_ref[...]
    if decode:
        a = cb_ref[...] + cw[0:1] * p0_ref[...] + cw[1:2] * p1_ref[...] + cw[2:3] * gcol
        st_ref[...] = gcol
    else:
        first = (i % tiles_per_seq) == 0

        @pl.when(first)
        def _():
            gs_scr[0:HALO, :] = jnp.zeros((HALO, gs_scr.shape[1]), F32)

        @pl.when(jnp.logical_not(first))
        def _():
            gs_scr[0:HALO, :] = carry_scr[j]

        gs_scr[HALO:HALO + tm, :] = gcol
        a = (cb_ref[...] + cw[0:1] * gs_scr[HALO - 2:HALO - 2 + tm, :]
             + cw[1:2] * gs_scr[HALO - 1:HALO - 1 + tm, :] + cw[2:3] * gcol)
        tail = gcol[tm - HALO:tm, :]
        carry_scr[j] = tail
        st_ref[...] = tail
    act = (a * jax.nn.sigmoid(a) * ucol).astype(BF16)
    acc_scr[...] += _bdot(act, wd_ref[...])

    @pl.when(j == pl.num_programs(1) - 1)
    def _():
        y_ref[...] = x_ref[...] + gate_ref[...] * _rms_rows(acc_scr[...], g3_ref[...])


def _conv_ffn(rows, x, g2, g3, mod, w_gu, conv_w, conv_b, w_d, prev):
    tm, tf = rows.tm, FFN_TF
    decode = rows.decode
    tps = rows.tiles_per_seq
    kern = functools.partial(_ffn_kernel, decode, tps)
    full = lambda shape: pl.BlockSpec(shape, lambda i, j: (0,) * len(shape))
    col = lambda r: pl.BlockSpec((r, tf), lambda i, j: (0, j))
    if decode:
        p0, p1 = prev[:, 0], prev[:, 1]
        pspec = pl.BlockSpec((tm, tf), lambda i, j: (0, j))
        st_spec = pl.BlockSpec((tm, tf), lambda i, j: (0, j))
        st_shape = jax.ShapeDtypeStruct((rows.m, D_FF), F32)
    else:
        p0 = p1 = jnp.zeros((SUBLANES, LANES), F32)
        pspec = full((SUBLANES, LANES))
        st_spec = pl.BlockSpec((None, HALO, tf), lambda i, j: (i, 0, j))
        st_shape = jax.ShapeDtypeStruct((rows.ntiles, HALO, D_FF), F32)
    marr = rows.mod_arr(mod)
    y, st = pl.pallas_call(
        kern,
        grid=(rows.ntiles, FFN_NJ),
        in_specs=[rows.row_spec(D_MODEL), full((1, D_MODEL)), rows.mod_spec(3), rows.mod_spec(4),
                  pl.BlockSpec((D_MODEL, tf), lambda i, j: (0, j)),
                  pl.BlockSpec((D_MODEL, tf), lambda i, j: (0, FFN_NJ + j)),
                  col(CONV_W), col(1),
                  pl.BlockSpec((tf, D_MODEL), lambda i, j: (j, 0)),
                  full((1, D_MODEL)), rows.mod_spec(5), pspec, pspec],
        out_specs=[rows.row_spec(D_MODEL), st_spec],
        out_shape=[jax.ShapeDtypeStruct((rows.m, D_MODEL), F32), st_shape],
        scratch_shapes=[pltpu.VMEM((tm, D_MODEL), BF16), pltpu.VMEM((tm, D_MODEL), F32),
                        pltpu.VMEM((HALO + tm, tf), F32), pltpu.VMEM((FFN_NJ, HALO, tf), F32)],
        compiler_params=_cparams("arbitrary", "arbitrary"),
        name="conv_ffn",
    )(x, g2.reshape(1, D_MODEL), marr, marr, w_gu, w_gu, conv_w, conv_b.reshape(1, D_FF),
      w_d, g3.reshape(1, D_MODEL), marr, p0, p1)
    if decode:
        return y, jnp.stack([prev[:, 1], st], axis=1)
    last = st.reshape(rows.nseq, tps, HALO, D_FF)[:, -1]
    return y, last[:, HALO - (CONV_W - 1):]


def _mm_kernel(a_ref, w_ref, o_ref):
    o_ref[...] = _bdot(a_ref[...], w_ref[...])


def _matmul(a, w, tm):
    m, k = a.shape
    n = w.shape[1]
    return pl.pallas_call(
        _mm_kernel,
        grid=(m // tm,),
        in_specs=[pl.BlockSpec((tm, k), lambda i: (i, 0)), pl.BlockSpec((k, n), lambda i: (0, 0))],
        out_specs=pl.BlockSpec((tm, n), lambda i: (i, 0)),
        out_shape=jax.ShapeDtypeStruct((m, n), F32),
        compiler_params=_cparams("parallel"),
        name="matmul",
    )(a, w)


Q_TILE = 128
KEY_CHUNK = 128
GQ = NSA_GROUP * Q_TILE


def _alibi_slopes():
    h = np.arange(1, N_HEADS + 1, dtype=np.float32)
    return np.exp2(-8.0 * h / N_HEADS).astype(np.float32).reshape(NSA_KV_HEADS, NSA_GROUP)


def _half_masks():
    lane = lax.broadcasted_iota(jnp.int32, (Q_TILE, LANES), 1)
    return lane < HEAD_DIM


def _stack_group_heads(q, lo, par=None):
    tiles = []
    for g in range(NSA_GROUP):
        t = q[:, (g // 2) * LANES:(g // 2 + 1) * LANES]
        keep = lo if g % 2 == 0 else jnp.logical_not(lo)
        t = jnp.where(keep, t, jnp.zeros_like(t))
        if par is not None:
            rolled = pltpu.roll(t.astype(F32), HEAD_DIM, axis=1).astype(BF16)
            t = jnp.where(par == g % 2, t, rolled)
        tiles.append(t)
    return jnp.concatenate(tiles, axis=0)


def _nsa_cmp_kernel(nb, q_ref, kcc_ref, vcc_ref, slope_ref, oc_ref, selb_ref):
    j = pl.program_id(2)
    q0 = j * Q_TILE
    lo = _half_masks()
    qs = _stack_group_heads(q_ref[...], lo)
    st = _dot_nt(kcc_ref[...], qs)
    n_i = lax.broadcasted_iota(jnp.int32, (nb, GQ), 0)
    qpos = q0 + (lax.broadcasted_iota(jnp.int32, (nb, GQ), 1) & (Q_TILE - 1))
    ends = n_i * NSA_BLOCK + (NSA_BLOCK - 1)
    mask = ends <= qpos
    s = st - slope_ref[...] * (qpos - ends).astype(F32)
    s = jnp.where(mask, s, NEG)
    e = jnp.exp(s - jnp.max(s, axis=0, keepdims=True))
    p = jnp.where(mask, e / jnp.sum(e, axis=0, keepdims=True), 0.0)
    acc = _bdot(p.T.astype(BF16), vcc_ref[...])
    for t in range(NSA_GROUP // 2):
        a0 = acc[(2 * t) * Q_TILE:(2 * t + 1) * Q_TILE]
        a1 = acc[(2 * t + 1) * Q_TILE:(2 * t + 2) * Q_TILE]
        oc_ref[:, t * LANES:(t + 1) * LANES] = jnp.where(lo, a0, a1)

    imp = p[:, 0:Q_TILE]
    for g in range(1, NSA_GROUP):
        imp = imp + p[:, g * Q_TILE:(g + 1) * Q_TILE]
    n2 = lax.broadcasted_iota(jnp.int32, (nb, Q_TILE), 0)
    cur = (q0 + lax.broadcasted_iota(jnp.int32, (nb, Q_TILE), 1)) // NSA_BLOCK
    forced = (n2 == 0) | (n2 == cur) | (n2 == cur - 1)
    score = jnp.where(forced, BIG, jnp.where(n2 <= cur, imp, NEG))

    def pick_next(_, carry):
        sc, sel = carry
        m = jnp.max(sc, axis=0, keepdims=True)
        first = jnp.min(jnp.where(sc == m, n2, nb), axis=0, keepdims=True)
        pick = n2 == first
        sel = jnp.where(pick & (m > 0.5 * NEG), 1.0, sel)
        return jnp.where(pick, -jnp.inf, sc), sel

    _, sel = lax.fori_loop(0, min(NSA_TOPN, nb), pick_next, (score, jnp.zeros((nb, Q_TILE), F32)), unroll=True)
    selb = jnp.where(sel > 0.0, 0.0, NEG).T
    if nb < LANES:
        selb = jnp.concatenate([selb, jnp.full((Q_TILE, LANES - nb), NEG, F32)], axis=1)
    selb_ref[...] = selb.astype(BF16)


def _nsa_compressed(b, t, q, kcc, vcc):
    nb = t // NSA_BLOCK
    nq = t // Q_TILE
    slopes = jnp.asarray(np.repeat(_alibi_slopes(), Q_TILE, axis=1).reshape(NSA_KV_HEADS, 1, GQ))
    return pl.pallas_call(
        functools.partial(_nsa_cmp_kernel, nb),
        grid=(b, NSA_KV_HEADS, nq),
        in_specs=[pl.BlockSpec((Q_TILE, KVW), lambda bi, k, j: (bi * nq + j, k)),
                  pl.BlockSpec((None, None, nb, LANES), lambda bi, k, j: (bi, k, 0, 0)),
                  pl.BlockSpec((None, None, nb, LANES), lambda bi, k, j: (bi, k, 0, 0)),
                  pl.BlockSpec((None, 1, GQ), lambda bi, k, j: (k, 0, 0))],
        out_specs=[pl.BlockSpec((Q_TILE, KVW), lambda bi, k, j: (bi * nq + j, k)),
                   pl.BlockSpec((None, None, Q_TILE, LANES), lambda bi, k, j: (bi, k, j, 0))],
        out_shape=[jax.ShapeDtypeStruct((b * t, D_MODEL), F32),
                   jax.ShapeDtypeStruct((b, NSA_KV_HEADS, t, LANES), BF16)],
        compiler_params=_cparams("parallel", "parallel", "parallel"),
        name="nsa_compressed",
    )(q, kcc, vcc, slopes)


def _softmax_step(s, v, m_ref, l_ref, acc_ref):
    m_prev = m_ref[...]
    m_new = jnp.maximum(m_prev, jnp.max(s, axis=-1, keepdims=True))
    alpha = jnp.exp(m_prev - m_new)
    p = jnp.exp(s - m_new)
    l_ref[...] = alpha * l_ref[...] + jnp.sum(p, axis=-1, keepdims=True)
    acc_ref[...] = alpha * acc_ref[...] + _bdot(p.astype(BF16), v)
    m_ref[...] = m_new


def _nsa_sw_kernel(q_ref, selb_ref, ks_ref, vs_ref, kw_ref, vw_ref, oc_ref, gate_ref, slope_ref, o_ref,
                   qaug_scr, dl_scr, ms_scr, ls_scr, as_scr, mw_scr, lw_scr, aw_scr):
    k, j = pl.program_id(1), pl.program_id(2)
    par = k % 2
    lo = _half_masks()
    qq = _stack_group_heads(q_ref[...], lo, par)
    selb = selb_ref[...]
    qaug_scr[:, 0:LANES] = jnp.concatenate([selb] * NSA_GROUP, axis=0)
    qaug_scr[:, LANES:2 * LANES] = qq
    slope = slope_ref[...]
    ql = lax.broadcasted_iota(jnp.int32, (GQ, KEY_CHUNK), 0) & (Q_TILE - 1)
    tl = lax.broadcasted_iota(jnp.int32, (GQ, KEY_CHUNK), 1)
    dl_scr[...] = slope * (ql - tl).astype(F32)
    for m_ref, l_ref, a_ref in ((ms_scr, ls_scr, as_scr), (mw_scr, lw_scr, aw_scr)):
        m_ref[...] = jnp.full(m_ref.shape, -jnp.inf, F32)
        l_ref[...] = jnp.zeros(l_ref.shape, F32)
        a_ref[...] = jnp.zeros(a_ref.shape, F32)

    def alibi(qk, c):
        return qk - dl_scr[...] - slope * ((j - c) * KEY_CHUNK).astype(F32)

    def sel_scores(c):
        k0 = pl.multiple_of(c * KEY_CHUNK, KEY_CHUNK)
        blk = 2 * c + (lax.broadcasted_iota(jnp.int32, (KEY_CHUNK, LANES), 0) // NSA_BLOCK)
        onehot = (blk == lax.broadcasted_iota(jnp.int32, (KEY_CHUNK, LANES), 1)).astype(BF16)
        kaug = jnp.concatenate([onehot, ks_ref[pl.ds(k0, KEY_CHUNK), :]], axis=1)
        return alibi(_dot_nt(qaug_scr[...], kaug), c), vs_ref[pl.ds(k0, KEY_CHUNK), :]

    def sel_body(c, carry):
        s, v = sel_scores(c)
        _softmax_step(s, v, ms_scr, ls_scr, as_scr)
        return carry

    lax.fori_loop(0, j, sel_body, 0)
    s, v = sel_scores(j)
    _softmax_step(jnp.where(tl <= ql, s, NEG), v, ms_scr, ls_scr, as_scr)

    n_win = NSA_WINDOW // KEY_CHUNK
    for dc in range(n_win + 1):
        c = j - n_win + dc

        @pl.when(c >= 0)
        def _():
            k0 = pl.multiple_of(c * KEY_CHUNK, KEY_CHUNK)
            s = alibi(_dot_nt(qaug_scr[:, LANES:2 * LANES], kw_ref[pl.ds(k0, KEY_CHUNK), :]), c)
            if dc == 0:
                s = jnp.where(tl > ql, s, NEG)
            elif dc == n_win:
                s = jnp.where(tl <= ql, s, NEG)
            _softmax_step(s, vw_ref[pl.ds(k0, KEY_CHUNK), :], mw_scr, lw_scr, aw_scr)

    o_s = as_scr[...] / ls_scr[...]
    o_w = aw_scr[...] / lw_scr[...]
    gates = gate_ref[...]

    def head(x, g):
        xg = x[g * Q_TILE:(g + 1) * Q_TILE]
        return jnp.where(par == g % 2, xg, pltpu.roll(xg, HEAD_DIM, axis=1))

    def gate(g, br):
        c = g * 3 + br
        return gates[:, c:c + 1]

    for t in range(NSA_GROUP // 2):
        g0, g1 = 2 * t, 2 * t + 1
        mix = (jnp.where(lo, gate(g0, 0), gate(g1, 0)) * oc_ref[:, t * LANES:(t + 1) * LANES]
               + jnp.where(lo, gate(g0, 1) * head(o_s, g0), gate(g1, 1) * head(o_s, g1))
               + jnp.where(lo, gate(g0, 2) * head(o_w, g0), gate(g1, 2) * head(o_w, g1)))
        o_ref[:, t * LANES:(t + 1) * LANES] = mix.astype(o_ref.dtype)


def _nsa_selected_window(b, t, q, selb, kvs, kvw, oc, gates):
    nq = t // Q_TILE
    slopes = jnp.asarray(np.repeat(_alibi_slopes(), Q_TILE, axis=1).reshape(NSA_KV_HEADS, GQ, 1))
    pairs = KVW // LANES
    qspec = pl.BlockSpec((Q_TILE, KVW), lambda bi, k, j: (bi * nq + j, k))
    kspec = pl.BlockSpec((None, t, LANES), lambda bi, k, j: (bi, 0, k // 2))
    vspec = pl.BlockSpec((None, t, LANES), lambda bi, k, j: (bi, 0, pairs + k // 2))
    return pl.pallas_call(
        _nsa_sw_kernel,
        grid=(b, NSA_KV_HEADS, nq),
        in_specs=[qspec,
                  pl.BlockSpec((None, None, Q_TILE, LANES), lambda bi, k, j: (bi, k, j, 0)),
                  kspec, vspec, kspec, vspec, qspec,
                  pl.BlockSpec((Q_TILE, LANES), lambda bi, k, j: (bi * nq + j, k)),
                  pl.BlockSpec((None, GQ, 1), lambda bi, k, j: (k, 0, 0))],
        out_specs=qspec,
        out_shape=jax.ShapeDtypeStruct((b * t, D_MODEL), BF16),
        scratch_shapes=[pltpu.VMEM((GQ, 2 * LANES), BF16), pltpu.VMEM((GQ, KEY_CHUNK), F32),
                        pltpu.VMEM((GQ, 1), F32), pltpu.VMEM((GQ, 1), F32), pltpu.VMEM((GQ, LANES), F32),
                        pltpu.VMEM((GQ, 1), F32), pltpu.VMEM((GQ, 1), F32), pltpu.VMEM((GQ, LANES), F32)],
        compiler_params=_cparams("parallel", "parallel", "arbitrary"),
        name="nsa_selected_window",
    )(q, selb, kvs, kvs, kvw, kvw, oc, gates, slopes)


def _nsa_weights(w_in):
    w = w_in.astype(BF16)
    ng = NSA_GROUP * 3
    gcols = [jnp.pad(w[:, NSA_G0 + k * ng:NSA_G0 + (k + 1) * ng], ((0, 0), (0, LANES - ng)))
             for k in range(NSA_KV_HEADS)]
    return jnp.concatenate([w[:, :NSA_G0]] + gcols, axis=1)


NSA_PROJ_OUTS = ((NSA_Q0, NSA_C0, "scale", (BF16,)),
                 (NSA_C0, NSA_S0, "", (F32, BF16)),
                 (NSA_S0, NSA_W0, "", (F32, BF16)),
                 (NSA_W0, NSA_G0, "", (F32, BF16)),
                 (NSA_G0, NSA_COLS, "sigmoid", (F32,)))
NO_BIAS = np.zeros((1, LANES), np.float32)


def _compress_blocks(kvc, nseq, nb, w_ck, w_cv):
    blocks = kvc.reshape(nseq, nb, NSA_BLOCK, 2, NSA_KV_HEADS, HEAD_DIM).transpose(3, 0, 1, 4, 2, 5)
    blocks = blocks.reshape(2, nseq * nb * NSA_KV_HEADS, NSA_BLOCK * HEAD_DIM)
    tm = min(256, blocks.shape[1])
    out = []
    for a, w in ((blocks[0], w_ck), (blocks[1], w_cv)):
        c = _matmul(a, w.astype(BF16), tm).reshape(nseq, nb, NSA_KV_HEADS, HEAD_DIM).transpose(0, 2, 1, 3)
        out.append(jnp.concatenate([c, c], axis=-1).astype(BF16))
    return out


def _nsa_prompt(rows, x, g, mod, w_in, w_ck, w_cv):
    b, t = rows.nseq, rows.t
    q, kvc, kvc_b, kvs, kvs_b, kvw, kvw_b, gates = _norm_proj(
        rows, x, g, mod, (0, 1), _nsa_weights(w_in), jnp.asarray(NO_BIAS), NSA_PROJ_OUTS)
    kcc, vcc = _compress_blocks(kvc_b, b, t // NSA_BLOCK, w_ck, w_cv)
    oc, selb = _nsa_compressed(b, t, q, kcc, vcc)
    o = _nsa_selected_window(b, t, q, selb, kvs_b.reshape(b, t, 2 * KVW), kvw_b.reshape(b, t, 2 * KVW), oc, gates)
    return o, kvc, kvs, kvw


FOX_TQ = 256
FOX_TK = 256
CUM_TILE = 256
FOX_PAIRS = FOX_HEADS // 2

FOX_PROJ_OUTS = ((0, D_MODEL, "scale", (BF16,)),
                 (D_MODEL, FOX_LF0, "", (F32, BF16)),
                 (FOX_LF0, FOX_COLS, "logsigmoid", (F32,)))


def _fox_weights(w_in, b_f):
    w = jnp.pad(w_in.astype(BF16), ((0, 0), (0, FOX_COLS - w_in.shape[1])))
    bias = jnp.pad(b_f.astype(F32), (0, LANES - FOX_HEADS)).reshape(1, LANES)
    return w, bias


def _tri_cumsum(x):
    n = x.shape[0]
    tri = (lax.broadcasted_iota(jnp.int32, (n, n), 1) <= lax.broadcasted_iota(jnp.int32, (n, n), 0)).astype(BF16)
    a, b, c = _split3(x)
    return _bdot(tri, a) + _bdot(tri, b) + _bdot(tri, c)


def _cumsum_kernel(x_ref, o_ref, carry_scr):
    @pl.when(pl.program_id(1) == 0)
    def _():
        carry_scr[...] = jnp.zeros_like(carry_scr)

    cum = _tri_cumsum(x_ref[...]) + carry_scr[0:1, :]
    o_ref[...] = cum
    carry_scr[...] = jnp.broadcast_to(cum[-1:, :], carry_scr.shape)


def _cumsum_rows(x, nseq, t):
    nt = t // CUM_TILE
    return pl.pallas_call(
        _cumsum_kernel,
        grid=(nseq, nt),
        in_specs=[pl.BlockSpec((CUM_TILE, LANES), lambda s, i: (s * nt + i, 0))],
        out_specs=pl.BlockSpec((CUM_TILE, LANES), lambda s, i: (s * nt + i, 0)),
        out_shape=jax.ShapeDtypeStruct(x.shape, F32),
        scratch_shapes=[pltpu.VMEM((SUBLANES, LANES), F32)],
        compiler_params=_cparams("parallel", "arbitrary"),
        name="cumsum_rows",
    )(x)


def _fox_kernel(q_ref, k_ref, v_ref, cq_ref, ck_ref, o_ref, qe_scr, m_scr, l_scr, acc_scr):
    qi = pl.program_id(2)
    lane = lax.broadcasted_iota(jnp.int32, (FOX_TQ, LANES), 1)
    lo = lane < HEAD_DIM
    q = q_ref[...]
    zero = jnp.zeros_like(q)
    qe_scr[0] = jnp.where(lo, q, zero)
    qe_scr[1] = jnp.where(lo, zero, q)
    m_scr[...] = jnp.full(m_scr.shape, -jnp.inf, F32)
    l_scr[...] = jnp.zeros(l_scr.shape, F32)
    acc_scr[...] = jnp.zeros(acc_scr.shape, F32)

    def chunk(c, diag):
        k0 = pl.multiple_of(c * FOX_TK, FOX_TK)
        kc = k_ref[pl.ds(k0, FOX_TK), :]
        vc = v_ref[pl.ds(k0, FOX_TK), :]
        for e in range(2):
            s = _dot_nt(qe_scr[e], kc) + (cq_ref[:, e:e + 1] - ck_ref[e:e + 1, pl.ds(k0, FOX_TK)])
            if diag:
                ql = lax.broadcasted_iota(jnp.int32, (FOX_TQ, FOX_TK), 0)
                tl = lax.broadcasted_iota(jnp.int32, (FOX_TQ, FOX_TK), 1)
                s = jnp.where(tl <= ql, s, NEG)
            _softmax_step(s, vc, m_scr.at[e], l_scr.at[e], acc_scr.at[e])

    def body(c, carry):
        chunk(c, False)
        return carry

    lax.fori_loop(0, qi, body, 0)
    chunk(qi, True)
    o_ref[...] = jnp.where(lo, acc_scr[0] / l_scr[0], acc_scr[1] / l_scr[1]).astype(o_ref.dtype)


def _fox_attention(b, t, q, k, v, cum_q, cum_k):
    assert FOX_TQ == FOX_TK
    nq = t // FOX_TQ
    kv_spec = pl.BlockSpec((None, t, LANES), lambda bi, hp, qi: (bi, 0, hp))
    return pl.pallas_call(
        _fox_kernel,
        grid=(b, FOX_PAIRS, nq),
        in_specs=[pl.BlockSpec((FOX_TQ, LANES), lambda bi, hp, qi: (bi * nq + qi, hp)),
                  kv_spec, kv_spec,
                  pl.BlockSpec((None, None, FOX_TQ, 2), lambda bi, hp, qi: (bi, hp, qi, 0)),
                  pl.BlockSpec((None, None, 2, t), lambda bi, hp, qi: (bi, hp, 0, 0))],
        out_specs=pl.BlockSpec((FOX_TQ, LANES), lambda bi, hp, qi: (bi * nq + qi, hp)),
        out_shape=jax.ShapeDtypeStruct((b * t, D_MODEL), BF16),
        scratch_shapes=[pltpu.VMEM((2, FOX_TQ, LANES), BF16), pltpu.VMEM((2, FOX_TQ, 1), F32),
                        pltpu.VMEM((2, FOX_TQ, 1), F32), pltpu.VMEM((2, FOX_TQ, LANES), F32)],
        compiler_params=_cparams("parallel", "parallel", "arbitrary"),
        name="fox_attention",
    )(q, k, v, cum_q, cum_k)


def _fox_prompt(rows, x, g, mod, w_in, b_f):
    b, t = rows.nseq, rows.t
    w, bias = _fox_weights(w_in, b_f)
    q, kv, kv_b, logf = _norm_proj(rows, x, g, mod, (0, 1), w, bias, FOX_PROJ_OUTS)
    cum = _cumsum_rows(logf, b, t)[:, :FOX_HEADS].reshape(b, t, FOX_PAIRS, 2)
    kv_b = kv_b.reshape(b, t, 2 * D_MODEL)
    o = _fox_attention(b, t, q, kv_b[:, :, :D_MODEL], kv_b[:, :, D_MODEL:],
                       cum.transpose(0, 2, 1, 3), cum.transpose(0, 2, 3, 1))
    return o, kv, logf[:, :FOX_HEADS]


def _pool_kernel(tps, pos0, h_ref, w_ref, b_ref, scale_ref, o_ref, ext_scr):
    i = pl.program_id(0)
    tm = h_ref.shape[0]
    halo = 2 * SUBLANES
    first = (i % tps) == 0

    @pl.when(first)
    def _():
        ext_scr[0:halo, :] = jnp.zeros((halo, D_MODEL), F32)

    @pl.when(jnp.logical_not(first))
    def _():
        ext_scr[0:halo, :] = ext_scr[tm:tm + halo, :]

    ext_scr[halo:halo + tm, :] = h_ref[...]
    pos = pos0 + (i % tps) * tm + lax.broadcasted_iota(jnp.int32, (tm, 1), 0)
    for gi, w in enumerate(POOL_WINDOWS):
        c0, c1 = gi * POOL_GROUP_DIM, (gi + 1) * POOL_GROUP_DIM
        win = ext_scr[halo:halo + tm, c0:c1]
        for back in range(1, w):
            win = win + ext_scr[halo - back:halo - back + tm, c0:c1]
        cnt = jnp.minimum(w, pos + 1).astype(F32)
        mixed = win / cnt - h_ref[:, c0:c1]
        y = _bdot(mixed.astype(BF16), w_ref[gi]) + b_ref[:, c0:c1]
        o_ref[:, c0:c1] = y * scale_ref[:, c0:c1]


def _pool_mix(h_ext, nseq, t, tm, pos0, w_g, b_g, scale):
    assert POOL_STATE < 2 * SUBLANES
    tps = t // tm
    full = lambda shape: pl.BlockSpec(shape, lambda i: (0,) * len(shape))
    return pl.pallas_call(
        functools.partial(_pool_kernel, tps, pos0),
        grid=(nseq * tps,),
        in_specs=[pl.BlockSpec((tm, D_MODEL), lambda i: (i, 0)),
                  full((len(POOL_WINDOWS), POOL_GROUP_DIM, POOL_GROUP_DIM)), full((1, D_MODEL)), full((1, D_MODEL))],
        out_specs=pl.BlockSpec((tm, D_MODEL), lambda i: (i, 0)),
        out_shape=jax.ShapeDtypeStruct((nseq * t, D_MODEL), F32),
        scratch_shapes=[pltpu.VMEM((tm + 2 * SUBLANES, D_MODEL), F32)],
        compiler_params=_cparams("arbitrary"),
        name="pool_mix",
    )(h_ext, w_g.astype(BF16), b_g.reshape(1, D_MODEL), scale.reshape(1, D_MODEL))


def _residual_kernel(o_ref, x_ref, g_ref, gate_ref, y_ref):
    y_ref[...] = x_ref[...] + gate_ref[...] * _rms_rows(o_ref[...], g_ref[...])


def _gated_residual(rows, o, x, g, mod, gate_chunk):
    full = lambda shape: pl.BlockSpec(shape, lambda i: (0,) * len(shape))
    return pl.pallas_call(
        _residual_kernel,
        grid=(rows.ntiles,),
        in_specs=[rows.row_spec(D_MODEL), rows.row_spec(D_MODEL), full((1, D_MODEL)), rows.mod_spec(gate_chunk)],
        out_specs=rows.row_spec(D_MODEL),
        out_shape=jax.ShapeDtypeStruct((rows.m, D_MODEL), F32),
        compiler_params=_cparams("parallel"),
        name="gated_residual",
    )(o, x, g.reshape(1, D_MODEL), rows.mod_arr(mod))


H_ONLY = ((0, D_MODEL, "h", (F32,)),)


def _norm_only(rows, x, g, mod, chunks):
    dummy = jnp.zeros((D_MODEL, LANES), BF16)
    return _norm_proj(rows, x, g, mod, chunks, dummy, jnp.asarray(NO_BIAS), H_ONLY)[0]


DEC_B = 8
NEW_ROWS = SUBLANES


def _paged_grid_spec(grid, in_specs, out_specs, scratch_shapes):
    return pltpu.PrefetchScalarGridSpec(num_scalar_prefetch=1, grid=grid, in_specs=in_specs,
                                        out_specs=out_specs, scratch_shapes=scratch_shapes)


def _cmp_dec_kernel(pt_ref, page_ref, bdk_ref, bdv_ref, o_ref, rows_scr):
    pg, npg = pl.program_id(1), pl.num_programs(1)
    ntile = 2 * KVW // LANES
    p0 = pl.multiple_of(pg * PAGE_SIZE, PAGE_SIZE)
    for t in range(ntile):
        rows_scr[t, pl.ds(p0, PAGE_SIZE), :] = page_ref[:, t * LANES:(t + 1) * LANES]

    @pl.when(pg == npg - 1)
    def _():
        nblk = o_ref.shape[0]

        def body(r, acc):
            parts = [_bdot(rows_scr[t, pl.ds(r, nblk, stride=NSA_BLOCK), :].astype(BF16),
                           (bdk_ref if t < ntile // 2 else bdv_ref)[r]) for t in range(ntile)]
            return acc + jnp.concatenate(parts, axis=1)

        acc = lax.fori_loop(0, NSA_BLOCK, body, jnp.zeros((nblk, 2 * KVW), F32))
        o_ref[...] = acc.astype(o_ref.dtype)


def _pair_blockdiag(w):
    w3 = w.astype(BF16).reshape(NSA_BLOCK, HEAD_DIM, HEAD_DIM)
    z = jnp.zeros_like(w3)
    return jnp.concatenate([jnp.concatenate([w3, z], axis=2), jnp.concatenate([z, w3], axis=2)], axis=1)


def _cmp_decode(page_table, cache, w_ck, w_cv):
    ns, npg = page_table.shape
    nblk = npg * PAGE_SIZE // NSA_BLOCK
    full3 = pl.BlockSpec((NSA_BLOCK, LANES, LANES), lambda b, pg, pt: (0, 0, 0))
    return pl.pallas_call(
        _cmp_dec_kernel,
        grid_spec=_paged_grid_spec(
            (ns, npg),
            [pl.BlockSpec((None, PAGE_SIZE, 2 * KVW), lambda b, pg, pt: (pt[b * npg + pg], 0, 0)), full3, full3],
            pl.BlockSpec((None, nblk, 2 * KVW), lambda b, pg, pt: (b, 0, 0)),
            [pltpu.VMEM((2 * KVW // LANES, npg * PAGE_SIZE, LANES), F32)]),
        out_shape=jax.ShapeDtypeStruct((ns, nblk, 2 * KVW), BF16),
        compiler_params=_cparams("parallel", "arbitrary"),
        name="nsa_decode_compress",
    )(page_table.reshape(-1), cache, _pair_blockdiag(w_ck), _pair_blockdiag(w_cv))


def _dec_select_kernel(qpos, nblk, qbd_ref, kcv_ref, slope_ref, oc_ref, selb_ref, imp_scr):
    lane_b = lax.broadcasted_iota(jnp.int32, (N_HEADS, nblk), 1)
    ends = lane_b * NSA_BLOCK + (NSA_BLOCK - 1)
    mask = ends <= qpos
    bias = slope_ref[...] * (qpos - ends).astype(F32)
    imp_scr[...] = jnp.zeros(imp_scr.shape, F32)
    for bi in range(DEC_B):
        s = _dot_nt(qbd_ref[bi], kcv_ref[bi, :, 0:KVW]) - bias
        s = jnp.where(mask, s, NEG)
        e = jnp.exp(s - jnp.max(s, axis=-1, keepdims=True))
        p = jnp.where(mask, e / jnp.sum(e, axis=-1, keepdims=True), 0.0)
        oc_ref[bi] = _bdot(p.astype(BF16), kcv_ref[bi, :, KVW:2 * KVW])
        imp = p[0:NSA_KV_HEADS]
        for g in range(1, NSA_GROUP):
            imp = imp + p[g * NSA_KV_HEADS:(g + 1) * NSA_KV_HEADS]
        imp_scr[bi * NSA_KV_HEADS:(bi + 1) * NSA_KV_HEADS, 0:nblk] = imp

    rows = DEC_B * NSA_KV_HEADS
    n2 = lax.broadcasted_iota(jnp.int32, (rows, LANES), 1)
    cur = qpos // NSA_BLOCK
    forced = (n2 == 0) | (n2 == cur) | (n2 == cur - 1)
    score = jnp.where(forced, BIG, jnp.where(n2 <= cur, imp_scr[...], NEG))
    score = jnp.where(n2 <= cur, score, -jnp.inf)

    def pick_next(_, carry):
        sc, sel = carry
        m = jnp.max(sc, axis=-1, keepdims=True)
        first = jnp.min(jnp.where(sc == m, n2, LANES), axis=-1, keepdims=True)
        pick = n2 == first
        sel = jnp.where(pick & (m > 0.5 * NEG), 1.0, sel)
        return jnp.where(pick, -jnp.inf, sc), sel

    _, sel = lax.fori_loop(0, min(NSA_TOPN, cur + 1), pick_next, (score, jnp.zeros((rows, LANES), F32)), unroll=True)
    selb = jnp.where(sel > 0.0, 0.0, NEG).astype(BF16)
    for bi in range(DEC_B):
        one = selb[bi * NSA_KV_HEADS:(bi + 1) * NSA_KV_HEADS]
        selb_ref[bi] = jnp.concatenate([one] * NSA_GROUP, axis=0)


def _dec_slopes():
    return jnp.asarray(_alibi_slopes().T.reshape(N_HEADS, 1))


def _nsa_decode_select(qbd, kcv, qpos):
    ns, nblk = kcv.shape[0], kcv.shape[1]
    assert qpos // NSA_BLOCK < LANES and ns % DEC_B == 0
    blk = lambda shape: pl.BlockSpec((DEC_B,) + shape, lambda i: (i, 0, 0))
    return pl.pallas_call(
        functools.partial(_dec_select_kernel, qpos, nblk),
        grid=(ns // DEC_B,),
        in_specs=[blk((N_HEADS, KVW)), blk((nblk, 2 * KVW)), pl.BlockSpec((N_HEADS, 1), lambda i: (0, 0))],
        out_specs=[blk((N_HEADS, KVW)), blk((N_HEADS, LANES))],
        out_shape=[jax.ShapeDtypeStruct((ns, N_HEADS, KVW), F32), jax.ShapeDtypeStruct((ns, N_HEADS, LANES), BF16)],
        scratch_shapes=[pltpu.VMEM((DEC_B * NSA_KV_HEADS, LANES), F32)],
        compiler_params=_cparams("parallel"),
        name="nsa_decode_select",
    )(qbd, kcv, _dec_slopes())


def _new_key_tile(row):
    r = lax.broadcasted_iota(jnp.int32, (NEW_ROWS, row.shape[1]), 0)
    return jnp.where(r == 0, jnp.broadcast_to(row, (NEW_ROWS, row.shape[1])), 0.0).astype(BF16)


def _new_key_mask(s):
    return jnp.where(lax.broadcasted_iota(jnp.int32, s.shape, 1) == 0, s, NEG)


def _dec_attend_kernel(qpos, wlen, pt_ref, qbd_ref, selb_ref, page_ref, win_ref, snew_ref, wnew_ref, oc_ref,
                       gate_ref, slope_ref, o_ref, wout_ref, m_scr, l_scr, acc_scr, ow_scr):
    pg, npg = pl.program_id(1), pl.num_programs(1)
    slope = slope_ref[...]
    qbd = qbd_ref[...]

    @pl.when(pg == 0)
    def _():
        m_scr[...] = jnp.full(m_scr.shape, -jnp.inf, F32)
        l_scr[...] = jnp.zeros(l_scr.shape, F32)
        acc_scr[...] = jnp.zeros(acc_scr.shape, F32)
        win = win_ref[...]
        wpos = qpos - wlen + lax.broadcasted_iota(jnp.int32, (N_HEADS, wlen), 1)
        s = _dot_nt(qbd, win[:, 0:KVW].astype(BF16)) - slope * (qpos - wpos).astype(F32)
        s = jnp.where(wpos > qpos - NSA_WINDOW, s, NEG)
        wnew = wnew_ref[...]
        s_new = _new_key_mask(_dot_nt(qbd, _new_key_tile(wnew[:, 0:KVW])))
        m = jnp.maximum(jnp.max(s, axis=-1, keepdims=True), jnp.max(s_new, axis=-1, keepdims=True))
        p, p_new = jnp.exp(s - m), jnp.exp(s_new - m)
        l = jnp.sum(p, axis=-1, keepdims=True) + jnp.sum(p_new, axis=-1, keepdims=True)
        ow = _bdot(p.astype(BF16), win[:, KVW:2 * KVW].astype(BF16))
        ow = ow + _bdot(p_new.astype(BF16), _new_key_tile(wnew[:, KVW:2 * KVW]))
        ow_scr[...] = ow / l
        row = lax.broadcasted_iota(jnp.int32, win.shape, 0)
        wout_ref[...] = jnp.where(row == wlen - 1, jnp.broadcast_to(wnew, win.shape), pltpu.roll(win, wlen - 1, axis=0))

    page = page_ref[...]
    blk = 2 * pg + (lax.broadcasted_iota(jnp.int32, (PAGE_SIZE, LANES), 0) // NSA_BLOCK)
    onehot = (blk == lax.broadcasted_iota(jnp.int32, (PAGE_SIZE, LANES), 1)).astype(BF16)
    kpos = pg * PAGE_SIZE + lax.broadcasted_iota(jnp.int32, (N_HEADS, PAGE_SIZE), 1)
    qaug = jnp.concatenate([qbd, selb_ref[...]], axis=1)
    kaug = jnp.concatenate([page[:, 0:KVW].astype(BF16), onehot], axis=1)
    s = _dot_nt(qaug, kaug) - slope * (qpos - kpos).astype(F32)
    _softmax_step(s, page[:, KVW:2 * KVW].astype(BF16), m_scr, l_scr, acc_scr)

    @pl.when(pg == npg - 1)
    def _():
        snew = snew_ref[...]
        s_new = _new_key_mask(_dot_nt(qbd, _new_key_tile(snew[:, 0:KVW])))
        _softmax_step(s_new, _new_key_tile(snew[:, KVW:2 * KVW]), m_scr, l_scr, acc_scr)
        gates = gate_ref[...]
        o_ref[...] = (gates[:, 0:1] * oc_ref[...] + gates[:, 1:2] * (acc_scr[...] / l_scr[...])
                      + gates[:, 2:3] * ow_scr[...])


def _nsa_decode_attend(page_table, qbd, selb, cache, win_buf, kvs_new, kvw_new, oc, gates, qpos):
    ns, npg = page_table.shape
    wlen = win_buf.shape[1]
    assert qpos == npg * PAGE_SIZE
    per_b = lambda shape: pl.BlockSpec((None,) + shape, lambda b, pg, pt: (b, 0, 0))
    return pl.pallas_call(
        functools.partial(_dec_attend_kernel, qpos, wlen),
        grid_spec=_paged_grid_spec(
            (ns, npg),
            [per_b((N_HEADS, KVW)), per_b((N_HEADS, LANES)),
             pl.BlockSpec((None, PAGE_SIZE, 2 * KVW), lambda b, pg, pt: (pt[b * npg + pg], 0, 0)),
             per_b((wlen, 2 * KVW)), per_b((1, 2 * KVW)), per_b((1, 2 * KVW)), per_b((N_HEADS, KVW)),
             per_b((N_HEADS, LANES)), pl.BlockSpec((N_HEADS, 1), lambda b, pg, pt: (0, 0))],
            [per_b((N_HEADS, KVW)), per_b((wlen, 2 * KVW))],
            [pltpu.VMEM((N_HEADS, 1), F32), pltpu.VMEM((N_HEADS, 1), F32), pltpu.VMEM((N_HEADS, KVW), F32),
             pltpu.VMEM((N_HEADS, KVW), F32)]),
        out_shape=[jax.ShapeDtypeStruct((ns, N_HEADS, KVW), F32), jax.ShapeDtypeStruct(win_buf.shape, F32)],
        compiler_params=_cparams("parallel", "arbitrary"),
        name="nsa_decode_attend",
    )(page_table.reshape(-1), qbd, selb, cache, win_buf, kvs_new.reshape(ns, 1, 2 * KVW),
      kvw_new.reshape(ns, 1, 2 * KVW), oc, gates, _dec_slopes())


def _nsa_sample(rows, x, g, mod, w_in, w_ck, w_cv, cache_c, cache_s, win_buf, page_table):
    ns = rows.nseq
    qpos = page_table.shape[1] * PAGE_SIZE
    q, kvc, _, kvs, _, kvw, _, gates = _norm_proj(
        rows, x, g, mod, (0, 1), _nsa_weights(w_in), jnp.asarray(NO_BIAS), NSA_PROJ_OUTS)
    n_pool = cache_c.shape[0]
    kcv = _cmp_decode(page_table, cache_c.reshape(n_pool, PAGE_SIZE, 2 * KVW), w_ck, w_cv)
    q4 = q.reshape(ns, NSA_KV_HEADS, NSA_GROUP, HEAD_DIM)
    eye = jnp.eye(NSA_KV_HEADS, dtype=q.dtype)
    qbd = jnp.einsum("bkgd,kj->bgkjd", q4, eye).reshape(ns, N_HEADS, KVW)
    oc, selb = _nsa_decode_select(qbd, kcv, qpos)
    g4 = gates.reshape(ns, NSA_KV_HEADS, LANES)[:, :, :NSA_GROUP * 3].reshape(ns, NSA_KV_HEADS, NSA_GROUP, 3)
    g_rows = jnp.pad(g4.transpose(0, 2, 1, 3).reshape(ns, N_HEADS, 3), ((0, 0), (0, 0), (0, LANES - 3)))
    wlen = win_buf.shape[1]
    o_bd, win_out = _nsa_decode_attend(
        page_table, qbd, selb, cache_s.reshape(n_pool, PAGE_SIZE, 2 * KVW), win_buf.reshape(ns, wlen, 2 * KVW),
        kvs, kvw, oc, g_rows, qpos)
    o5 = o_bd.reshape(ns, NSA_GROUP, NSA_KV_HEADS, NSA_KV_HEADS, HEAD_DIM)
    o = jnp.einsum("bgkkd->bkgd", o5).reshape(ns, D_MODEL).astype(BF16)
    return o, kvc, kvs, win_out.reshape(win_buf.shape)


def _fox_dec_kernel(pt_ref, q_ref, kv_ref, lf_ref, kvn_ref, lfn_ref, o_ref, qbd_scr, cum_scr, m_scr, l_scr, acc_scr):
    pg, npg = pl.program_id(1), pl.num_programs(1)
    own = (lax.broadcasted_iota(jnp.int32, (FOX_HEADS, D_MODEL), 1) // HEAD_DIM
           == lax.broadcasted_iota(jnp.int32, (FOX_HEADS, D_MODEL), 0))

    @pl.when(pg == 0)
    def _():
        q = jnp.broadcast_to(q_ref[...].astype(F32), (FOX_HEADS, D_MODEL))
        qbd_scr[...] = jnp.where(own, q, 0.0).astype(BF16)
        cum_scr[...] = jnp.zeros(cum_scr.shape, F32)
        m_scr[...] = jnp.full(m_scr.shape, -jnp.inf, F32)
        l_scr[...] = jnp.zeros(l_scr.shape, F32)
        acc_scr[...] = jnp.zeros(acc_scr.shape, F32)

    kv = kv_ref[...]
    lf = lf_ref[...]
    lf_t = jnp.concatenate([lf, jnp.zeros((PAGE_SIZE, LANES - FOX_HEADS), F32)], axis=1).T[0:FOX_HEADS]
    upper = (lax.broadcasted_iota(jnp.int32, (PAGE_SIZE, PAGE_SIZE), 0)
             <= lax.broadcasted_iota(jnp.int32, (PAGE_SIZE, PAGE_SIZE), 1)).astype(BF16)
    a, b, c = _split3(lf_t)
    cum = cum_scr[...] + (_bdot(a, upper) + _bdot(b, upper) + _bdot(c, upper))
    cum_scr[...] = cum[:, PAGE_SIZE - 1:PAGE_SIZE]
    s = _dot_nt(qbd_scr[...], kv[:, 0:D_MODEL].astype(BF16)) - cum
    _softmax_step(s, kv[:, D_MODEL:2 * D_MODEL].astype(BF16), m_scr, l_scr, acc_scr)

    @pl.when(pg == npg - 1)
    def _():
        kvn = kvn_ref[...]
        s_new = _dot_nt(qbd_scr[...], _new_key_tile(kvn[:, 0:D_MODEL])) - (cum_scr[...] + lfn_ref[...])
        _softmax_step(_new_key_mask(s_new), _new_key_tile(kvn[:, D_MODEL:2 * D_MODEL]), m_scr, l_scr, acc_scr)
        o = acc_scr[...] / l_scr[...]
        o_ref[...] = jnp.sum(jnp.where(own, o, 0.0), axis=0, keepdims=True).astype(o_ref.dtype)


def _fox_decode(page_table, q, cache_kv, cache_logf, kv_new, logf_new):
    ns, npg = page_table.shape
    per_b = lambda shape: pl.BlockSpec((None,) + shape, lambda b, pg, pt: (b, 0, 0))
    paged = lambda width: pl.BlockSpec((None, PAGE_SIZE, width), lambda b, pg, pt: (pt[b * npg + pg], 0, 0))
    out = pl.pallas_call(
        _fox_dec_kernel,
        grid_spec=_paged_grid_spec(
            (ns, npg),
            [per_b((1, D_MODEL)), paged(2 * D_MODEL), paged(FOX_HEADS), per_b((1, 2 * D_MODEL)), per_b((FOX_HEADS, 1))],
            per_b((1, D_MODEL)),
            [pltpu.VMEM((FOX_HEADS, D_MODEL), BF16), pltpu.VMEM((FOX_HEADS, 1), F32), pltpu.VMEM((FOX_HEADS, 1), F32),
             pltpu.VMEM((FOX_HEADS, 1), F32), pltpu.VMEM((FOX_HEADS, D_MODEL), F32)]),
        out_shape=jax.ShapeDtypeStruct((ns, 1, D_MODEL), BF16),
        compiler_params=_cparams("parallel", "arbitrary"),
        name="fox_decode",
    )(page_table.reshape(-1), q.reshape(ns, 1, D_MODEL), cache_kv, cache_logf,
      kv_new.reshape(ns, 1, 2 * D_MODEL), logf_new.reshape(ns, FOX_HEADS, 1))
    return out.reshape(ns, D_MODEL)


def _fox_sample(rows, x, g, mod, w_in, b_f, cache_kv, cache_logf, page_table):
    w, bias = _fox_weights(w_in, b_f)
    q, kv, _, logf = _norm_proj(rows, x, g, mod, (0, 1), w, bias, FOX_PROJ_OUTS)
    logf = logf[:, :FOX_HEADS]
    n_pool = cache_kv.shape[0]
    o = _fox_decode(page_table, q, cache_kv.reshape(n_pool, PAGE_SIZE, 2 * D_MODEL), cache_logf, kv, logf)
    return o, kv, logf


PROMPT_TM = 512
POOL_TM = 256


def kernel(x_prompt, x_sample, cache_l0_cmp_kv, cache_l0_sel_kv, state_l0_win_kv, cache_l1_kv, cache_l1_logf, state_l2_pool, cache_l3_cmp_kv, cache_l3_sel_kv, state_l3_win_kv, state_ffn_conv, page_table, c_prompt, c_sample, mod_w, mod_b, norm_g, l0_nsa_w_in, l0_nsa_w_ck, l0_nsa_w_cv, l0_nsa_w_o, l1_fox_w_in, l1_fox_b_f, l1_fox_w_o, l2_pool_w, l2_pool_b, l2_pool_scale, l3_nsa_w_in, l3_nsa_w_ck, l3_nsa_w_cv, l3_nsa_w_o, ffn_w_gu, ffn_conv_w, ffn_conv_b, ffn_w_d):
    b, t, _ = x_prompt.shape
    ns = x_sample.shape[0]
    past_len = page_table.shape[1] * PAGE_SIZE
    rp = _Rows(b, t, min(PROMPT_TM, t))
    rs = _Rows(ns, 1, ns)
    nsa = {0: (cache_l0_cmp_kv, cache_l0_sel_kv, state_l0_win_kv, l0_nsa_w_in, l0_nsa_w_ck, l0_nsa_w_cv, l0_nsa_w_o),
           3: (cache_l3_cmp_kv, cache_l3_sel_kv, state_l3_win_kv, l3_nsa_w_in, l3_nsa_w_ck, l3_nsa_w_cv, l3_nsa_w_o)}

    c_all = jnp.concatenate([c_prompt, c_sample], axis=0)
    c_all = jnp.pad(c_all, ((0, -c_all.shape[0] % SUBLANES), (0, 0)))
    mod = _modulation(c_all, mod_w, mod_b)

    xp = x_prompt.reshape(b * t, D_MODEL)
    xs = x_sample.reshape(ns, D_MODEL)
    st = {}
    conv_p, conv_s = [], []
    kv5 = lambda a, n: a.reshape(n, -1, 2, NSA_KV_HEADS, HEAD_DIM)
    for i in range(DEPTH):
        mp, ms = mod[i, :b], mod[i, b:b + ns]
        g = norm_g[i]
        kind = i % 3
        if kind == 0:
            c_c, c_s, s_w, w_in, w_ck, w_cv, w_o = nsa[i]
            op, kvc, kvs, kvw = _nsa_prompt(rp, xp, g[0], mp, w_in, w_ck, w_cv)
            os_, kvc_s, kvs_s, win_s = _nsa_sample(rs, xs, g[0], ms, w_in, w_ck, w_cv, c_c, c_s, s_w, page_table)
            st[i] = (kv5(kvc, b), kv5(kvc_s, ns), kv5(kvs, b), kv5(kvs_s, ns),
                     kv5(kvw, b)[:, -min(NSA_WINDOW, t):], win_s)
            w_ob = w_o.astype(BF16)
            xp = _out_proj_residual(rp, op, w_ob, xp, g[1], mp, 2)
            xs = _out_proj_residual(rs, os_, w_ob, xs, g[1], ms, 2)
        elif kind == 1:
            op, kv_p, lf_p = _fox_prompt(rp, xp, g[0], mp, l1_fox_w_in, l1_fox_b_f)
            os_, kv_s, lf_s = _fox_sample(rs, xs, g[0], ms, l1_fox_w_in, l1_fox_b_f, cache_l1_kv, cache_l1_logf, page_table)
            st[i] = (kv_p.reshape(b, t, 2, FOX_HEADS, HEAD_DIM), kv_s.reshape(ns, 1, 2, FOX_HEADS, HEAD_DIM),
                     lf_p.reshape(b, t, FOX_HEADS), lf_s.reshape(ns, 1, FOX_HEADS))
            w_ob = l1_fox_w_o.astype(BF16)
            xp = _out_proj_residual(rp, op, w_ob, xp, g[1], mp, 2)
            xs = _out_proj_residual(rs, os_, w_ob, xs, g[1], ms, 2)
        else:
            hp = _norm_only(rp, xp, g[0], mp, (0, 1))
            hs = _norm_only(rs, xs, g[0], ms, (0, 1))
            yp = _pool_mix(hp, b, t, min(POOL_TM, t), 0, l2_pool_w, l2_pool_b.reshape(-1), l2_pool_scale)
            ext = jnp.concatenate([state_l2_pool, hs[:, None, :]], axis=1)
            n_ext = POOL_STATE + 1
            ys = _pool_mix(ext.reshape(ns * n_ext, D_MODEL), ns, n_ext, n_ext, past_len - POOL_STATE,
                           l2_pool_w, l2_pool_b.reshape(-1), l2_pool_scale).reshape(ns, n_ext, D_MODEL)[:, -1]
            st[i] = (hp.reshape(b, t, D_MODEL)[:, -POOL_STATE:], ext[:, -POOL_STATE:])
            xp = _gated_residual(rp, yp, xp, g[1], mp, 2)
            xs = _gated_residual(rs, ys, xs, g[1], ms, 2)
        w_gu, w_d = ffn_w_gu[i].astype(BF16), ffn_w_d[i].astype(BF16)
        xp, cp = _conv_ffn(rp, xp, g[2], g[3], mp, w_gu, ffn_conv_w[i], ffn_conv_b[i], w_d, None)
        xs, cs = _conv_ffn(rs, xs, g[2], g[3], ms, w_gu, ffn_conv_w[i], ffn_conv_b[i], w_d, state_ffn_conv[i])
        conv_p.append(cp)
        conv_s.append(cs)
    return (xp.reshape(b, t, D_MODEL), xs.reshape(ns, 1, D_MODEL),
            *st[0], *st[1], *st[2], *st[3],
            jnp.stack(conv_p), jnp.stack(conv_s))
```

```python
import functools

import jax
import jax.numpy as jnp
import numpy as np
from jax import lax
from jax.experimental import pallas as pl
from jax.experimental.pallas import tpu as pltpu

D_MODEL = 1024
DEPTH = 4
PAGE_SIZE = 128
HEAD_DIM = 64
N_HEADS = D_MODEL // HEAD_DIM
NSA_KV_HEADS = 4
NSA_GROUP = N_HEADS // NSA_KV_HEADS
NSA_BLOCK = 64
NSA_TOPN = 16
NSA_WINDOW = 512
FOX_HEADS = D_MODEL // HEAD_DIM
POOL_WINDOWS = (2, 4, 8, 16)
POOL_GROUP_DIM = D_MODEL // len(POOL_WINDOWS)
POOL_STATE = max(POOL_WINDOWS) - 1
D_FF = 2816
CONV_W = 3
N_MOD = 6
RMS_EPS = 1e-6
NEG = -1e30
BIG = 1e30
ATTN_SCALE = HEAD_DIM ** -0.5

LANES = 128
SUBLANES = 8
VMEM_LIMIT = 56 * 1024 * 1024

KVW = NSA_KV_HEADS * HEAD_DIM
NSA_Q0, NSA_C0, NSA_S0, NSA_W0, NSA_G0 = 0, D_MODEL, D_MODEL + 2 * KVW, D_MODEL + 4 * KVW, D_MODEL + 6 * KVW
NSA_COLS = NSA_G0 + NSA_KV_HEADS * LANES
FOX_LF0 = 3 * D_MODEL
FOX_COLS = FOX_LF0 + LANES

F32 = jnp.float32
BF16 = jnp.bfloat16


def _cparams(*sem):
    return pltpu.CompilerParams(dimension_semantics=sem, vmem_limit_bytes=VMEM_LIMIT)


def _bdot(a, b):
    return jnp.dot(a, b, preferred_element_type=F32)


def _dot_nt(a, b):
    return lax.dot_general(a, b, (((1,), (1,)), ((), ())), preferred_element_type=F32)


def _rms_rows(x, g):
    return x * lax.rsqrt(jnp.mean(x * x, axis=-1, keepdims=True) + RMS_EPS) * g


def _split3(x):
    a = x.astype(BF16)
    r = x - a.astype(F32)
    b = r.astype(BF16)
    c = (r - b.astype(F32)).astype(BF16)
    return a, b, c


def _mod_kernel(c_ref, w_ref, b_ref, o_ref):
    c = c_ref[...]
    a = (c * jax.nn.sigmoid(c)).astype(BF16)
    o_ref[...] = _bdot(a, w_ref[...].astype(BF16)) + b_ref[...]


def _modulation(c_all, mod_w, mod_b):
    rows = c_all.shape[0]
    n = N_MOD * D_MODEL
    tn = D_MODEL
    return pl.pallas_call(
        _mod_kernel,
        grid=(DEPTH, n // tn),
        in_specs=[pl.BlockSpec((rows, D_MODEL), lambda i, j: (0, 0)),
                  pl.BlockSpec((None, D_MODEL, tn), lambda i, j: (i, 0, j)),
                  pl.BlockSpec((None, 1, tn), lambda i, j: (i, 0, j))],
        out_specs=pl.BlockSpec((None, rows, tn), lambda i, j: (i, 0, j)),
        out_shape=jax.ShapeDtypeStruct((DEPTH, rows, n), F32),
        compiler_params=_cparams("parallel", "parallel"),
        name="modulation",
    )(c_all, mod_w, mod_b.reshape(DEPTH, 1, n))


def _norm_h(x, g, shift, scale):
    return _rms_rows(x, g) * (1.0 + scale) + shift


class _Rows:
    def __init__(self, nseq, t, tm):
        assert t % tm == 0 or t == 1
        self.nseq, self.t = nseq, t
        self.decode = t == 1
        self.tm = nseq if self.decode else tm
        self.m = nseq * t
        self.tiles_per_seq = 1 if self.decode else t // tm
        self.ntiles = self.m // self.tm

    def mod_spec(self, chunk):
        if self.decode:
            return pl.BlockSpec((self.tm, D_MODEL), lambda i, *_: (0, chunk))
        tps = self.tiles_per_seq
        return pl.BlockSpec((None, 1, D_MODEL), lambda i, *_: (i // tps, 0, chunk))

    def mod_arr(self, mod):
        return mod if self.decode else mod.reshape(self.nseq, 1, N_MOD * D_MODEL)

    def row_spec(self, width, col=0):
        return pl.BlockSpec((self.tm, width), lambda i, *_: (i, col))


def _proj_kernel(outs, x_ref, g_ref, sh_ref, sc_ref, w_ref, bias_ref, *o_refs):
    h = _norm_h(x_ref[...], g_ref[...], sh_ref[...], sc_ref[...])
    hb = h.astype(BF16)
    refs = iter(o_refs)
    for c0, c1, kind, dtypes in outs:
        if kind == "h":
            p = h
        else:
            p = _bdot(hb, w_ref[:, c0:c1])
        if kind == "scale":
            p = p * ATTN_SCALE
        elif kind == "sigmoid":
            p = jax.nn.sigmoid(p)
        elif kind == "logsigmoid":
            p = jax.nn.log_sigmoid(p + bias_ref[...])
        for dt in dtypes:
            next(refs)[...] = p.astype(dt)


def _norm_proj(rows, x, g, mod, chunks, w, bias, outs):
    ncols = w.shape[1]
    kern = functools.partial(_proj_kernel, outs)
    full = lambda shape: pl.BlockSpec(shape, lambda i: (0,) * len(shape))
    marr = rows.mod_arr(mod)
    flat = [(c1 - c0, dt) for c0, c1, _, dts in outs for dt in dts]
    return pl.pallas_call(
        kern,
        grid=(rows.ntiles,),
        in_specs=[rows.row_spec(D_MODEL), full((1, D_MODEL)),
                  rows.mod_spec(chunks[0]), rows.mod_spec(chunks[1]),
                  full((D_MODEL, ncols)), full((1, LANES))],
        out_specs=[rows.row_spec(wd) for wd, _ in flat],
        out_shape=[jax.ShapeDtypeStruct((rows.m, wd), dt) for wd, dt in flat],
        compiler_params=_cparams("parallel"),
        name="norm_proj",
    )(x, g.reshape(1, D_MODEL), marr, marr, w, bias)


def _oproj_kernel(o_ref, w_ref, x_ref, g_ref, gate_ref, y_ref):
    y = _bdot(o_ref[...], w_ref[...])
    y_ref[...] = x_ref[...] + gate_ref[...] * _rms_rows(y, g_ref[...])


def _out_proj_residual(rows, o, w, x, g, mod, gate_chunk):
    k = w.shape[0]
    full = lambda shape: pl.BlockSpec(shape, lambda i: (0,) * len(shape))
    return pl.pallas_call(
        _oproj_kernel,
        grid=(rows.ntiles,),
        in_specs=[rows.row_spec(k), full((k, D_MODEL)), rows.row_spec(D_MODEL),
                  full((1, D_MODEL)), rows.mod_spec(gate_chunk)],
        out_specs=rows.row_spec(D_MODEL),
        out_shape=jax.ShapeDtypeStruct((rows.m, D_MODEL), F32),
        compiler_params=_cparams("parallel"),
        name="out_proj_residual",
    )(o, w, x, g.reshape(1, D_MODEL), rows.mod_arr(mod))


FFN_TF = 256
FFN_NJ = D_FF // FFN_TF
HALO = SUBLANES


def _ffn_kernel(decode, tiles_per_seq, x_ref, g2_ref, sh_ref, sc_ref, wg_ref, wu_ref, cw_ref, cb_ref,
                wd_ref, g3_ref, gate_ref, p0_ref, p1_ref, y_ref, st_ref, h_scr, acc_scr, gs_scr, carry_scr):
    i, j = pl.program_id(0), pl.program_id(1)
    tm = x_ref.shape[0]

    @pl.when(j == 0)
    def _():
        h_scr[...] = _norm_h(x_ref[...], g2_ref[...], sh_ref[...], sc_ref[...]).astype(BF16)
        acc_scr[...] = jnp.zeros_like(acc_scr)

    h = h_scr[...]
    gcol = _bdot(h, wg_ref[...])
    ucol = _bdot(h, wu_ref[...])
    cw = cw_ref[...]
    if decode:
        a = cb_ref[...] + cw[0:1] * p0_ref[...] + cw[1:2] * p1_ref[...] + cw[2:3] * gcol
        st_ref[...] = gcol
    else:
        first = (i % tiles_per_seq) == 0

        @pl.when(first)
        def _():
            gs_scr[0:HALO, :] = jnp.zeros((HALO, gs_scr.shape[1]), F32)

        @pl.when(jnp.logical_not(first))
        def _():
            gs_scr[0:HALO, :] = carry_scr[j]

        gs_scr[HALO:HALO + tm, :] = gcol
        a = (cb_ref[...] + cw[0:1] * gs_scr[HALO - 2:HALO - 2 + tm, :]
             + cw[1:2] * gs_scr[HALO - 1:HALO - 1 + tm, :] + cw[2:3] * gcol)
        tail = gcol[tm - HALO:tm, :]
        carry_scr[j] = tail
        st_ref[...] = tail
    act = (a * jax.nn.sigmoid(a) * ucol).astype(BF16)
    acc_scr[...] += _bdot(act, wd_ref[...])

    @pl.when(j == pl.num_programs(1) - 1)
    def _():
        y_ref[...] = x_ref[...] + gate_ref[...] * _rms_rows(acc_scr[...], g3_ref[...])


def _conv_ffn(rows, x, g2, g3, mod, w_gu, conv_w, conv_b, w_d, prev):
    tm, tf = rows.tm, FFN_TF
    decode = rows.decode
    tps = rows.tiles_per_seq
    kern = functools.partial(_ffn_kernel, decode, tps)
    full = lambda shape: pl.BlockSpec(shape, lambda i, j: (0,) * len(shape))
    col = lambda r: pl.BlockSpec((r, tf), lambda i, j: (0, j))
    if decode:
        p0, p1 = prev[:, 0], prev[:, 1]
        pspec = pl.BlockSpec((tm, tf), lambda i, j: (0, j))
        st_spec = pl.BlockSpec((tm, tf), lambda i, j: (0, j))
        st_shape = jax.ShapeDtypeStruct((rows.m, D_FF), F32)
    else:
        p0 = p1 = jnp.zeros((SUBLANES, LANES), F32)
        pspec = full((SUBLANES, LANES))
        st_spec = pl.BlockSpec((None, HALO, tf), lambda i, j: (i, 0, j))
        st_shape = jax.ShapeDtypeStruct((rows.ntiles, HALO, D_FF), F32)
    marr = rows.mod_arr(mod)
    y, st = pl.pallas_call(
        kern,
        grid=(rows.ntiles, FFN_NJ),
        in_specs=[rows.row_spec(D_MODEL), full((1, D_MODEL)), rows.mod_spec(3), rows.mod_spec(4),
                  pl.BlockSpec((D_MODEL, tf), lambda i, j: (0, j)),
                  pl.BlockSpec((D_MODEL, tf), lambda i, j: (0, FFN_NJ + j)),
                  col(CONV_W), col(1),
                  pl.BlockSpec((tf, D_MODEL), lambda i, j: (j, 0)),
                  full((1, D_MODEL)), rows.mod_spec(5), pspec, pspec],
        out_specs=[rows.row_spec(D_MODEL), st_spec],
        out_shape=[jax.ShapeDtypeStruct((rows.m, D_MODEL), F32), st_shape],
        scratch_shapes=[pltpu.VMEM((tm, D_MODEL), BF16), pltpu.VMEM((tm, D_MODEL), F32),
                        pltpu.VMEM((HALO + tm, tf), F32), pltpu.VMEM((FFN_NJ, HALO, tf), F32)],
        compiler_params=_cparams("arbitrary", "arbitrary"),
        name="conv_ffn",
    )(x, g2.reshape(1, D_MODEL), marr, marr, w_gu, w_gu, conv_w, conv_b.reshape(1, D_FF),
      w_d, g3.reshape(1, D_MODEL), marr, p0, p1)
    if decode:
        return y, jnp.stack([prev[:, 1], st], axis=1)
    last = st.reshape(rows.nseq, tps, HALO, D_FF)[:, -1]
    return y, last[:, HALO - (CONV_W - 1):]


def _mm_kernel(a_ref, w_ref, o_ref):
    o_ref[...] = _bdot(a_ref[...], w_ref[...])


def _matmul(a, w, tm):
    m, k = a.shape
    n = w.shape[1]
    return pl.pallas_call(
        _mm_kernel,
        grid=(m // tm,),
        in_specs=[pl.BlockSpec((tm, k), lambda i: (i, 0)), pl.BlockSpec((k, n), lambda i: (0, 0))],
        out_specs=pl.BlockSpec((tm, n), lambda i: (i, 0)),
        out_shape=jax.ShapeDtypeStruct((m, n), F32),
        compiler_params=_cparams("parallel"),
        name="matmul",
    )(a, w)


Q_TILE = 128
KEY_CHUNK = 128
GQ = NSA_GROUP * Q_TILE
SEL_UNROLL = 4
M_INIT = -3e38


def _alibi_slopes():
    h = np.arange(1, N_HEADS + 1, dtype=np.float32)
    return np.exp2(-8.0 * h / N_HEADS).astype(np.float32).reshape(NSA_KV_HEADS, NSA_GROUP)


def _half_masks():
    lane = lax.broadcasted_iota(jnp.int32, (Q_TILE, LANES), 1)
    return lane < HEAD_DIM


def _stack_group_heads(q, lo, par=None):
    tiles = []
    for g in range(NSA_GROUP):
        t = q[:, (g // 2) * LANES:(g // 2 + 1) * LANES]
        keep = lo if g % 2 == 0 else jnp.logical_not(lo)
        t = jnp.where(keep, t, jnp.zeros_like(t))
        if par is not None:
            rolled = pltpu.roll(t.astype(F32), HEAD_DIM, axis=1).astype(BF16)
            t = jnp.where(par == g % 2, t, rolled)
        tiles.append(t)
    return jnp.concatenate(tiles, axis=0)


def _nsa_cmp_kernel(nb, q_ref, kcc_ref, vcc_ref, slope_ref, oc_ref, selb_ref, any_ref):
    j = pl.program_id(2)
    q0 = j * Q_TILE
    lo = _half_masks()
    qs = _stack_group_heads(q_ref[...], lo)
    st = _dot_nt(kcc_ref[...], qs)
    n_i = lax.broadcasted_iota(jnp.int32, (nb, GQ), 0)
    qpos = q0 + (lax.broadcasted_iota(jnp.int32, (nb, GQ), 1) & (Q_TILE - 1))
    ends = n_i * NSA_BLOCK + (NSA_BLOCK - 1)
    mask = ends <= qpos
    s = st - slope_ref[...] * (qpos - ends).astype(F32)
    s = jnp.where(mask, s, NEG)
    e = jnp.exp(s - jnp.max(s, axis=0, keepdims=True))
    p = jnp.where(mask, e / jnp.sum(e, axis=0, keepdims=True), 0.0)
    acc = _bdot(p.T.astype(BF16), vcc_ref[...])
    for t in range(NSA_GROUP // 2):
        a0 = acc[(2 * t) * Q_TILE:(2 * t + 1) * Q_TILE]
        a1 = acc[(2 * t + 1) * Q_TILE:(2 * t + 2) * Q_TILE]
        oc_ref[:, t * LANES:(t + 1) * LANES] = jnp.where(lo, a0, a1)

    imp = p[:, 0:Q_TILE]
    for g in range(1, NSA_GROUP):
        imp = imp + p[:, g * Q_TILE:(g + 1) * Q_TILE]
    n2 = lax.broadcasted_iota(jnp.int32, (nb, Q_TILE), 0)
    cur = (q0 + lax.broadcasted_iota(jnp.int32, (nb, Q_TILE), 1)) // NSA_BLOCK
    forced = (n2 == 0) | (n2 == cur) | (n2 == cur - 1)
    score = jnp.where(forced, BIG, jnp.where(n2 <= cur, imp, NEG))

    def pick_next(_, carry):
        sc, sel = carry
        m = jnp.max(sc, axis=0, keepdims=True)
        first = jnp.min(jnp.where(sc == m, n2, nb), axis=0, keepdims=True)
        pick = n2 == first
        sel = jnp.where(pick & (m > 0.5 * NEG), 1.0, sel)
        return jnp.where(pick, -jnp.inf, sc), sel

    _, sel = lax.fori_loop(0, min(NSA_TOPN, nb), pick_next, (score, jnp.zeros((nb, Q_TILE), F32)), unroll=True)
    sel_t = sel.T
    if nb < LANES:
        sel_t = jnp.concatenate([sel_t, jnp.zeros((Q_TILE, LANES - nb), F32)], axis=1)
    selb_ref[...] = jnp.where(sel_t > 0.0, 0.0, NEG).astype(BF16)
    any_ref[...] = jnp.max(sel_t, axis=0, keepdims=True)


def _nsa_compressed(b, t, q, kcc, vcc):
    nb = t // NSA_BLOCK
    nq = t // Q_TILE
    slopes = jnp.asarray(np.repeat(_alibi_slopes(), Q_TILE, axis=1).reshape(NSA_KV_HEADS, 1, GQ))
    return pl.pallas_call(
        functools.partial(_nsa_cmp_kernel, nb),
        grid=(b, NSA_KV_HEADS, nq),
        in_specs=[pl.BlockSpec((Q_TILE, KVW), lambda bi, k, j: (bi * nq + j, k)),
                  pl.BlockSpec((None, None, nb, LANES), lambda bi, k, j: (bi, k, 0, 0)),
                  pl.BlockSpec((None, None, nb, LANES), lambda bi, k, j: (bi, k, 0, 0)),
                  pl.BlockSpec((None, 1, GQ), lambda bi, k, j: (k, 0, 0))],
        out_specs=[pl.BlockSpec((Q_TILE, KVW), lambda bi, k, j: (bi * nq + j, k)),
                   pl.BlockSpec((None, None, Q_TILE, LANES), lambda bi, k, j: (bi, k, j, 0)),
                   pl.BlockSpec((None, None, None, 1, LANES), lambda bi, k, j: (bi, k, j, 0, 0))],
        out_shape=[jax.ShapeDtypeStruct((b * t, D_MODEL), F32),
                   jax.ShapeDtypeStruct((b, NSA_KV_HEADS, t, LANES), BF16),
                   jax.ShapeDtypeStruct((b, NSA_KV_HEADS, nq, 1, LANES), F32)],
        compiler_params=_cparams("parallel", "parallel", "parallel"),
        name="nsa_compressed",
    )(q, kcc, vcc, slopes)


def _softmax_step(s, v, m_ref, l_ref, acc_ref):
    m_prev = m_ref[...]
    m_new = jnp.maximum(m_prev, jnp.max(s, axis=-1, keepdims=True))
    alpha = jnp.exp(m_prev - m_new)
    p = jnp.exp(s - m_new)
    l_ref[...] = alpha * l_ref[...] + jnp.sum(p, axis=-1, keepdims=True)
    acc_ref[...] = alpha * acc_ref[...] + _bdot(p.astype(BF16), v)
    m_ref[...] = m_new


def _nsa_sw_kernel(nq, flags_ref, q_ref, selb_ref, ks_ref, vst_ref, kw_ref, vwt_ref, posx_ref, onehot_ref, qx_ref,
                   oc_ref, gate_ref, slope_ref, o_ref, ksa_scr, kwa_scr, qaug_scr, ms_scr, as_scr, mw_scr, aw_scr, list_scr):
    k, j = pl.program_id(1), pl.program_id(2)
    par = k % 2
    lane_b = lax.broadcasted_iota(jnp.int32, (KEY_CHUNK, LANES), 1)

    @pl.when(j == 0)
    def _():
        def fill(c, carry):
            r0 = pl.multiple_of(c * KEY_CHUNK, KEY_CHUNK)
            own = (lane_b // HEAD_DIM) == par
            px = posx_ref[...]
            ksa_scr[pl.ds(r0, KEY_CHUNK), 0:LANES] = onehot_ref[pl.ds(r0, KEY_CHUNK), :]
            ksa_scr[pl.ds(r0, KEY_CHUNK), LANES:2 * LANES] = jnp.where(own, ks_ref[pl.ds(r0, KEY_CHUNK), :], px)
            kwa_scr[pl.ds(r0, KEY_CHUNK), :] = jnp.where(own, kw_ref[pl.ds(r0, KEY_CHUNK), :], px)
            return carry

        lax.fori_loop(0, nq, fill, 0)

    lo = _half_masks()
    qq = _stack_group_heads(q_ref[...], lo, par)
    own_q = (lax.broadcasted_iota(jnp.int32, (GQ, LANES), 1) // HEAD_DIM) == par
    qaug_scr[:, 0:LANES] = jnp.concatenate([selb_ref[...]] * NSA_GROUP, axis=0)
    qaug_scr[:, LANES:2 * LANES] = jnp.where(own_q, qq, qx_ref[...])
    slope = slope_ref[...]
    for m_ref, a_ref in ((ms_scr, as_scr), (mw_scr, aw_scr)):
        m_ref[...] = jnp.full(m_ref.shape, M_INIT, F32)
        a_ref[...] = jnp.zeros(a_ref.shape, F32)

    key_l = lax.broadcasted_iota(jnp.int32, (KEY_CHUNK, GQ), 0)
    q_l = lax.broadcasted_iota(jnp.int32, (KEY_CHUNK, GQ), 1) & (Q_TILE - 1)
    own_v = (lax.broadcasted_iota(jnp.int32, (LANES, KEY_CHUNK), 0) // HEAD_DIM) == par

    def step(pieces, m_ref, a_ref):
        shifts = [jnp.where(live, slope * ((c - j) * KEY_CHUNK).astype(F32), -jnp.inf) for _, c, live, _ in pieces]
        m_prev = m_ref[...]
        m_new = m_prev
        for (st, _, _, _), shift in zip(pieces, shifts):
            m_new = jnp.maximum(m_new, jnp.max(st, axis=0, keepdims=True) + shift)
        p = jnp.concatenate([jnp.exp(st - (m_new - shift)).astype(BF16)
                             for (st, _, _, _), shift in zip(pieces, shifts)], axis=0)
        vo = jnp.concatenate([jnp.where(own_v, vt, 1.0).astype(BF16) for _, _, _, vt in pieces], axis=1)
        a_ref[...] = jnp.exp(m_prev - m_new) * a_ref[...] + _bdot(vo, p)
        m_ref[...] = m_new

    def sel_piece(c, live, diag=False):
        k0 = pl.multiple_of(c * KEY_CHUNK, KEY_CHUNK)
        st = _dot_nt(ksa_scr[pl.ds(k0, KEY_CHUNK), :], qaug_scr[...])
        if diag:
            st = jnp.where(key_l <= q_l, st, NEG)
        return st, c, live, vst_ref[:, pl.ds(k0, KEY_CHUNK)]

    word0 = ((pl.program_id(0) * NSA_KV_HEADS + k) * nq + j) * _flag_words(nq)

    def scan(c, n):
        act = (flags_ref[word0 + c // 32] >> (c % 32)) & 1

        @pl.when(act == 1)
        def _():
            list_scr[n] = c

        return n + act

    n_act = lax.fori_loop(0, j, scan, 0)

    def sel_body(i, carry):
        pieces = []
        for u in range(SEL_UNROLL):
            idx = i * SEL_UNROLL + u
            pieces.append(sel_piece(list_scr[jnp.minimum(idx, n_act - 1)], idx < n_act))
        step(pieces, ms_scr, as_scr)
        return carry

    lax.fori_loop(0, (n_act + SEL_UNROLL - 1) // SEL_UNROLL, sel_body, 0)
    step([sel_piece(j, True, diag=True)], ms_scr, as_scr)

    n_win = NSA_WINDOW // KEY_CHUNK
    pieces = []
    for dc in range(n_win + 1):
        c = j - n_win + dc
        cc = jnp.maximum(c, 0)
        k0 = pl.multiple_of(cc * KEY_CHUNK, KEY_CHUNK)
        st = _dot_nt(kwa_scr[pl.ds(k0, KEY_CHUNK), :], qaug_scr[:, LANES:2 * LANES])
        if dc == 0:
            st = jnp.where(key_l > q_l, st, NEG)
        elif dc == n_win:
            st = jnp.where(key_l <= q_l, st, NEG)
        pieces.append((st, cc, c >= 0, vwt_ref[:, pl.ds(k0, KEY_CHUNK)]))
    step(pieces, mw_scr, aw_scr)

    own0 = pl.multiple_of(par * HEAD_DIM, HEAD_DIM)
    oth0 = pl.multiple_of((1 - par) * HEAD_DIM, HEAD_DIM)
    o_s = as_scr[pl.ds(own0, HEAD_DIM), :] / as_scr[pl.ds(oth0, 1), :]
    o_w = aw_scr[pl.ds(own0, HEAD_DIM), :] / aw_scr[pl.ds(oth0, 1), :]
    gates_t = gate_ref[...].T
    oc_t = oc_ref[...].T
    mix = []
    for g in range(NSA_GROUP):
        cols = slice(g * Q_TILE, (g + 1) * Q_TILE)
        mix.append(gates_t[3 * g:3 * g + 1] * oc_t[g * HEAD_DIM:(g + 1) * HEAD_DIM]
                   + gates_t[3 * g + 1:3 * g + 2] * o_s[:, cols] + gates_t[3 * g + 2:3 * g + 3] * o_w[:, cols])
    o_ref[...] = jnp.concatenate(mix, axis=0).T.astype(o_ref.dtype)


def _flag_words(nchunks):
    return -(-nchunks // 32)


def _bf16_pieces(x):
    def rnd(v):
        return np.asarray(v, np.float32).astype(jnp.bfloat16).astype(np.float32)
    a = rnd(x)
    b = rnd(x - a)
    return a, b, rnd(x - a - b)


def _nsa_fold_constants(t):
    posx = np.zeros((KEY_CHUNK, LANES), np.float32)
    loc = np.arange(KEY_CHUNK, dtype=np.float32)
    slopes = _alibi_slopes()
    qx = np.zeros((NSA_KV_HEADS, GQ, LANES), np.float32)
    ql = np.tile(np.arange(Q_TILE, dtype=np.float32), NSA_GROUP)
    for k in range(NSA_KV_HEADS):
        srow = np.repeat(slopes[k], Q_TILE)
        pieces = _bf16_pieces(srow) + _bf16_pieces(-srow * ql)
        for base in (0, HEAD_DIM):
            for i, pc in enumerate(pieces):
                qx[k, :, base + i] = pc
    for base in (0, HEAD_DIM):
        posx[:, base:base + 3] = loc[:, None]
        posx[:, base + 3:base + 6] = 1.0
    onehot = (np.arange(t)[:, None] // NSA_BLOCK == np.arange(LANES)[None, :]).astype(np.float32)
    return jnp.asarray(posx, BF16), jnp.asarray(onehot, BF16), jnp.asarray(qx, BF16)


def _pack_chunk_flags(any_sel, nchunks):
    blocks_per_chunk = KEY_CHUNK // NSA_BLOCK
    f = any_sel[:, :, :, 0, :nchunks * blocks_per_chunk]
    f = f.reshape(f.shape[:3] + (nchunks, blocks_per_chunk)).max(axis=-1) > 0.0
    words = _flag_words(nchunks)
    f = jnp.pad(f, ((0, 0), (0, 0), (0, 0), (0, words * 32 - nchunks)))
    bits = f.reshape(f.shape[:3] + (words, 32)).astype(jnp.uint32) << jnp.arange(32, dtype=jnp.uint32)
    return lax.bitcast_convert_type(bits.sum(axis=-1, dtype=jnp.uint32), jnp.int32).reshape(-1)


def _nsa_selected_window(b, t, q, selb, any_sel, kvs, kvs_t, kvw, kvw_t, oc, gates):
    assert Q_TILE == KEY_CHUNK
    nq = t // Q_TILE
    slopes = jnp.asarray(np.repeat(_alibi_slopes(), Q_TILE, axis=1).reshape(NSA_KV_HEADS, 1, GQ))
    posx, onehot, qx = _nsa_fold_constants(t)
    flags = _pack_chunk_flags(any_sel, nq)
    pairs = KVW // LANES
    qspec = pl.BlockSpec((Q_TILE, KVW), lambda bi, k, j, fl: (bi * nq + j, k))
    kspec = pl.BlockSpec((None, t, LANES), lambda bi, k, j, fl: (bi, 0, k // 2))
    vspec = pl.BlockSpec((None, LANES, t), lambda bi, k, j, fl: (bi, pairs + k // 2, 0))
    const = lambda shape: pl.BlockSpec(shape, lambda bi, k, j, fl: (0,) * len(shape))
    return pl.pallas_call(
        functools.partial(_nsa_sw_kernel, nq),
        grid_spec=pltpu.PrefetchScalarGridSpec(
            num_scalar_prefetch=1,
            grid=(b, NSA_KV_HEADS, nq),
            in_specs=[qspec,
                      pl.BlockSpec((None, None, Q_TILE, LANES), lambda bi, k, j, fl: (bi, k, j, 0)),
                      kspec, vspec, kspec, vspec,
                      const((KEY_CHUNK, LANES)), const((t, LANES)),
                      pl.BlockSpec((None, GQ, LANES), lambda bi, k, j, fl: (k, 0, 0)),
                      qspec,
                      pl.BlockSpec((Q_TILE, LANES), lambda bi, k, j, fl: (bi * nq + j, k)),
                      pl.BlockSpec((None, 1, GQ), lambda bi, k, j, fl: (k, 0, 0))],
            out_specs=qspec,
            scratch_shapes=[pltpu.VMEM((t, 2 * LANES), BF16), pltpu.VMEM((t, LANES), BF16),
                            pltpu.VMEM((GQ, 2 * LANES), BF16),
                            pltpu.VMEM((1, GQ), F32), pltpu.VMEM((LANES, GQ), F32),
                            pltpu.VMEM((1, GQ), F32), pltpu.VMEM((LANES, GQ), F32),
                            pltpu.SMEM((nq,), jnp.int32)]),
        out_shape=jax.ShapeDtypeStruct((b * t, D_MODEL), BF16),
        compiler_params=_cparams("parallel", "arbitrary", "arbitrary"),
        name="nsa_selected_window",
    )(flags, q, selb, kvs, kvs_t, kvw, kvw_t, posx, onehot, qx, oc, gates, slopes)


def _nsa_weights(w_in):
    w = w_in.astype(BF16)
    ng = NSA_GROUP * 3
    gcols = [jnp.pad(w[:, NSA_G0 + k * ng:NSA_G0 + (k + 1) * ng], ((0, 0), (0, LANES - ng)))
             for k in range(NSA_KV_HEADS)]
    return jnp.concatenate([w[:, :NSA_G0]] + gcols, axis=1)


NSA_PROJ_OUTS = ((NSA_Q0, NSA_C0, "scale", (BF16,)),
                 (NSA_C0, NSA_S0, "", (F32, BF16)),
                 (NSA_S0, NSA_W0, "", (F32, BF16)),
                 (NSA_W0, NSA_G0, "", (F32, BF16)),
                 (NSA_G0, NSA_COLS, "sigmoid", (F32,)))
NO_BIAS = np.zeros((1, LANES), np.float32)


def _compress_blocks(kvc, nseq, nb, w_ck, w_cv):
    blocks = kvc.reshape(nseq, nb, NSA_BLOCK, 2, NSA_KV_HEADS, HEAD_DIM).transpose(3, 0, 1, 4, 2, 5)
    blocks = blocks.reshape(2, nseq * nb * NSA_KV_HEADS, NSA_BLOCK * HEAD_DIM)
    tm = min(256, blocks.shape[1])
    out = []
    for a, w in ((blocks[0], w_ck), (blocks[1], w_cv)):
        c = _matmul(a, w.astype(BF16), tm).reshape(nseq, nb, NSA_KV_HEADS, HEAD_DIM).transpose(0, 2, 1, 3)
        out.append(jnp.concatenate([c, c], axis=-1).astype(BF16))
    return out


def _nsa_prompt(rows, x, g, mod, w_in, w_ck, w_cv):
    b, t = rows.nseq, rows.t
    q, kvc, kvc_b, kvs, kvs_b, kvw, kvw_b, gates = _norm_proj(
        rows, x, g, mod, (0, 1), _nsa_weights(w_in), jnp.asarray(NO_BIAS), NSA_PROJ_OUTS)
    kcc, vcc = _compress_blocks(kvc_b, b, t // NSA_BLOCK, w_ck, w_cv)
    oc, selb, any_sel = _nsa_compressed(b, t, q, kcc, vcc)
    feat_major = lambda a: a.reshape(b, t, 2 * KVW).transpose(0, 2, 1)
    o = _nsa_selected_window(b, t, q, selb, any_sel, kvs_b.reshape(b, t, 2 * KVW), feat_major(kvs),
                             kvw_b.reshape(b, t, 2 * KVW), feat_major(kvw), oc, gates)
    return o, kvc, kvs, kvw


FOX_TQ = 512
FOX_TK = 512
CUM_TILE = 256
FOX_PAIRS = FOX_HEADS // 2

FOX_PROJ_OUTS = ((0, D_MODEL, "scale", (BF16,)),
                 (D_MODEL, FOX_LF0, "", (F32, BF16)),
                 (FOX_LF0, FOX_COLS, "logsigmoid", (F32,)))


def _fox_weights(w_in, b_f):
    w = jnp.pad(w_in.astype(BF16), ((0, 0), (0, FOX_COLS - w_in.shape[1])))
    bias = jnp.pad(b_f.astype(F32), (0, LANES - FOX_HEADS)).reshape(1, LANES)
    return w, bias


def _tri_cumsum(x):
    n = x.shape[0]
    tri = (lax.broadcasted_iota(jnp.int32, (n, n), 1) <= lax.broadcasted_iota(jnp.int32, (n, n), 0)).astype(BF16)
    a, b, c = _split3(x)
    return _bdot(tri, a) + _bdot(tri, b) + _bdot(tri, c)


def _cumsum_kernel(x_ref, o_ref, carry_scr):
    @pl.when(pl.program_id(1) == 0)
    def _():
        carry_scr[...] = jnp.zeros_like(carry_scr)

    cum = _tri_cumsum(x_ref[...]) + carry_scr[0:1, :]
    o_ref[...] = cum
    carry_scr[...] = jnp.broadcast_to(cum[-1:, :], carry_scr.shape)


def _cumsum_rows(x, nseq, t):
    nt = t // CUM_TILE
    return pl.pallas_call(
        _cumsum_kernel,
        grid=(nseq, nt),
        in_specs=[pl.BlockSpec((CUM_TILE, LANES), lambda s, i: (s * nt + i, 0))],
        out_specs=pl.BlockSpec((CUM_TILE, LANES), lambda s, i: (s * nt + i, 0)),
        out_shape=jax.ShapeDtypeStruct(x.shape, F32),
        scratch_shapes=[pltpu.VMEM((SUBLANES, LANES), F32)],
        compiler_params=_cparams("parallel", "arbitrary"),
        name="cumsum_rows",
    )(x)


def _fox_kernel(nk, q_ref, k_ref, vt_ref, ccol_ref, crow_ref, o_ref, ka_scr, base_scr, qa_scr, m_scr, acc_scr):
    qi = pl.program_id(2)
    tq = q_ref.shape[0]
    lane = lax.broadcasted_iota(jnp.int32, (FOX_TK, LANES), 1)

    def own(e):
        return (lane // HEAD_DIM) == e

    def other_lane(e, i):
        return lane == (1 - e) * HEAD_DIM + i

    @pl.when(qi == 0)
    def _():
        def fill(c, carry):
            k0 = pl.multiple_of(c * FOX_TK, FOX_TK)
            kc = k_ref[pl.ds(k0, FOX_TK), :].astype(F32)
            for e in range(2):
                col = ccol_ref[pl.ds(k0, FOX_TK), e:e + 1]
                base = col[0:1, :]
                ext = jnp.zeros((FOX_TK, LANES), F32)
                for i, piece in enumerate(_split3(base - col)):
                    ext = jnp.where(other_lane(e, i), piece.astype(F32), ext)
                ka_scr[e, pl.ds(k0, FOX_TK), :] = jnp.where(own(e), kc, ext).astype(BF16)
                base_scr[e, pl.ds(c, 1), :] = jnp.broadcast_to(base, (1, LANES))
            return carry

        lax.fori_loop(0, nk, fill, 0)

    q = q_ref[...].astype(F32)
    ones3 = jnp.zeros((tq, LANES), F32)
    for e in range(2):
        ext = ones3
        for i in range(3):
            ext = jnp.where(other_lane(e, i), 1.0, ext)
        qa_scr[e] = jnp.where(own(e), q, ext).astype(BF16)
    m_scr[...] = jnp.full(m_scr.shape, M_INIT, F32)
    acc_scr[...] = jnp.zeros(acc_scr.shape, F32)
    q0 = pl.multiple_of(qi * tq, tq)
    row = lax.broadcasted_iota(jnp.int32, (LANES, FOX_TK), 0)

    def chunk(c, diag):
        k0 = pl.multiple_of(c * FOX_TK, FOX_TK)
        vt = vt_ref[:, pl.ds(k0, FOX_TK)]
        for e in range(2):
            st = _dot_nt(ka_scr[e, pl.ds(k0, FOX_TK), :], qa_scr[e])
            if diag:
                st = jnp.where(lax.broadcasted_iota(jnp.int32, st.shape, 0)
                               <= lax.broadcasted_iota(jnp.int32, st.shape, 1), st, NEG)
            base = base_scr[e, pl.ds(c, 1), :]
            shift = crow_ref[e:e + 1, pl.ds(q0, tq)] - jnp.concatenate([base] * (tq // LANES), axis=1)
            m_prev = m_scr[e]
            m_new = jnp.maximum(m_prev, jnp.max(st, axis=0, keepdims=True) + shift)
            p = jnp.exp(st - (m_new - shift)).astype(BF16)
            vo = jnp.where((row // HEAD_DIM) == e, vt, 1.0).astype(BF16)
            acc_scr[e] = jnp.exp(m_prev - m_new) * acc_scr[e] + _bdot(vo, p)
            m_scr[e] = m_new

    def body(c, carry):
        chunk(c, False)
        return carry

    lax.fori_loop(0, qi, body, 0)
    chunk(qi, True)
    a0, a1 = acc_scr[0], acc_scr[1]
    r = lax.broadcasted_iota(jnp.int32, a0.shape, 0)
    o_t = jnp.where(r < HEAD_DIM, a0 / a0[HEAD_DIM:HEAD_DIM + 1], a1 / a1[0:1])
    o_ref[...] = o_t.T.astype(o_ref.dtype)


def _fox_attention(b, t, q, k, kv_t, cum_col, cum_row):
    assert FOX_TQ == FOX_TK and t % FOX_TQ == 0
    tq = FOX_TQ
    nq = t // tq
    vrow0 = D_MODEL // LANES
    return pl.pallas_call(
        functools.partial(_fox_kernel, nq),
        grid=(b, FOX_PAIRS, nq),
        in_specs=[pl.BlockSpec((tq, LANES), lambda bi, hp, qi: (bi * nq + qi, hp)),
                  pl.BlockSpec((None, t, LANES), lambda bi, hp, qi: (bi, 0, hp)),
                  pl.BlockSpec((None, LANES, t), lambda bi, hp, qi: (bi, vrow0 + hp, 0)),
                  pl.BlockSpec((None, None, t, 2), lambda bi, hp, qi: (bi, hp, 0, 0)),
                  pl.BlockSpec((None, None, 2, t), lambda bi, hp, qi: (bi, hp, 0, 0))],
        out_specs=pl.BlockSpec((tq, LANES), lambda bi, hp, qi: (bi * nq + qi, hp)),
        out_shape=jax.ShapeDtypeStruct((b * t, D_MODEL), BF16),
        scratch_shapes=[pltpu.VMEM((2, t, LANES), BF16), pltpu.VMEM((2, max(nq, SUBLANES), LANES), F32),
                        pltpu.VMEM((2, tq, LANES), BF16), pltpu.VMEM((2, 1, tq), F32),
                        pltpu.VMEM((2, LANES, tq), F32)],
        compiler_params=_cparams("parallel", "arbitrary", "arbitrary"),
        name="fox_attention",
    )(q, k, kv_t, cum_col, cum_row)


def _fox_prompt(rows, x, g, mod, w_in, b_f):
    b, t = rows.nseq, rows.t
    w, bias = _fox_weights(w_in, b_f)
    q, kv, kv_b, logf = _norm_proj(rows, x, g, mod, (0, 1), w, bias, FOX_PROJ_OUTS)
    cum = _cumsum_rows(logf, b, t)[:, :FOX_HEADS].reshape(b, t, FOX_PAIRS, 2)
    o = _fox_attention(b, t, q, kv_b.reshape(b, t, 2 * D_MODEL), kv.reshape(b, t, 2 * D_MODEL).transpose(0, 2, 1),
                       cum.transpose(0, 2, 1, 3), cum.transpose(0, 2, 3, 1))
    return o, kv, logf[:, :FOX_HEADS]


def _pool_kernel(tps, pos0, h_ref, w_ref, b_ref, scale_ref, o_ref, ext_scr):
    i = pl.program_id(0)
    tm = h_ref.shape[0]
    halo = 2 * SUBLANES
    first = (i % tps) == 0

    @pl.when(first)
    def _():
        ext_scr[0:halo, :] = jnp.zeros((halo, D_MODEL), F32)

    @pl.when(jnp.logical_not(first))
    def _():
        ext_scr[0:halo, :] = ext_scr[tm:tm + halo, :]

    ext_scr[halo:halo + tm, :] = h_ref[...]
    pos = pos0 + (i % tps) * tm + lax.broadcasted_iota(jnp.int32, (tm, 1), 0)
    for gi, w in enumerate(POOL_WINDOWS):
        c0, c1 = gi * POOL_GROUP_DIM, (gi + 1) * POOL_GROUP_DIM
        win = ext_scr[halo:halo + tm, c0:c1]
        for back in range(1, w):
            win = win + ext_scr[halo - back:halo - back + tm, c0:c1]
        cnt = jnp.minimum(w, pos + 1).astype(F32)
        mixed = win / cnt - h_ref[:, c0:c1]
        y = _bdot(mixed.astype(BF16), w_ref[gi]) + b_ref[:, c0:c1]
        o_ref[:, c0:c1] = y * scale_ref[:, c0:c1]


def _pool_mix(h_ext, nseq, t, tm, pos0, w_g, b_g, scale):
    assert POOL_STATE < 2 * SUBLANES
    tps = t // tm
    full = lambda shape: pl.BlockSpec(shape, lambda i: (0,) * len(shape))
    return pl.pallas_call(
        functools.partial(_pool_kernel, tps, pos0),
        grid=(nseq * tps,),
        in_specs=[pl.BlockSpec((tm, D_MODEL), lambda i: (i, 0)),
                  full((len(POOL_WINDOWS), POOL_GROUP_DIM, POOL_GROUP_DIM)), full((1, D_MODEL)), full((1, D_MODEL))],
        out_specs=pl.BlockSpec((tm, D_MODEL), lambda i: (i, 0)),
        out_shape=jax.ShapeDtypeStruct((nseq * t, D_MODEL), F32),
        scratch_shapes=[pltpu.VMEM((tm + 2 * SUBLANES, D_MODEL), F32)],
        compiler_params=_cparams("arbitrary"),
        name="pool_mix",
    )(h_ext, w_g.astype(BF16), b_g.reshape(1, D_MODEL), scale.reshape(1, D_MODEL))


def _residual_kernel(o_ref, x_ref, g_ref, gate_ref, y_ref):
    y_ref[...] = x_ref[...] + gate_ref[...] * _rms_rows(o_ref[...], g_ref[...])


def _gated_residual(rows, o, x, g, mod, gate_chunk):
    full = lambda shape: pl.BlockSpec(shape, lambda i: (0,) * len(shape))
    return pl.pallas_call(
        _residual_kernel,
        grid=(rows.ntiles,),
        in_specs=[rows.row_spec(D_MODEL), rows.row_spec(D_MODEL), full((1, D_MODEL)), rows.mod_spec(gate_chunk)],
        out_specs=rows.row_spec(D_MODEL),
        out_shape=jax.ShapeDtypeStruct((rows.m, D_MODEL), F32),
        compiler_params=_cparams("parallel"),
        name="gated_residual",
    )(o, x, g.reshape(1, D_MODEL), rows.mod_arr(mod))


H_ONLY = ((0, D_MODEL, "h", (F32,)),)


def _norm_only(rows, x, g, mod, chunks):
    dummy = jnp.zeros((D_MODEL, LANES), BF16)
    return _norm_proj(rows, x, g, mod, chunks, dummy, jnp.asarray(NO_BIAS), H_ONLY)[0]


DEC_B = 8
NEW_ROWS = SUBLANES


def _paged_grid_spec(grid, in_specs, out_specs, scratch_shapes):
    return pltpu.PrefetchScalarGridSpec(num_scalar_prefetch=1, grid=grid, in_specs=in_specs,
                                        out_specs=out_specs, scratch_shapes=scratch_shapes)


def _feat_major_pages(cache):
    n, p = cache.shape[:2]
    return jnp.moveaxis(cache, 1, -1).reshape(n, -1, p)


def _page_specs(npg, rows):
    return [pl.BlockSpec((None, rows, PAGE_SIZE), functools.partial(lambda b, pt, i: (pt[b * npg + i], 0, 0), i=i))
            for i in range(npg)]


def _cmp_dec_kernel(npg, pt_ref, *refs):
    pages, (wk_ref, wv_ref, o_ref, lhs_scr) = refs[:npg], refs[npg:]
    groups = 2 * NSA_KV_HEADS
    for d in range(HEAD_DIM):
        for i, page in enumerate(pages):
            lhs_scr[i * groups:(i + 1) * groups, d * PAGE_SIZE:(d + 1) * PAGE_SIZE] = page[pl.ds(d, groups, stride=HEAD_DIM), :]
    lhs = lhs_scr[...].astype(BF16)
    is_key = (lax.broadcasted_iota(jnp.int32, o_ref.shape, 0) % groups) < NSA_KV_HEADS
    o_ref[...] = jnp.where(is_key, _bdot(lhs, wk_ref[...]), _bdot(lhs, wv_ref[...])).astype(o_ref.dtype)


def _cmp_dec_weight(w):
    w3 = w.astype(BF16).reshape(NSA_BLOCK, HEAD_DIM, HEAD_DIM)
    halves = PAGE_SIZE // NSA_BLOCK
    eye = jnp.eye(halves, dtype=BF16)
    return jnp.einsum("rde,hg->dhrge", w3, eye).reshape(HEAD_DIM * PAGE_SIZE, halves * HEAD_DIM)


def _cmp_decode(page_table, cache_t, w_ck, w_cv):
    ns, npg = page_table.shape
    halves = PAGE_SIZE // NSA_BLOCK
    groups = 2 * NSA_KV_HEADS
    wspec = pl.BlockSpec((HEAD_DIM * PAGE_SIZE, halves * HEAD_DIM), lambda b, pt: (0, 0))
    out = pl.pallas_call(
        functools.partial(_cmp_dec_kernel, npg),
        grid_spec=_paged_grid_spec(
            (ns,), _page_specs(npg, 2 * KVW) + [wspec, wspec],
            pl.BlockSpec((None, npg * groups, halves * HEAD_DIM), lambda b, pt: (b, 0, 0)),
            [pltpu.VMEM((npg * groups, HEAD_DIM * PAGE_SIZE), F32)]),
        out_shape=jax.ShapeDtypeStruct((ns, npg * groups, halves * HEAD_DIM), BF16),
        compiler_params=_cparams("parallel"),
        name="nsa_decode_compress",
    )(page_table.reshape(-1), *([cache_t] * npg), _cmp_dec_weight(w_ck), _cmp_dec_weight(w_cv))
    out = out.reshape(ns, npg, 2, NSA_KV_HEADS, halves, HEAD_DIM).transpose(0, 1, 4, 2, 3, 5)
    return out.reshape(ns, npg * halves, 2 * KVW)


def _dec_select_kernel(qpos, nblk, qbd_ref, kcv_ref, slope_ref, oc_ref, selb_ref, imp_scr):
    lane_b = lax.broadcasted_iota(jnp.int32, (N_HEADS, nblk), 1)
    ends = lane_b * NSA_BLOCK + (NSA_BLOCK - 1)
    mask = ends <= qpos
    bias = slope_ref[...] * (qpos - ends).astype(F32)
    imp_scr[...] = jnp.zeros(imp_scr.shape, F32)
    for bi in range(DEC_B):
        s = _dot_nt(qbd_ref[bi], kcv_ref[bi, :, 0:KVW]) - bias
        s = jnp.where(mask, s, NEG)
        e = jnp.exp(s - jnp.max(s, axis=-1, keepdims=True))
        p = jnp.where(mask, e / jnp.sum(e, axis=-1, keepdims=True), 0.0)
        oc_ref[bi] = _bdot(p.astype(BF16), kcv_ref[bi, :, KVW:2 * KVW])
        imp = p[0:NSA_KV_HEADS]
        for g in range(1, NSA_GROUP):
            imp = imp + p[g * NSA_KV_HEADS:(g + 1) * NSA_KV_HEADS]
        imp_scr[bi * NSA_KV_HEADS:(bi + 1) * NSA_KV_HEADS, 0:nblk] = imp

    rows = DEC_B * NSA_KV_HEADS
    n2 = lax.broadcasted_iota(jnp.int32, (rows, LANES), 1)
    cur = qpos // NSA_BLOCK
    forced = (n2 == 0) | (n2 == cur) | (n2 == cur - 1)
    score = jnp.where(forced, BIG, jnp.where(n2 <= cur, imp_scr[...], NEG))
    score = jnp.where(n2 <= cur, score, -jnp.inf)

    def pick_next(_, carry):
        sc, sel = carry
        m = jnp.max(sc, axis=-1, keepdims=True)
        first = jnp.min(jnp.where(sc == m, n2, LANES), axis=-1, keepdims=True)
        pick = n2 == first
        sel = jnp.where(pick & (m > 0.5 * NEG), 1.0, sel)
        return jnp.where(pick, -jnp.inf, sc), sel

    _, sel = lax.fori_loop(0, min(NSA_TOPN, cur + 1), pick_next, (score, jnp.zeros((rows, LANES), F32)), unroll=True)
    selb = jnp.where(sel > 0.0, 0.0, NEG).astype(BF16)
    for bi in range(DEC_B):
        one = selb[bi * NSA_KV_HEADS:(bi + 1) * NSA_KV_HEADS]
        selb_ref[bi] = jnp.concatenate([one] * NSA_GROUP, axis=0)


def _dec_slopes():
    return jnp.asarray(_alibi_slopes().T.reshape(N_HEADS, 1))


def _nsa_decode_select(qbd, kcv, qpos):
    ns, nblk = kcv.shape[0], kcv.shape[1]
    assert qpos // NSA_BLOCK < LANES and ns % DEC_B == 0
    blk = lambda shape: pl.BlockSpec((DEC_B,) + shape, lambda i: (i, 0, 0))
    return pl.pallas_call(
        functools.partial(_dec_select_kernel, qpos, nblk),
        grid=(ns // DEC_B,),
        in_specs=[blk((N_HEADS, KVW)), blk((nblk, 2 * KVW)), pl.BlockSpec((N_HEADS, 1), lambda i: (0, 0))],
        out_specs=[blk((N_HEADS, KVW)), blk((N_HEADS, LANES))],
        out_shape=[jax.ShapeDtypeStruct((ns, N_HEADS, KVW), F32), jax.ShapeDtypeStruct((ns, N_HEADS, LANES), BF16)],
        scratch_shapes=[pltpu.VMEM((DEC_B * NSA_KV_HEADS, LANES), F32)],
        compiler_params=_cparams("parallel"),
        name="nsa_decode_select",
    )(qbd, kcv, _dec_slopes())


def _new_key_tile(row):
    r = lax.broadcasted_iota(jnp.int32, (NEW_ROWS, row.shape[1]), 0)
    return jnp.where(r == 0, jnp.broadcast_to(row, (NEW_ROWS, row.shape[1])), 0.0).astype(BF16)


def _new_key_mask(s):
    return jnp.where(lax.broadcasted_iota(jnp.int32, s.shape, 1) == 0, s, NEG)


def _softmax_with_new_key(s, s_new):
    m = jnp.maximum(jnp.max(s, axis=-1, keepdims=True), jnp.max(s_new, axis=-1, keepdims=True))
    p, p_new = jnp.exp(s - m), jnp.exp(s_new - m)
    return p, p_new, jnp.sum(p, axis=-1, keepdims=True) + jnp.sum(p_new, axis=-1, keepdims=True)


def _dec_attend_kernel(npg, qpos, wlen, pt_ref, *refs):
    pages = refs[:npg]
    (qbd_ref, selb_ref, win_ref, snew_ref, wnew_ref, wcol_ref, oc_ref, gate_ref, slope_ref,
     o_ref, wout_ref) = refs[npg:]
    slope = slope_ref[...]
    qbd = qbd_ref[...]

    qaug = jnp.concatenate([qbd, selb_ref[...]], axis=1)
    blk_row = lax.broadcasted_iota(jnp.int32, (LANES, PAGE_SIZE), 0)
    blk_of_lane = lax.broadcasted_iota(jnp.int32, (LANES, PAGE_SIZE), 1) // NSA_BLOCK
    scores = []
    for i, page in enumerate(pages):
        onehot_t = (blk_row == (PAGE_SIZE // NSA_BLOCK) * i + blk_of_lane).astype(BF16)
        kaug_t = jnp.concatenate([page[0:KVW, :].astype(BF16), onehot_t], axis=0)
        scores.append(_bdot(qaug, kaug_t))
    kpos = lax.broadcasted_iota(jnp.int32, (N_HEADS, npg * PAGE_SIZE), 1)
    s = jnp.concatenate(scores, axis=1) - slope * (qpos - kpos).astype(F32)
    snew = snew_ref[...]
    s_new = _new_key_mask(_dot_nt(qbd, _new_key_tile(snew[:, 0:KVW])))
    p, p_new, l = _softmax_with_new_key(s, s_new)
    p = p.astype(BF16)
    acc = _bdot(p_new.astype(BF16), _new_key_tile(snew[:, KVW:2 * KVW]))
    for i, page in enumerate(pages):
        acc = acc + _dot_nt(p[:, i * PAGE_SIZE:(i + 1) * PAGE_SIZE], page[KVW:2 * KVW, :].astype(BF16))
    o_s = acc / l

    win = win_ref[...]
    wpos = qpos - wlen + lax.broadcasted_iota(jnp.int32, (N_HEADS, wlen), 1)
    s = _bdot(qbd, win[0:KVW, :].astype(BF16)) - slope * (qpos - wpos).astype(F32)
    s = jnp.where(wpos > qpos - NSA_WINDOW, s, NEG)
    wnew = wnew_ref[...]
    s_new = _new_key_mask(_dot_nt(qbd, _new_key_tile(wnew[:, 0:KVW])))
    p, p_new, l = _softmax_with_new_key(s, s_new)
    o_w = (_dot_nt(p.astype(BF16), win[KVW:2 * KVW, :].astype(BF16))
           + _bdot(p_new.astype(BF16), _new_key_tile(wnew[:, KVW:2 * KVW]))) / l

    gates = gate_ref[...]
    o_ref[...] = gates[:, 0:1] * oc_ref[...] + gates[:, 1:2] * o_s + gates[:, 2:3] * o_w
    lane = lax.broadcasted_iota(jnp.int32, win.shape, 1)
    wout_ref[...] = jnp.where(lane == wlen - 1, wcol_ref[...], pltpu.roll(win, wlen - 1, axis=1))


def _nsa_decode_attend(page_table, qbd, selb, cache_t, win_t, kvs_new, kvw_new, oc, gates, qpos):
    ns, npg = page_table.shape
    wlen = win_t.shape[2]
    assert qpos == npg * PAGE_SIZE
    per_b = lambda shape: pl.BlockSpec((None,) + shape, lambda b, pt: (b, 0, 0))
    return pl.pallas_call(
        functools.partial(_dec_attend_kernel, npg, qpos, wlen),
        grid_spec=_paged_grid_spec(
            (ns,),
            _page_specs(npg, 2 * KVW)
            + [per_b((N_HEADS, KVW)), per_b((N_HEADS, LANES)), per_b((2 * KVW, wlen)), per_b((1, 2 * KVW)),
               per_b((1, 2 * KVW)), per_b((2 * KVW, 1)), per_b((N_HEADS, KVW)), per_b((N_HEADS, LANES)),
               pl.BlockSpec((N_HEADS, 1), lambda b, pt: (0, 0))],
            [per_b((N_HEADS, KVW)), per_b((2 * KVW, wlen))],
            []),
        out_shape=[jax.ShapeDtypeStruct((ns, N_HEADS, KVW), F32), jax.ShapeDtypeStruct(win_t.shape, F32)],
        compiler_params=_cparams("parallel"),
        name="nsa_decode_attend",
    )(page_table.reshape(-1), *([cache_t] * npg), qbd, selb, win_t, kvs_new.reshape(ns, 1, 2 * KVW),
      kvw_new.reshape(ns, 1, 2 * KVW), kvw_new.reshape(ns, 2 * KVW, 1), oc, gates, _dec_slopes())


def _nsa_sample(rows, x, g, mod, w_in, w_ck, w_cv, cache_c, cache_s, win_buf, page_table):
    ns = rows.nseq
    qpos = page_table.shape[1] * PAGE_SIZE
    q, kvc, _, kvs, _, kvw, _, gates = _norm_proj(
        rows, x, g, mod, (0, 1), _nsa_weights(w_in), jnp.asarray(NO_BIAS), NSA_PROJ_OUTS)
    kcv = _cmp_decode(page_table, _feat_major_pages(cache_c), w_ck, w_cv)
    q4 = q.reshape(ns, NSA_KV_HEADS, NSA_GROUP, HEAD_DIM)
    eye = jnp.eye(NSA_KV_HEADS, dtype=q.dtype)
    qbd = jnp.einsum("bkgd,kj->bgkjd", q4, eye).reshape(ns, N_HEADS, KVW)
    oc, selb = _nsa_decode_select(qbd, kcv, qpos)
    g4 = gates.reshape(ns, NSA_KV_HEADS, LANES)[:, :, :NSA_GROUP * 3].reshape(ns, NSA_KV_HEADS, NSA_GROUP, 3)
    g_rows = jnp.pad(g4.transpose(0, 2, 1, 3).reshape(ns, N_HEADS, 3), ((0, 0), (0, 0), (0, LANES - 3)))
    o_bd, win_out = _nsa_decode_attend(
        page_table, qbd, selb, _feat_major_pages(cache_s), _feat_major_pages(win_buf), kvs, kvw, oc, g_rows, qpos)
    o5 = o_bd.reshape(ns, NSA_GROUP, NSA_KV_HEADS, NSA_KV_HEADS, HEAD_DIM)
    o = jnp.einsum("bgkkd->bkgd", o5).reshape(ns, D_MODEL).astype(BF16)
    win_out = jnp.moveaxis(win_out.reshape(win_buf.shape[:1] + win_buf.shape[2:] + win_buf.shape[1:2]), -1, 1)
    return o, kvc, kvs, win_out


def _fox_dec_kernel(npg, pt_ref, *refs):
    kv_pages, lf_pages = refs[:npg], refs[npg:2 * npg]
    q_ref, kvn_ref, lfn_ref, o_ref = refs[2 * npg:]
    own = (lax.broadcasted_iota(jnp.int32, (FOX_HEADS, D_MODEL), 1) // HEAD_DIM
           == lax.broadcasted_iota(jnp.int32, (FOX_HEADS, D_MODEL), 0))
    q = jnp.broadcast_to(q_ref[...].astype(F32), (FOX_HEADS, D_MODEL))
    qbd = jnp.where(own, q, 0.0).astype(BF16)
    upper = (lax.broadcasted_iota(jnp.int32, (PAGE_SIZE, PAGE_SIZE), 0)
             <= lax.broadcasted_iota(jnp.int32, (PAGE_SIZE, PAGE_SIZE), 1)).astype(BF16)
    carry = jnp.zeros((FOX_HEADS, 1), F32)
    cums, scores = [], []
    for kv_page, lf_page in zip(kv_pages, lf_pages):
        a, b, c = _split3(lf_page[...])
        cum = carry + (_bdot(a, upper) + _bdot(b, upper) + _bdot(c, upper))
        carry = cum[:, PAGE_SIZE - 1:PAGE_SIZE]
        cums.append(cum)
        scores.append(_bdot(qbd, kv_page[0:D_MODEL, :].astype(BF16)))
    cum_new = carry + lfn_ref[...]
    s = jnp.concatenate(scores, axis=1) + (cum_new - jnp.concatenate(cums, axis=1))
    kvn = kvn_ref[...]
    s_new = _new_key_mask(_dot_nt(qbd, _new_key_tile(kvn[:, 0:D_MODEL])))
    p, p_new, l = _softmax_with_new_key(s, s_new)
    p = p.astype(BF16)
    acc = _bdot(p_new.astype(BF16), _new_key_tile(kvn[:, D_MODEL:2 * D_MODEL]))
    for i, kv_page in enumerate(kv_pages):
        acc = acc + _dot_nt(p[:, i * PAGE_SIZE:(i + 1) * PAGE_SIZE], kv_page[D_MODEL:2 * D_MODEL, :].astype(BF16))
    o_ref[...] = jnp.sum(jnp.where(own, acc / l, 0.0), axis=0, keepdims=True).astype(o_ref.dtype)


def _fox_decode(page_table, q, cache_kv_t, cache_logf_t, kv_new, logf_new):
    ns, npg = page_table.shape
    per_b = lambda shape: pl.BlockSpec((None,) + shape, lambda b, pt: (b, 0, 0))
    out = pl.pallas_call(
        functools.partial(_fox_dec_kernel, npg),
        grid_spec=_paged_grid_spec(
            (ns,),
            _page_specs(npg, 2 * D_MODEL) + _page_specs(npg, FOX_HEADS)
            + [per_b((1, D_MODEL)), per_b((1, 2 * D_MODEL)), per_b((FOX_HEADS, 1))],
            per_b((1, D_MODEL)), []),
        out_shape=jax.ShapeDtypeStruct((ns, 1, D_MODEL), BF16),
        compiler_params=_cparams("parallel"),
        name="fox_decode",
    )(page_table.reshape(-1), *([cache_kv_t] * npg), *([cache_logf_t] * npg), q.reshape(ns, 1, D_MODEL),
      kv_new.reshape(ns, 1, 2 * D_MODEL), logf_new.reshape(ns, FOX_HEADS, 1))
    return out.reshape(ns, D_MODEL)


def _fox_sample(rows, x, g, mod, w_in, b_f, cache_kv, cache_logf, page_table):
    w, bias = _fox_weights(w_in, b_f)
    q, kv, _, logf = _norm_proj(rows, x, g, mod, (0, 1), w, bias, FOX_PROJ_OUTS)
    logf = logf[:, :FOX_HEADS]
    o = _fox_decode(page_table, q, _feat_major_pages(cache_kv), _feat_major_pages(cache_logf), kv, logf)
    return o, kv, logf


PROMPT_TM = 512
POOL_TM = 256


def kernel(x_prompt, x_sample, cache_l0_cmp_kv, cache_l0_sel_kv, state_l0_win_kv, cache_l1_kv, cache_l1_logf, state_l2_pool, cache_l3_cmp_kv, cache_l3_sel_kv, state_l3_win_kv, state_ffn_conv, page_table, c_prompt, c_sample, mod_w, mod_b, norm_g, l0_nsa_w_in, l0_nsa_w_ck, l0_nsa_w_cv, l0_nsa_w_o, l1_fox_w_in, l1_fox_b_f, l1_fox_w_o, l2_pool_w, l2_pool_b, l2_pool_scale, l3_nsa_w_in, l3_nsa_w_ck, l3_nsa_w_cv, l3_nsa_w_o, ffn_w_gu, ffn_conv_w, ffn_conv_b, ffn_w_d):
    b, t, _ = x_prompt.shape
    ns = x_sample.shape[0]
    past_len = page_table.shape[1] * PAGE_SIZE
    rp = _Rows(b, t, min(PROMPT_TM, t))
    rs = _Rows(ns, 1, ns)
    nsa = {0: (cache_l0_cmp_kv, cache_l0_sel_kv, state_l0_win_kv, l0_nsa_w_in, l0_nsa_w_ck, l0_nsa_w_cv, l0_nsa_w_o),
           3: (cache_l3_cmp_kv, cache_l3_sel_kv, state_l3_win_kv, l3_nsa_w_in, l3_nsa_w_ck, l3_nsa_w_cv, l3_nsa_w_o)}

    c_all = jnp.concatenate([c_prompt, c_sample], axis=0)
    c_all = jnp.pad(c_all, ((0, -c_all.shape[0] % SUBLANES), (0, 0)))
    mod = _modulation(c_all, mod_w, mod_b)

    xp = x_prompt.reshape(b * t, D_MODEL)
    xs = x_sample.reshape(ns, D_MODEL)
    st = {}
    conv_p, conv_s = [], []
    kv5 = lambda a, n: a.reshape(n, -1, 2, NSA_KV_HEADS, HEAD_DIM)
    for i in range(DEPTH):
        mp, ms = mod[i, :b], mod[i, b:b + ns]
        g = norm_g[i]
        kind = i % 3
        if kind == 0:
            c_c, c_s, s_w, w_in, w_ck, w_cv, w_o = nsa[i]
            op, kvc, kvs, kvw = _nsa_prompt(rp, xp, g[0], mp, w_in, w_ck, w_cv)
            os_, kvc_s, kvs_s, win_s = _nsa_sample(rs, xs, g[0], ms, w_in, w_ck, w_cv, c_c, c_s, s_w, page_table)
            st[i] = (kv5(kvc, b), kv5(kvc_s, ns), kv5(kvs, b), kv5(kvs_s, ns),
                     kv5(kvw, b)[:, -min(NSA_WINDOW, t):], win_s)
            w_ob = w_o.astype(BF16)
            xp = _out_proj_residual(rp, op, w_ob, xp, g[1], mp, 2)
            xs = _out_proj_residual(rs, os_, w_ob, xs, g[1], ms, 2)
        elif kind == 1:
            op, kv_p, lf_p = _fox_prompt(rp, xp, g[0], mp, l1_fox_w_in, l1_fox_b_f)
            os_, kv_s, lf_s = _fox_sample(rs, xs, g[0], ms, l1_fox_w_in, l1_fox_b_f, cache_l1_kv, cache_l1_logf, page_table)
            st[i] = (kv_p.reshape(b, t, 2, FOX_HEADS, HEAD_DIM), kv_s.reshape(ns, 1, 2, FOX_HEADS, HEAD_DIM),
                     lf_p.reshape(b, t, FOX_HEADS), lf_s.reshape(ns, 1, FOX_HEADS))
            w_ob = l1_fox_w_o.astype(BF16)
            xp = _out_proj_residual(rp, op, w_ob, xp, g[1], mp, 2)
            xs = _out_proj_residual(rs, os_, w_ob, xs, g[1], ms, 2)
        else:
            hp = _norm_only(rp, xp, g[0], mp, (0, 1))
            hs = _norm_only(rs, xs, g[0], ms, (0, 1))
            yp = _pool_mix(hp, b, t, min(POOL_TM, t), 0, l2_pool_w, l2_pool_b.reshape(-1), l2_pool_scale)
            ext = jnp.concatenate([state_l2_pool, hs[:, None, :]], axis=1)
            n_ext = POOL_STATE + 1
            ys = _pool_mix(ext.reshape(ns * n_ext, D_MODEL), ns, n_ext, n_ext, past_len - POOL_STATE,
                           l2_pool_w, l2_pool_b.reshape(-1), l2_pool_scale).reshape(ns, n_ext, D_MODEL)[:, -1]
            st[i] = (hp.reshape(b, t, D_MODEL)[:, -POOL_STATE:], ext[:, -POOL_STATE:])
            xp = _gated_residual(rp, yp, xp, g[1], mp, 2)
            xs = _gated_residual(rs, ys, xs, g[1], ms, 2)
        w_gu, w_d = ffn_w_gu[i].astype(BF16), ffn_w_d[i].astype(BF16)
        xp, cp = _conv_ffn(rp, xp, g[2], g[3], mp, w_gu, ffn_conv_w[i], ffn_conv_b[i], w_d, None)
        xs, cs = _conv_ffn(rs, xs, g[2], g[3], ms, w_gu, ffn_conv_w[i], ffn_conv_b[i], w_d, state_ffn_conv[i])
        conv_p.append(cp)
        conv_s.append(cs)
    return (xp.reshape(b, t, D_MODEL), xs.reshape(ns, 1, D_MODEL),
            *st[0], *st[1], *st[2], *st[3],
            jnp.stack(conv_p), jnp.stack(conv_s))
```

```python
import functools

import jax
import jax.numpy as jnp
import numpy as np
from jax import lax
from jax.experimental import pallas as pl
from jax.experimental.pallas import tpu as pltpu

D_MODEL = 1024
DEPTH = 4
PAGE_SIZE = 128
HEAD_DIM = 64
N_HEADS = D_MODEL // HEAD_DIM
NSA_KV_HEADS = 4
NSA_GROUP = N_HEADS // NSA_KV_HEADS
NSA_BLOCK = 64
NSA_TOPN = 16
NSA_WINDOW = 512
FOX_HEADS = D_MODEL // HEAD_DIM
POOL_WINDOWS = (2, 4, 8, 16)
POOL_GROUP_DIM = D_MODEL // len(POOL_WINDOWS)
POOL_STATE = max(POOL_WINDOWS) - 1
D_FF = 2816
CONV_W = 3
N_MOD = 6
RMS_EPS = 1e-6
NEG = -1e30
BIG = 1e30
ATTN_SCALE = HEAD_DIM ** -0.5

LANES = 128
SUBLANES = 8
VMEM_LIMIT = 56 * 1024 * 1024

KVW = NSA_KV_HEADS * HEAD_DIM
NSA_Q0, NSA_C0, NSA_S0, NSA_W0, NSA_G0 = 0, D_MODEL, D_MODEL + 2 * KVW, D_MODEL + 4 * KVW, D_MODEL + 6 * KVW
NSA_COLS = NSA_G0 + NSA_KV_HEADS * LANES
FOX_LF0 = 3 * D_MODEL
FOX_COLS = FOX_LF0 + LANES

F32 = jnp.float32
BF16 = jnp.bfloat16


def _cparams(*sem):
    return pltpu.CompilerParams(dimension_semantics=sem, vmem_limit_bytes=VMEM_LIMIT)


def _bdot(a, b):
    return jnp.dot(a, b, preferred_element_type=F32)


def _dot_nt(a, b):
    return lax.dot_general(a, b, (((1,), (1,)), ((), ())), preferred_element_type=F32)


def _rms_rows(x, g):
    return x * lax.rsqrt(jnp.mean(x * x, axis=-1, keepdims=True) + RMS_EPS) * g


def _split3(x):
    a = x.astype(BF16)
    r = x - a.astype(F32)
    b = r.astype(BF16)
    c = (r - b.astype(F32)).astype(BF16)
    return a, b, c


def _mod_kernel(c_ref, w_ref, b_ref, o_ref):
    c = c_ref[...]
    a = (c * jax.nn.sigmoid(c)).astype(BF16)
    o_ref[...] = _bdot(a, w_ref[...].astype(BF16)) + b_ref[...]


def _modulation(c_all, mod_w, mod_b):
    rows = c_all.shape[0]
    n = N_MOD * D_MODEL
    tn = D_MODEL
    return pl.pallas_call(
        _mod_kernel,
        grid=(DEPTH, n // tn),
        in_specs=[pl.BlockSpec((rows, D_MODEL), lambda i, j: (0, 0)),
                  pl.BlockSpec((None, D_MODEL, tn), lambda i, j: (i, 0, j)),
                  pl.BlockSpec((None, 1, tn), lambda i, j: (i, 0, j))],
        out_specs=pl.BlockSpec((None, rows, tn), lambda i, j: (i, 0, j)),
        out_shape=jax.ShapeDtypeStruct((DEPTH, rows, n), F32),
        compiler_params=_cparams("parallel", "parallel"),
        name="modulation",
    )(c_all, mod_w, mod_b.reshape(DEPTH, 1, n))


def _norm_h(x, g, shift, scale):
    return _rms_rows(x, g) * (1.0 + scale) + shift


class _Rows:
    def __init__(self, nseq, t, tm):
        assert t % tm == 0 or t == 1
        self.nseq, self.t = nseq, t
        self.decode = t == 1
        self.tm = nseq if self.decode else tm
        self.m = nseq * t
        self.tiles_per_seq = 1 if self.decode else t // tm
        self.ntiles = self.m // self.tm

    def mod_spec(self, chunk):
        if self.decode:
            return pl.BlockSpec((self.tm, D_MODEL), lambda i, *_: (0, chunk))
        tps = self.tiles_per_seq
        return pl.BlockSpec((None, 1, D_MODEL), lambda i, *_: (i // tps, 0, chunk))

    def mod_arr(self, mod):
        return mod if self.decode else mod.reshape(self.nseq, 1, N_MOD * D_MODEL)

    def row_spec(self, width, col=0):
        return pl.BlockSpec((self.tm, width), lambda i, *_: (i, col))


def _proj_kernel(outs, x_ref, g_ref, sh_ref, sc_ref, w_ref, bias_ref, *o_refs):
    h = _norm_h(x_ref[...], g_ref[...], sh_ref[...], sc_ref[...])
    hb = h.astype(BF16)
    refs = iter(o_refs)
    for c0, c1, kind, dtypes in outs:
        if kind == "h":
            p = h
        else:
            p = _bdot(hb, w_ref[:, c0:c1])
        if kind == "scale":
            p = p * ATTN_SCALE
        elif kind == "sigmoid":
            p = jax.nn.sigmoid(p)
        elif kind == "logsigmoid":
            p = jax.nn.log_sigmoid(p + bias_ref[...])
        for dt in dtypes:
            if dt == FEAT_MAJOR:
                next(refs)[...] = p.T
            else:
                next(refs)[...] = p.astype(dt)


FEAT_MAJOR = "feature-major f32"


def _norm_proj(rows, x, g, mod, chunks, w, bias, outs):
    ncols = w.shape[1]
    kern = functools.partial(_proj_kernel, outs)
    full = lambda shape: pl.BlockSpec(shape, lambda i: (0,) * len(shape))
    marr = rows.mod_arr(mod)
    flat = [(c1 - c0, dt) for c0, c1, _, dts in outs for dt in dts]
    tps = rows.tiles_per_seq

    def spec(wd, dt):
        if dt == FEAT_MAJOR:
            return pl.BlockSpec((None, wd, rows.tm), lambda i: (i // tps, 0, i % tps))
        return rows.row_spec(wd)

    def shape(wd, dt):
        if dt == FEAT_MAJOR:
            return jax.ShapeDtypeStruct((rows.nseq, wd, rows.t), F32)
        return jax.ShapeDtypeStruct((rows.m, wd), dt)

    return pl.pallas_call(
        kern,
        grid=(rows.ntiles,),
        in_specs=[rows.row_spec(D_MODEL), full((1, D_MODEL)),
                  rows.mod_spec(chunks[0]), rows.mod_spec(chunks[1]),
                  full((D_MODEL, ncols)), full((1, LANES))],
        out_specs=[spec(wd, dt) for wd, dt in flat],
        out_shape=[shape(wd, dt) for wd, dt in flat],
        compiler_params=_cparams("parallel"),
        name="norm_proj",
    )(x, g.reshape(1, D_MODEL), marr, marr, w, bias)


def _oproj_kernel(o_ref, w_ref, x_ref, g_ref, gate_ref, y_ref):
    y = _bdot(o_ref[...], w_ref[...])
    y_ref[...] = x_ref[...] + gate_ref[...] * _rms_rows(y, g_ref[...])


def _out_proj_residual(rows, o, w, x, g, mod, gate_chunk):
    k = w.shape[0]
    full = lambda shape: pl.BlockSpec(shape, lambda i: (0,) * len(shape))
    return pl.pallas_call(
        _oproj_kernel,
        grid=(rows.ntiles,),
        in_specs=[rows.row_spec(k), full((k, D_MODEL)), rows.row_spec(D_MODEL),
                  full((1, D_MODEL)), rows.mod_spec(gate_chunk)],
        out_specs=rows.row_spec(D_MODEL),
        out_shape=jax.ShapeDtypeStruct((rows.m, D_MODEL), F32),
        compiler_params=_cparams("parallel"),
        name="out_proj_residual",
    )(o, w, x, g.reshape(1, D_MODEL), rows.mod_arr(mod))


FFN_TF = 256
FFN_NJ = D_FF // FFN_TF
HALO = SUBLANES


def _ffn_kernel(decode, tiles_per_seq, x_ref, g2_ref, sh_ref, sc_ref, wg_ref, wu_ref, cw_ref, cb_ref,
                wd_ref, g3_ref, gate_ref, p0_ref, p1_ref, y_ref, st_ref, h_scr, act_scr, gs_scr, carry_scr):
    i, j = pl.program_id(0), pl.program_id(1)
    tm = x_ref.shape[0]
    tf = wg_ref.shape[1]

    @pl.when(j == 0)
    def _():
        h_scr[...] = _norm_h(x_ref[...], g2_ref[...], sh_ref[...], sc_ref[...]).astype(BF16)

    h = h_scr[...]
    gcol = _bdot(h, wg_ref[...])
    ucol = _bdot(h, wu_ref[...])
    cw = cw_ref[...]
    if decode:
        a = cb_ref[...] + cw[0:1] * p0_ref[...] + cw[1:2] * p1_ref[...] + cw[2:3] * gcol
        st_ref[...] = gcol
    else:
        first = (i % tiles_per_seq) == 0

        @pl.when(first)
        def _():
            gs_scr[0:HALO, :] = jnp.zeros((HALO, gs_scr.shape[1]), F32)

        @pl.when(jnp.logical_not(first))
        def _():
            gs_scr[0:HALO, :] = carry_scr[j]

        gs_scr[HALO:HALO + tm, :] = gcol
        a = (cb_ref[...] + cw[0:1] * gs_scr[HALO - 2:HALO - 2 + tm, :]
             + cw[1:2] * gs_scr[HALO - 1:HALO - 1 + tm, :] + cw[2:3] * gcol)
        tail = gcol[tm - HALO:tm, :]
        carry_scr[j] = tail
        st_ref[...] = tail
    act_scr[:, pl.ds(pl.multiple_of(j * tf, tf), tf)] = (a * jax.nn.sigmoid(a) * ucol).astype(BF16)

    @pl.when(j == pl.num_programs(1) - 1)
    def _():
        y = _bdot(act_scr[...], wd_ref[...])
        y_ref[...] = x_ref[...] + gate_ref[...] * _rms_rows(y, g3_ref[...])


def _conv_ffn(rows, x, g2, g3, mod, w_gu, conv_w, conv_b, w_d, prev):
    tm, tf = rows.tm, FFN_TF
    decode = rows.decode
    tps = rows.tiles_per_seq
    kern = functools.partial(_ffn_kernel, decode, tps)
    full = lambda shape: pl.BlockSpec(shape, lambda i, j: (0,) * len(shape))
    col = lambda r: pl.BlockSpec((r, tf), lambda i, j: (0, j))
    if decode:
        p0, p1 = prev[:, 0], prev[:, 1]
        pspec = pl.BlockSpec((tm, tf), lambda i, j: (0, j))
        st_spec = pl.BlockSpec((tm, tf), lambda i, j: (0, j))
        st_shape = jax.ShapeDtypeStruct((rows.m, D_FF), F32)
    else:
        p0 = p1 = jnp.zeros((SUBLANES, LANES), F32)
        pspec = full((SUBLANES, LANES))
        st_spec = pl.BlockSpec((None, HALO, tf), lambda i, j: (i, 0, j))
        st_shape = jax.ShapeDtypeStruct((rows.ntiles, HALO, D_FF), F32)
    marr = rows.mod_arr(mod)
    y, st = pl.pallas_call(
        kern,
        grid=(rows.ntiles, FFN_NJ),
        in_specs=[rows.row_spec(D_MODEL), full((1, D_MODEL)), rows.mod_spec(3), rows.mod_spec(4),
                  pl.BlockSpec((D_MODEL, tf), lambda i, j: (0, j)),
                  pl.BlockSpec((D_MODEL, tf), lambda i, j: (0, FFN_NJ + j)),
                  col(CONV_W), col(1),
                  full((D_FF, D_MODEL)),
                  full((1, D_MODEL)), rows.mod_spec(5), pspec, pspec],
        out_specs=[rows.row_spec(D_MODEL), st_spec],
        out_shape=[jax.ShapeDtypeStruct((rows.m, D_MODEL), F32), st_shape],
        scratch_shapes=[pltpu.VMEM((tm, D_MODEL), BF16), pltpu.VMEM((tm, D_FF), BF16),
                        pltpu.VMEM((HALO + tm, tf), F32), pltpu.VMEM((FFN_NJ, HALO, tf), F32)],
        compiler_params=_cparams("arbitrary", "arbitrary"),
        name="conv_ffn",
    )(x, g2.reshape(1, D_MODEL), marr, marr, w_gu, w_gu, conv_w, conv_b.reshape(1, D_FF),
      w_d, g3.reshape(1, D_MODEL), marr, p0, p1)
    if decode:
        return y, jnp.stack([prev[:, 1], st], axis=1)
    last = st.reshape(rows.nseq, tps, HALO, D_FF)[:, -1]
    return y, last[:, HALO - (CONV_W - 1):]


def _mm_kernel(a_ref, w_ref, o_ref):
    o_ref[...] = _bdot(a_ref[...], w_ref[...])


def _matmul(a, w, tm):
    m, k = a.shape
    n = w.shape[1]
    return pl.pallas_call(
        _mm_kernel,
        grid=(m // tm,),
        in_specs=[pl.BlockSpec((tm, k), lambda i: (i, 0)), pl.BlockSpec((k, n), lambda i: (0, 0))],
        out_specs=pl.BlockSpec((tm, n), lambda i: (i, 0)),
        out_shape=jax.ShapeDtypeStruct((m, n), F32),
        compiler_params=_cparams("parallel"),
        name="matmul",
    )(a, w)


Q_TILE = 128
KEY_CHUNK = 128
GQ = NSA_GROUP * Q_TILE
SEL_UNROLL = 4
M_INIT = -3e38


def _alibi_slopes():
    h = np.arange(1, N_HEADS + 1, dtype=np.float32)
    return np.exp2(-8.0 * h / N_HEADS).astype(np.float32).reshape(NSA_KV_HEADS, NSA_GROUP)


def _half_masks():
    lane = lax.broadcasted_iota(jnp.int32, (Q_TILE, LANES), 1)
    return lane < HEAD_DIM


def _stack_group_heads(q, lo, par=None):
    tiles = []
    for g in range(NSA_GROUP):
        t = q[:, (g // 2) * LANES:(g // 2 + 1) * LANES]
        keep = lo if g % 2 == 0 else jnp.logical_not(lo)
        t = jnp.where(keep, t, jnp.zeros_like(t))
        if par is not None:
            rolled = pltpu.roll(t.astype(F32), HEAD_DIM, axis=1).astype(BF16)
            t = jnp.where(par == g % 2, t, rolled)
        tiles.append(t)
    return jnp.concatenate(tiles, axis=0)


def _nsa_cmp_kernel(nb, q_ref, kcc_ref, vcc_ref, slope_ref, oc_ref, selb_ref, any_ref):
    j = pl.program_id(2)
    q0 = j * Q_TILE
    lo = _half_masks()
    qs = _stack_group_heads(q_ref[...], lo)
    st = _dot_nt(kcc_ref[...], qs)
    n_i = lax.broadcasted_iota(jnp.int32, (nb, GQ), 0)
    qpos = q0 + (lax.broadcasted_iota(jnp.int32, (nb, GQ), 1) & (Q_TILE - 1))
    ends = n_i * NSA_BLOCK + (NSA_BLOCK - 1)
    mask = ends <= qpos
    s = st - slope_ref[...] * (qpos - ends).astype(F32)
    s = jnp.where(mask, s, NEG)
    e = jnp.exp(s - jnp.max(s, axis=0, keepdims=True))
    p = jnp.where(mask, e / jnp.sum(e, axis=0, keepdims=True), 0.0)
    acc = _bdot(p.T.astype(BF16), vcc_ref[...])
    for t in range(NSA_GROUP // 2):
        a0 = acc[(2 * t) * Q_TILE:(2 * t + 1) * Q_TILE]
        a1 = acc[(2 * t + 1) * Q_TILE:(2 * t + 2) * Q_TILE]
        oc_ref[:, t * LANES:(t + 1) * LANES] = jnp.where(lo, a0, a1)

    imp = p[:, 0:Q_TILE]
    for g in range(1, NSA_GROUP):
        imp = imp + p[:, g * Q_TILE:(g + 1) * Q_TILE]
    n2 = lax.broadcasted_iota(jnp.int32, (nb, Q_TILE), 0)
    cur = (q0 + lax.broadcasted_iota(jnp.int32, (nb, Q_TILE), 1)) // NSA_BLOCK
    forced = (n2 == 0) | (n2 == cur) | (n2 == cur - 1)
    score = jnp.where(forced, BIG, jnp.where(n2 <= cur, imp, NEG))

    def pick_next(_, carry):
        sc, sel = carry
        m = jnp.max(sc, axis=0, keepdims=True)
        first = jnp.min(jnp.where(sc == m, n2, nb), axis=0, keepdims=True)
        pick = n2 == first
        sel = jnp.where(pick & (m > 0.5 * NEG), 1.0, sel)
        return jnp.where(pick, -jnp.inf, sc), sel

    _, sel = lax.fori_loop(0, min(NSA_TOPN, nb), pick_next, (score, jnp.zeros((nb, Q_TILE), F32)), unroll=True)
    sel_t = sel.T
    if nb < LANES:
        sel_t = jnp.concatenate([sel_t, jnp.zeros((Q_TILE, LANES - nb), F32)], axis=1)
    selb_ref[...] = jnp.where(sel_t > 0.0, 0.0, NEG).astype(BF16)
    any_ref[...] = jnp.max(sel_t, axis=0, keepdims=True)


def _nsa_compressed(b, t, q, kcc, vcc):
    nb = t // NSA_BLOCK
    nq = t // Q_TILE
    slopes = jnp.asarray(np.repeat(_alibi_slopes(), Q_TILE, axis=1).reshape(NSA_KV_HEADS, 1, GQ))
    return pl.pallas_call(
        functools.partial(_nsa_cmp_kernel, nb),
        grid=(b, NSA_KV_HEADS, nq),
        in_specs=[pl.BlockSpec((Q_TILE, KVW), lambda bi, k, j: (bi * nq + j, k)),
                  pl.BlockSpec((None, None, nb, LANES), lambda bi, k, j: (bi, k, 0, 0)),
                  pl.BlockSpec((None, None, nb, LANES), lambda bi, k, j: (bi, k, 0, 0)),
                  pl.BlockSpec((None, 1, GQ), lambda bi, k, j: (k, 0, 0))],
        out_specs=[pl.BlockSpec((Q_TILE, KVW), lambda bi, k, j: (bi * nq + j, k)),
                   pl.BlockSpec((None, None, Q_TILE, LANES), lambda bi, k, j: (bi, k, j, 0)),
                   pl.BlockSpec((None, None, None, 1, LANES), lambda bi, k, j: (bi, k, j, 0, 0))],
        out_shape=[jax.ShapeDtypeStruct((b * t, D_MODEL), F32),
                   jax.ShapeDtypeStruct((b, NSA_KV_HEADS, t, LANES), BF16),
                   jax.ShapeDtypeStruct((b, NSA_KV_HEADS, nq, 1, LANES), F32)],
        compiler_params=_cparams("parallel", "parallel", "parallel"),
        name="nsa_compressed",
    )(q, kcc, vcc, slopes)


def _softmax_step(s, v, m_ref, l_ref, acc_ref):
    m_prev = m_ref[...]
    m_new = jnp.maximum(m_prev, jnp.max(s, axis=-1, keepdims=True))
    alpha = jnp.exp(m_prev - m_new)
    p = jnp.exp(s - m_new)
    l_ref[...] = alpha * l_ref[...] + jnp.sum(p, axis=-1, keepdims=True)
    acc_ref[...] = alpha * acc_ref[...] + _bdot(p.astype(BF16), v)
    m_ref[...] = m_new


def _nsa_sw_kernel(nq, flags_ref, q_ref, selb_ref, ks_ref, vst_ref, kw_ref, vwt_ref, posx_ref, onehot_ref, qx_ref,
                   oc_ref, gate_ref, slope_ref, o_ref, ksa_scr, kwa_scr, qaug_scr, ms_scr, as_scr, mw_scr, aw_scr, list_scr):
    k, j = pl.program_id(1), pl.program_id(2)
    par = k % 2
    lane_b = lax.broadcasted_iota(jnp.int32, (KEY_CHUNK, LANES), 1)

    @pl.when(j == 0)
    def _():
        def fill(c, carry):
            r0 = pl.multiple_of(c * KEY_CHUNK, KEY_CHUNK)
            own = (lane_b // HEAD_DIM) == par
            px = posx_ref[...]
            ksa_scr[pl.ds(r0, KEY_CHUNK), 0:LANES] = onehot_ref[pl.ds(r0, KEY_CHUNK), :]
            ksa_scr[pl.ds(r0, KEY_CHUNK), LANES:2 * LANES] = jnp.where(own, ks_ref[pl.ds(r0, KEY_CHUNK), :], px)
            kwa_scr[pl.ds(r0, KEY_CHUNK), :] = jnp.where(own, kw_ref[pl.ds(r0, KEY_CHUNK), :], px)
            return carry

        lax.fori_loop(0, nq, fill, 0)

    lo = _half_masks()
    qq = _stack_group_heads(q_ref[...], lo, par)
    own_q = (lax.broadcasted_iota(jnp.int32, (GQ, LANES), 1) // HEAD_DIM) == par
    qaug_scr[:, 0:LANES] = jnp.concatenate([selb_ref[...]] * NSA_GROUP, axis=0)
    qaug_scr[:, LANES:2 * LANES] = jnp.where(own_q, qq, qx_ref[...])
    slope = slope_ref[...]
    for m_ref, a_ref in ((ms_scr, as_scr), (mw_scr, aw_scr)):
        m_ref[...] = jnp.full(m_ref.shape, M_INIT, F32)
        a_ref[...] = jnp.zeros(a_ref.shape, F32)

    key_l = lax.broadcasted_iota(jnp.int32, (KEY_CHUNK, GQ), 0)
    q_l = lax.broadcasted_iota(jnp.int32, (KEY_CHUNK, GQ), 1) & (Q_TILE - 1)
    own_v = (lax.broadcasted_iota(jnp.int32, (LANES, KEY_CHUNK), 0) // HEAD_DIM) == par

    def step(pieces, m_ref, a_ref):
        shifts = [jnp.where(live, slope * ((c - j) * KEY_CHUNK).astype(F32), -jnp.inf) for _, c, live, _ in pieces]
        m_prev = m_ref[...]
        m_new = m_prev
        for (st, _, _, _), shift in zip(pieces, shifts):
            m_new = jnp.maximum(m_new, jnp.max(st, axis=0, keepdims=True) + shift)
        p = jnp.concatenate([jnp.exp(st - (m_new - shift)).astype(BF16)
                             for (st, _, _, _), shift in zip(pieces, shifts)], axis=0)
        vo = jnp.concatenate([jnp.where(own_v, vt, 1.0).astype(BF16) for _, _, _, vt in pieces], axis=1)
        a_ref[...] = jnp.exp(m_prev - m_new) * a_ref[...] + _bdot(vo, p)
        m_ref[...] = m_new

    def sel_piece(c, live, diag=False):
        k0 = pl.multiple_of(c * KEY_CHUNK, KEY_CHUNK)
        st = _dot_nt(ksa_scr[pl.ds(k0, KEY_CHUNK), :], qaug_scr[...])
        if diag:
            st = jnp.where(key_l <= q_l, st, NEG)
        return st, c, live, vst_ref[:, pl.ds(k0, KEY_CHUNK)]

    word0 = ((pl.program_id(0) * NSA_KV_HEADS + k) * nq + j) * _flag_words(nq)

    def scan(c, n):
        act = (flags_ref[word0 + c // 32] >> (c % 32)) & 1

        @pl.when(act == 1)
        def _():
            list_scr[n] = c

        return n + act

    n_act = lax.fori_loop(0, j, scan, 0)

    def sel_body(i, carry):
        pieces = []
        for u in range(SEL_UNROLL):
            idx = i * SEL_UNROLL + u
            pieces.append(sel_piece(list_scr[jnp.minimum(idx, n_act - 1)], idx < n_act))
        step(pieces, ms_scr, as_scr)
        return carry

    lax.fori_loop(0, (n_act + SEL_UNROLL - 1) // SEL_UNROLL, sel_body, 0)
    step([sel_piece(j, True, diag=True)], ms_scr, as_scr)

    n_win = NSA_WINDOW // KEY_CHUNK
    pieces = []
    for dc in range(n_win + 1):
        c = j - n_win + dc
        cc = jnp.maximum(c, 0)
        k0 = pl.multiple_of(cc * KEY_CHUNK, KEY_CHUNK)
        st = _dot_nt(kwa_scr[pl.ds(k0, KEY_CHUNK), :], qaug_scr[:, LANES:2 * LANES])
        if dc == 0:
            st = jnp.where(key_l > q_l, st, NEG)
        elif dc == n_win:
            st = jnp.where(key_l <= q_l, st, NEG)
        pieces.append((st, cc, c >= 0, vwt_ref[:, pl.ds(k0, KEY_CHUNK)]))
    step(pieces, mw_scr, aw_scr)

    own0 = pl.multiple_of(par * HEAD_DIM, HEAD_DIM)
    oth0 = pl.multiple_of((1 - par) * HEAD_DIM, HEAD_DIM)
    o_s = as_scr[pl.ds(own0, HEAD_DIM), :] / as_scr[pl.ds(oth0, 1), :]
    o_w = aw_scr[pl.ds(own0, HEAD_DIM), :] / aw_scr[pl.ds(oth0, 1), :]
    gates_t = gate_ref[...].T
    oc_t = oc_ref[...].T
    mix = []
    for g in range(NSA_GROUP):
        cols = slice(g * Q_TILE, (g + 1) * Q_TILE)
        mix.append(gates_t[3 * g:3 * g + 1] * oc_t[g * HEAD_DIM:(g + 1) * HEAD_DIM]
                   + gates_t[3 * g + 1:3 * g + 2] * o_s[:, cols] + gates_t[3 * g + 2:3 * g + 3] * o_w[:, cols])
    o_ref[...] = jnp.concatenate(mix, axis=0).T.astype(o_ref.dtype)


def _flag_words(nchunks):
    return -(-nchunks // 32)


def _bf16_pieces(x):
    def rnd(v):
        return np.asarray(v, np.float32).astype(jnp.bfloat16).astype(np.float32)
    a = rnd(x)
    b = rnd(x - a)
    return a, b, rnd(x - a - b)


def _nsa_fold_constants(t):
    posx = np.zeros((KEY_CHUNK, LANES), np.float32)
    loc = np.arange(KEY_CHUNK, dtype=np.float32)
    slopes = _alibi_slopes()
    qx = np.zeros((NSA_KV_HEADS, GQ, LANES), np.float32)
    ql = np.tile(np.arange(Q_TILE, dtype=np.float32), NSA_GROUP)
    for k in range(NSA_KV_HEADS):
        srow = np.repeat(slopes[k], Q_TILE)
        pieces = _bf16_pieces(srow) + _bf16_pieces(-srow * ql)
        for base in (0, HEAD_DIM):
            for i, pc in enumerate(pieces):
                qx[k, :, base + i] = pc
    for base in (0, HEAD_DIM):
        posx[:, base:base + 3] = loc[:, None]
        posx[:, base + 3:base + 6] = 1.0
    onehot = (np.arange(t)[:, None] // NSA_BLOCK == np.arange(LANES)[None, :]).astype(np.float32)
    return jnp.asarray(posx, BF16), jnp.asarray(onehot, BF16), jnp.asarray(qx, BF16)


def _pack_chunk_flags(any_sel, nchunks):
    blocks_per_chunk = KEY_CHUNK // NSA_BLOCK
    f = any_sel[:, :, :, 0, :nchunks * blocks_per_chunk]
    f = f.reshape(f.shape[:3] + (nchunks, blocks_per_chunk)).max(axis=-1) > 0.0
    words = _flag_words(nchunks)
    f = jnp.pad(f, ((0, 0), (0, 0), (0, 0), (0, words * 32 - nchunks)))
    bits = f.reshape(f.shape[:3] + (words, 32)).astype(jnp.uint32) << jnp.arange(32, dtype=jnp.uint32)
    return lax.bitcast_convert_type(bits.sum(axis=-1, dtype=jnp.uint32), jnp.int32).reshape(-1)


def _nsa_selected_window(b, t, q, selb, any_sel, kvs, kvs_t, kvw, kvw_t, oc, gates):
    assert Q_TILE == KEY_CHUNK
    nq = t // Q_TILE
    slopes = jnp.asarray(np.repeat(_alibi_slopes(), Q_TILE, axis=1).reshape(NSA_KV_HEADS, 1, GQ))
    posx, onehot, qx = _nsa_fold_constants(t)
    flags = _pack_chunk_flags(any_sel, nq)
    pairs = KVW // LANES
    qspec = pl.BlockSpec((Q_TILE, KVW), lambda bi, k, j, fl: (bi * nq + j, k))
    kspec = pl.BlockSpec((None, t, LANES), lambda bi, k, j, fl: (bi, 0, k // 2))
    vspec = pl.BlockSpec((None, LANES, t), lambda bi, k, j, fl: (bi, pairs + k // 2, 0))
    const = lambda shape: pl.BlockSpec(shape, lambda bi, k, j, fl: (0,) * len(shape))
    return pl.pallas_call(
        functools.partial(_nsa_sw_kernel, nq),
        grid_spec=pltpu.PrefetchScalarGridSpec(
            num_scalar_prefetch=1,
            grid=(b, NSA_KV_HEADS, nq),
            in_specs=[qspec,
                      pl.BlockSpec((None, None, Q_TILE, LANES), lambda bi, k, j, fl: (bi, k, j, 0)),
                      kspec, vspec, kspec, vspec,
                      const((KEY_CHUNK, LANES)), const((t, LANES)),
                      pl.BlockSpec((None, GQ, LANES), lambda bi, k, j, fl: (k, 0, 0)),
                      qspec,
                      pl.BlockSpec((Q_TILE, LANES), lambda bi, k, j, fl: (bi * nq + j, k)),
                      pl.BlockSpec((None, 1, GQ), lambda bi, k, j, fl: (k, 0, 0))],
            out_specs=qspec,
            scratch_shapes=[pltpu.VMEM((t, 2 * LANES), BF16), pltpu.VMEM((t, LANES), BF16),
                            pltpu.VMEM((GQ, 2 * LANES), BF16),
                            pltpu.VMEM((1, GQ), F32), pltpu.VMEM((LANES, GQ), F32),
                            pltpu.VMEM((1, GQ), F32), pltpu.VMEM((LANES, GQ), F32),
                            pltpu.SMEM((nq,), jnp.int32)]),
        out_shape=jax.ShapeDtypeStruct((b * t, D_MODEL), BF16),
        compiler_params=_cparams("parallel", "arbitrary", "arbitrary"),
        name="nsa_selected_window",
    )(flags, q, selb, kvs, kvs_t, kvw, kvw_t, posx, onehot, qx, oc, gates, slopes)


def _nsa_weights(w_in):
    w = w_in.astype(BF16)
    ng = NSA_GROUP * 3
    gcols = [jnp.pad(w[:, NSA_G0 + k * ng:NSA_G0 + (k + 1) * ng], ((0, 0), (0, LANES - ng)))
             for k in range(NSA_KV_HEADS)]
    return jnp.concatenate([w[:, :NSA_G0]] + gcols, axis=1)


def _nsa_proj_outs(kv_dtypes):
    return ((NSA_Q0, NSA_C0, "scale", (BF16,)),
            (NSA_C0, NSA_S0, "", kv_dtypes), (NSA_S0, NSA_W0, "", kv_dtypes), (NSA_W0, NSA_G0, "", kv_dtypes),
            (NSA_G0, NSA_COLS, "sigmoid", (F32,)))


NO_BIAS = np.zeros((1, LANES), np.float32)


def _leaf_from_feat_major(a, feat_shape):
    return jnp.moveaxis(a.reshape(a.shape[:1] + tuple(feat_shape) + a.shape[2:]), -1, 1)


def _compress_blocks(kvc, nseq, nb, w_ck, w_cv):
    blocks = kvc.reshape(nseq, nb, NSA_BLOCK, 2, NSA_KV_HEADS, HEAD_DIM).transpose(3, 0, 1, 4, 2, 5)
    blocks = blocks.reshape(2, nseq * nb * NSA_KV_HEADS, NSA_BLOCK * HEAD_DIM)
    tm = min(256, blocks.shape[1])
    out = []
    for a, w in ((blocks[0], w_ck), (blocks[1], w_cv)):
        c = _matmul(a, w.astype(BF16), tm).reshape(nseq, nb, NSA_KV_HEADS, HEAD_DIM).transpose(0, 2, 1, 3)
        out.append(jnp.concatenate([c, c], axis=-1).astype(BF16))
    return out


def _nsa_prompt(rows, x, g, mod, w_in, w_ck, w_cv):
    b, t = rows.nseq, rows.t
    q, kvc_t, kvc_b, kvs_t, kvs_b, kvw_t, kvw_b, gates = _norm_proj(
        rows, x, g, mod, (0, 1), _nsa_weights(w_in), jnp.asarray(NO_BIAS), _nsa_proj_outs((FEAT_MAJOR, BF16)))
    kcc, vcc = _compress_blocks(kvc_b, b, t // NSA_BLOCK, w_ck, w_cv)
    oc, selb, any_sel = _nsa_compressed(b, t, q, kcc, vcc)
    o = _nsa_selected_window(b, t, q, selb, any_sel, kvs_b.reshape(b, t, 2 * KVW), kvs_t,
                             kvw_b.reshape(b, t, 2 * KVW), kvw_t, oc, gates)
    return o, kvc_t, kvs_t, kvw_t


FOX_TQ = 512
FOX_TK = 512
FOX_UNROLL = 2
CUM_TILE = 256
FOX_PAIRS = FOX_HEADS // 2

def _fox_proj_outs(kv_dtypes):
    return ((0, D_MODEL, "scale", (BF16,)), (D_MODEL, FOX_LF0, "", kv_dtypes), (FOX_LF0, FOX_COLS, "logsigmoid", (F32,)))


def _fox_weights(w_in, b_f):
    w = jnp.pad(w_in.astype(BF16), ((0, 0), (0, FOX_COLS - w_in.shape[1])))
    bias = jnp.pad(b_f.astype(F32), (0, LANES - FOX_HEADS)).reshape(1, LANES)
    return w, bias


def _tri_cumsum(x):
    n = x.shape[0]
    tri = (lax.broadcasted_iota(jnp.int32, (n, n), 1) <= lax.broadcasted_iota(jnp.int32, (n, n), 0)).astype(BF16)
    a, b, c = _split3(x)
    return _bdot(tri, a) + _bdot(tri, b) + _bdot(tri, c)


def _cumsum_kernel(x_ref, o_ref, carry_scr):
    @pl.when(pl.program_id(1) == 0)
    def _():
        carry_scr[...] = jnp.zeros_like(carry_scr)

    cum = _tri_cumsum(x_ref[...]) + carry_scr[0:1, :]
    o_ref[...] = cum
    carry_scr[...] = jnp.broadcast_to(cum[-1:, :], carry_scr.shape)


def _cumsum_rows(x, nseq, t):
    nt = t // CUM_TILE
    return pl.pallas_call(
        _cumsum_kernel,
        grid=(nseq, nt),
        in_specs=[pl.BlockSpec((CUM_TILE, LANES), lambda s, i: (s * nt + i, 0))],
        out_specs=pl.BlockSpec((CUM_TILE, LANES), lambda s, i: (s * nt + i, 0)),
        out_shape=jax.ShapeDtypeStruct(x.shape, F32),
        scratch_shapes=[pltpu.VMEM((SUBLANES, LANES), F32)],
        compiler_params=_cparams("parallel", "arbitrary"),
        name="cumsum_rows",
    )(x)


def _fox_kernel(nk, q_ref, k_ref, vt_ref, ccol_ref, crow_ref, o_ref, ka_scr, base_scr, qa_scr, m_scr, acc_scr):
    qi = pl.program_id(2)
    tq = q_ref.shape[0]
    lane = lax.broadcasted_iota(jnp.int32, (FOX_TK, LANES), 1)

    def own(e):
        return (lane // HEAD_DIM) == e

    def other_lane(e, i):
        return lane == (1 - e) * HEAD_DIM + i

    @pl.when(qi == 0)
    def _():
        def fill(c, carry):
            k0 = pl.multiple_of(c * FOX_TK, FOX_TK)
            kc = k_ref[pl.ds(k0, FOX_TK), :].astype(F32)
            for e in range(2):
                col = ccol_ref[pl.ds(k0, FOX_TK), e:e + 1]
                base = col[0:1, :]
                ext = jnp.zeros((FOX_TK, LANES), F32)
                for i, piece in enumerate(_split3(base - col)):
                    ext = jnp.where(other_lane(e, i), piece.astype(F32), ext)
                ka_scr[e, pl.ds(k0, FOX_TK), :] = jnp.where(own(e), kc, ext).astype(BF16)
                base_scr[e, pl.ds(c, 1), :] = jnp.broadcast_to(base, (1, LANES))
            return carry

        lax.fori_loop(0, nk, fill, 0)

    q = q_ref[...].astype(F32)
    ones3 = jnp.zeros((tq, LANES), F32)
    for e in range(2):
        ext = ones3
        for i in range(3):
            ext = jnp.where(other_lane(e, i), 1.0, ext)
        qa_scr[e] = jnp.where(own(e), q, ext).astype(BF16)
    m_scr[...] = jnp.full(m_scr.shape, M_INIT, F32)
    acc_scr[...] = jnp.zeros(acc_scr.shape, F32)
    q0 = pl.multiple_of(qi * tq, tq)
    row = lax.broadcasted_iota(jnp.int32, (LANES, FOX_TK), 0)

    def step(chunks, diag):
        starts = [pl.multiple_of(c * FOX_TK, FOX_TK) for c, _ in chunks]
        vts = [vt_ref[:, pl.ds(k0, FOX_TK)] for k0 in starts]
        for e in range(2):
            sts, shifts = [], []
            for (c, live), k0 in zip(chunks, starts):
                st = _dot_nt(ka_scr[e, pl.ds(k0, FOX_TK), :], qa_scr[e])
                if diag:
                    st = jnp.where(lax.broadcasted_iota(jnp.int32, st.shape, 0)
                                   <= lax.broadcasted_iota(jnp.int32, st.shape, 1), st, NEG)
                base = base_scr[e, pl.ds(c, 1), :]
                shift = crow_ref[e:e + 1, pl.ds(q0, tq)] - jnp.concatenate([base] * (tq // LANES), axis=1)
                sts.append(st)
                shifts.append(jnp.where(live, shift, -jnp.inf))
            m_prev = m_scr[e]
            m_new = m_prev
            for st, shift in zip(sts, shifts):
                m_new = jnp.maximum(m_new, jnp.max(st, axis=0, keepdims=True) + shift)
            p = jnp.concatenate([jnp.exp(st - (m_new - shift)).astype(BF16) for st, shift in zip(sts, shifts)], axis=0)
            vo = jnp.concatenate([jnp.where((row // HEAD_DIM) == e, vt, 1.0).astype(BF16) for vt in vts], axis=1)
            acc_scr[e] = jnp.exp(m_prev - m_new) * acc_scr[e] + _bdot(vo, p)
            m_scr[e] = m_new

    def body(i, carry):
        c0 = i * FOX_UNROLL
        step([(jnp.minimum(c0 + u, qi - 1), c0 + u < qi) for u in range(FOX_UNROLL)], False)
        return carry

    lax.fori_loop(0, (qi + FOX_UNROLL - 1) // FOX_UNROLL, body, 0)
    step([(qi, True)], True)
    a0, a1 = acc_scr[0], acc_scr[1]
    r = lax.broadcasted_iota(jnp.int32, a0.shape, 0)
    o_t = jnp.where(r < HEAD_DIM, a0 / a0[HEAD_DIM:HEAD_DIM + 1], a1 / a1[0:1])
    o_ref[...] = o_t.T.astype(o_ref.dtype)


def _fox_attention(b, t, q, k, kv_t, cum_col, cum_row):
    assert FOX_TQ == FOX_TK and t % FOX_TQ == 0
    tq = FOX_TQ
    nq = t // tq
    vrow0 = D_MODEL // LANES
    return pl.pallas_call(
        functools.partial(_fox_kernel, nq),
        grid=(b, FOX_PAIRS, nq),
        in_specs=[pl.BlockSpec((tq, LANES), lambda bi, hp, qi: (bi * nq + qi, hp)),
                  pl.BlockSpec((None, t, LANES), lambda bi, hp, qi: (bi, 0, hp)),
                  pl.BlockSpec((None, LANES, t), lambda bi, hp, qi: (bi, vrow0 + hp, 0)),
                  pl.BlockSpec((None, None, t, 2), lambda bi, hp, qi: (bi, hp, 0, 0)),
                  pl.BlockSpec((None, None, 2, t), lambda bi, hp, qi: (bi, hp, 0, 0))],
        out_specs=pl.BlockSpec((tq, LANES), lambda bi, hp, qi: (bi * nq + qi, hp)),
        out_shape=jax.ShapeDtypeStruct((b * t, D_MODEL), BF16),
        scratch_shapes=[pltpu.VMEM((2, t, LANES), BF16), pltpu.VMEM((2, max(nq, SUBLANES), LANES), F32),
                        pltpu.VMEM((2, tq, LANES), BF16), pltpu.VMEM((2, 1, tq), F32),
                        pltpu.VMEM((2, LANES, tq), F32)],
        compiler_params=_cparams("parallel", "arbitrary", "arbitrary"),
        name="fox_attention",
    )(q, k, kv_t, cum_col, cum_row)


def _fox_prompt(rows, x, g, mod, w_in, b_f):
    b, t = rows.nseq, rows.t
    w, bias = _fox_weights(w_in, b_f)
    q, kv_t, kv_b, logf = _norm_proj(rows, x, g, mod, (0, 1), w, bias, _fox_proj_outs((FEAT_MAJOR, BF16)))
    cum = _cumsum_rows(logf, b, t)[:, :FOX_HEADS].reshape(b, t, FOX_PAIRS, 2)
    o = _fox_attention(b, t, q, kv_b.reshape(b, t, 2 * D_MODEL), kv_t,
                       cum.transpose(0, 2, 1, 3), cum.transpose(0, 2, 3, 1))
    return o, kv_t, logf[:, :FOX_HEADS]


def _pool_kernel(tps, pos0, h_ref, w_ref, b_ref, scale_ref, o_ref, ext_scr):
    i = pl.program_id(0)
    tm = h_ref.shape[0]
    halo = 2 * SUBLANES
    first = (i % tps) == 0

    @pl.when(first)
    def _():
        ext_scr[0:halo, :] = jnp.zeros((halo, D_MODEL), F32)

    @pl.when(jnp.logical_not(first))
    def _():
        ext_scr[0:halo, :] = ext_scr[tm:tm + halo, :]

    ext_scr[halo:halo + tm, :] = h_ref[...]
    pos = pos0 + (i % tps) * tm + lax.broadcasted_iota(jnp.int32, (tm, 1), 0)
    for gi, w in enumerate(POOL_WINDOWS):
        c0, c1 = gi * POOL_GROUP_DIM, (gi + 1) * POOL_GROUP_DIM
        win = ext_scr[halo:halo + tm, c0:c1]
        for back in range(1, w):
            win = win + ext_scr[halo - back:halo - back + tm, c0:c1]
        cnt = jnp.minimum(w, pos + 1).astype(F32)
        mixed = win / cnt - h_ref[:, c0:c1]
        y = _bdot(mixed.astype(BF16), w_ref[gi]) + b_ref[:, c0:c1]
        o_ref[:, c0:c1] = y * scale_ref[:, c0:c1]


def _pool_mix(h_ext, nseq, t, tm, pos0, w_g, b_g, scale):
    assert POOL_STATE < 2 * SUBLANES
    tps = t // tm
    full = lambda shape: pl.BlockSpec(shape, lambda i: (0,) * len(shape))
    return pl.pallas_call(
        functools.partial(_pool_kernel, tps, pos0),
        grid=(nseq * tps,),
        in_specs=[pl.BlockSpec((tm, D_MODEL), lambda i: (i, 0)),
                  full((len(POOL_WINDOWS), POOL_GROUP_DIM, POOL_GROUP_DIM)), full((1, D_MODEL)), full((1, D_MODEL))],
        out_specs=pl.BlockSpec((tm, D_MODEL), lambda i: (i, 0)),
        out_shape=jax.ShapeDtypeStruct((nseq * t, D_MODEL), F32),
        scratch_shapes=[pltpu.VMEM((tm + 2 * SUBLANES, D_MODEL), F32)],
        compiler_params=_cparams("arbitrary"),
        name="pool_mix",
    )(h_ext, w_g.astype(BF16), b_g.reshape(1, D_MODEL), scale.reshape(1, D_MODEL))


def _residual_kernel(o_ref, x_ref, g_ref, gate_ref, y_ref):
    y_ref[...] = x_ref[...] + gate_ref[...] * _rms_rows(o_ref[...], g_ref[...])


def _gated_residual(rows, o, x, g, mod, gate_chunk):
    full = lambda shape: pl.BlockSpec(shape, lambda i: (0,) * len(shape))
    return pl.pallas_call(
        _residual_kernel,
        grid=(rows.ntiles,),
        in_specs=[rows.row_spec(D_MODEL), rows.row_spec(D_MODEL), full((1, D_MODEL)), rows.mod_spec(gate_chunk)],
        out_specs=rows.row_spec(D_MODEL),
        out_shape=jax.ShapeDtypeStruct((rows.m, D_MODEL), F32),
        compiler_params=_cparams("parallel"),
        name="gated_residual",
    )(o, x, g.reshape(1, D_MODEL), rows.mod_arr(mod))


H_ONLY = ((0, D_MODEL, "h", (F32,)),)


def _norm_only(rows, x, g, mod, chunks):
    dummy = jnp.zeros((D_MODEL, LANES), BF16)
    return _norm_proj(rows, x, g, mod, chunks, dummy, jnp.asarray(NO_BIAS), H_ONLY)[0]


DEC_B = 8
NEW_ROWS = SUBLANES


def _paged_grid_spec(grid, in_specs, out_specs, scratch_shapes):
    return pltpu.PrefetchScalarGridSpec(num_scalar_prefetch=1, grid=grid, in_specs=in_specs,
                                        out_specs=out_specs, scratch_shapes=scratch_shapes)


def _feat_major_pages(cache):
    n, p = cache.shape[:2]
    return jnp.moveaxis(cache, 1, -1).reshape(n, -1, p)


def _page_specs(npg, rows):
    return [pl.BlockSpec((None, rows, PAGE_SIZE), functools.partial(lambda b, pt, i: (pt[b * npg + i], 0, 0), i=i))
            for i in range(npg)]


def _cmp_dec_kernel(npg, pt_ref, *refs):
    pages, (wk_ref, wv_ref, o_ref, lhs_scr) = refs[:npg], refs[npg:]
    groups = 2 * NSA_KV_HEADS
    for d in range(HEAD_DIM):
        for i, page in enumerate(pages):
            lhs_scr[i * groups:(i + 1) * groups, d * PAGE_SIZE:(d + 1) * PAGE_SIZE] = page[pl.ds(d, groups, stride=HEAD_DIM), :]
    lhs = lhs_scr[...].astype(BF16)
    is_key = (lax.broadcasted_iota(jnp.int32, o_ref.shape, 0) % groups) < NSA_KV_HEADS
    o_ref[...] = jnp.where(is_key, _bdot(lhs, wk_ref[...]), _bdot(lhs, wv_ref[...])).astype(o_ref.dtype)


def _cmp_dec_weight(w):
    w3 = w.astype(BF16).reshape(NSA_BLOCK, HEAD_DIM, HEAD_DIM)
    halves = PAGE_SIZE // NSA_BLOCK
    eye = jnp.eye(halves, dtype=BF16)
    return jnp.einsum("rde,hg->dhrge", w3, eye).reshape(HEAD_DIM * PAGE_SIZE, halves * HEAD_DIM)


def _cmp_decode(page_table, cache_t, w_ck, w_cv):
    ns, npg = page_table.shape
    halves = PAGE_SIZE // NSA_BLOCK
    groups = 2 * NSA_KV_HEADS
    wspec = pl.BlockSpec((HEAD_DIM * PAGE_SIZE, halves * HEAD_DIM), lambda b, pt: (0, 0))
    out = pl.pallas_call(
        functools.partial(_cmp_dec_kernel, npg),
        grid_spec=_paged_grid_spec(
            (ns,), _page_specs(npg, 2 * KVW) + [wspec, wspec],
            pl.BlockSpec((None, npg * groups, halves * HEAD_DIM), lambda b, pt: (b, 0, 0)),
            [pltpu.VMEM((npg * groups, HEAD_DIM * PAGE_SIZE), F32)]),
        out_shape=jax.ShapeDtypeStruct((ns, npg * groups, halves * HEAD_DIM), BF16),
        compiler_params=_cparams("parallel"),
        name="nsa_decode_compress",
    )(page_table.reshape(-1), *([cache_t] * npg), _cmp_dec_weight(w_ck), _cmp_dec_weight(w_cv))
    out = out.reshape(ns, npg, 2, NSA_KV_HEADS, halves, HEAD_DIM).transpose(0, 1, 4, 2, 3, 5)
    return out.reshape(ns, npg * halves, 2 * KVW)


def _dec_select_kernel(qpos, nblk, qbd_ref, kcv_ref, slope_ref, oc_ref, selb_ref, imp_scr):
    lane_b = lax.broadcasted_iota(jnp.int32, (N_HEADS, nblk), 1)
    ends = lane_b * NSA_BLOCK + (NSA_BLOCK - 1)
    mask = ends <= qpos
    bias = slope_ref[...] * (qpos - ends).astype(F32)
    imp_scr[...] = jnp.zeros(imp_scr.shape, F32)
    for bi in range(DEC_B):
        s = _dot_nt(qbd_ref[bi], kcv_ref[bi, :, 0:KVW]) - bias
        s = jnp.where(mask, s, NEG)
        e = jnp.exp(s - jnp.max(s, axis=-1, keepdims=True))
        p = jnp.where(mask, e / jnp.sum(e, axis=-1, keepdims=True), 0.0)
        oc_ref[bi] = _bdot(p.astype(BF16), kcv_ref[bi, :, KVW:2 * KVW])
        imp = p[0:NSA_KV_HEADS]
        for g in range(1, NSA_GROUP):
            imp = imp + p[g * NSA_KV_HEADS:(g + 1) * NSA_KV_HEADS]
        imp_scr[bi * NSA_KV_HEADS:(bi + 1) * NSA_KV_HEADS, 0:nblk] = imp

    rows = DEC_B * NSA_KV_HEADS
    n2 = lax.broadcasted_iota(jnp.int32, (rows, LANES), 1)
    cur = qpos // NSA_BLOCK
    forced = (n2 == 0) | (n2 == cur) | (n2 == cur - 1)
    score = jnp.where(forced, BIG, jnp.where(n2 <= cur, imp_scr[...], NEG))
    score = jnp.where(n2 <= cur, score, -jnp.inf)

    def pick_next(_, carry):
        sc, sel = carry
        m = jnp.max(sc, axis=-1, keepdims=True)
        first = jnp.min(jnp.where(sc == m, n2, LANES), axis=-1, keepdims=True)
        pick = n2 == first
        sel = jnp.where(pick & (m > 0.5 * NEG), 1.0, sel)
        return jnp.where(pick, -jnp.inf, sc), sel

    _, sel = lax.fori_loop(0, min(NSA_TOPN, cur + 1), pick_next, (score, jnp.zeros((rows, LANES), F32)), unroll=True)
    selb = jnp.where(sel > 0.0, 0.0, NEG).astype(BF16)
    for bi in range(DEC_B):
        one = selb[bi * NSA_KV_HEADS:(bi + 1) * NSA_KV_HEADS]
        selb_ref[bi] = jnp.concatenate([one] * NSA_GROUP, axis=0)


def _dec_slopes():
    return jnp.asarray(_alibi_slopes().T.reshape(N_HEADS, 1))


def _nsa_decode_select(qbd, kcv, qpos):
    ns, nblk = kcv.shape[0], kcv.shape[1]
    assert qpos // NSA_BLOCK < LANES and ns % DEC_B == 0
    blk = lambda shape: pl.BlockSpec((DEC_B,) + shape, lambda i: (i, 0, 0))
    return pl.pallas_call(
        functools.partial(_dec_select_kernel, qpos, nblk),
        grid=(ns // DEC_B,),
        in_specs=[blk((N_HEADS, KVW)), blk((nblk, 2 * KVW)), pl.BlockSpec((N_HEADS, 1), lambda i: (0, 0))],
        out_specs=[blk((N_HEADS, KVW)), blk((N_HEADS, LANES))],
        out_shape=[jax.ShapeDtypeStruct((ns, N_HEADS, KVW), F32), jax.ShapeDtypeStruct((ns, N_HEADS, LANES), BF16)],
        scratch_shapes=[pltpu.VMEM((DEC_B * NSA_KV_HEADS, LANES), F32)],
        compiler_params=_cparams("parallel"),
        name="nsa_decode_select",
    )(qbd, kcv, _dec_slopes())


def _new_key_tile(row):
    r = lax.broadcasted_iota(jnp.int32, (NEW_ROWS, row.shape[1]), 0)
    return jnp.where(r == 0, jnp.broadcast_to(row, (NEW_ROWS, row.shape[1])), 0.0).astype(BF16)


def _new_key_mask(s):
    return jnp.where(lax.broadcasted_iota(jnp.int32, s.shape, 1) == 0, s, NEG)


def _softmax_with_new_key(s, s_new):
    m = jnp.maximum(jnp.max(s, axis=-1, keepdims=True), jnp.max(s_new, axis=-1, keepdims=True))
    p, p_new = jnp.exp(s - m), jnp.exp(s_new - m)
    return p, p_new, jnp.sum(p, axis=-1, keepdims=True) + jnp.sum(p_new, axis=-1, keepdims=True)


def _dec_attend_kernel(npg, qpos, wlen, pt_ref, *refs):
    pages = refs[:npg]
    (qbd_ref, selb_ref, win_ref, snew_ref, wnew_ref, wcol_ref, oc_ref, gate_ref, slope_ref,
     o_ref, wout_ref) = refs[npg:]
    slope = slope_ref[...]
    qbd = qbd_ref[...]

    qaug = jnp.concatenate([qbd, selb_ref[...]], axis=1)
    blk_row = lax.broadcasted_iota(jnp.int32, (LANES, PAGE_SIZE), 0)
    blk_of_lane = lax.broadcasted_iota(jnp.int32, (LANES, PAGE_SIZE), 1) // NSA_BLOCK
    scores = []
    for i, page in enumerate(pages):
        onehot_t = (blk_row == (PAGE_SIZE // NSA_BLOCK) * i + blk_of_lane).astype(BF16)
        kaug_t = jnp.concatenate([page[0:KVW, :].astype(BF16), onehot_t], axis=0)
        scores.append(_bdot(qaug, kaug_t))
    kpos = lax.broadcasted_iota(jnp.int32, (N_HEADS, npg * PAGE_SIZE), 1)
    s = jnp.concatenate(scores, axis=1) - slope * (qpos - kpos).astype(F32)
    snew = snew_ref[...]
    s_new = _new_key_mask(_dot_nt(qbd, _new_key_tile(snew[:, 0:KVW])))
    p, p_new, l = _softmax_with_new_key(s, s_new)
    p = p.astype(BF16)
    acc = _bdot(p_new.astype(BF16), _new_key_tile(snew[:, KVW:2 * KVW]))
    for i, page in enumerate(pages):
        acc = acc + _dot_nt(p[:, i * PAGE_SIZE:(i + 1) * PAGE_SIZE], page[KVW:2 * KVW, :].astype(BF16))
    o_s = acc / l

    win = win_ref[...]
    wpos = qpos - wlen + lax.broadcasted_iota(jnp.int32, (N_HEADS, wlen), 1)
    s = _bdot(qbd, win[0:KVW, :].astype(BF16)) - slope * (qpos - wpos).astype(F32)
    s = jnp.where(wpos > qpos - NSA_WINDOW, s, NEG)
    wnew = wnew_ref[...]
    s_new = _new_key_mask(_dot_nt(qbd, _new_key_tile(wnew[:, 0:KVW])))
    p, p_new, l = _softmax_with_new_key(s, s_new)
    o_w = (_dot_nt(p.astype(BF16), win[KVW:2 * KVW, :].astype(BF16))
           + _bdot(p_new.astype(BF16), _new_key_tile(wnew[:, KVW:2 * KVW]))) / l

    gates = gate_ref[...]
    o_ref[...] = gates[:, 0:1] * oc_ref[...] + gates[:, 1:2] * o_s + gates[:, 2:3] * o_w
    lane = lax.broadcasted_iota(jnp.int32, win.shape, 1)
    wout_ref[...] = jnp.where(lane == wlen - 1, wcol_ref[...], pltpu.roll(win, wlen - 1, axis=1))


def _nsa_decode_attend(page_table, qbd, selb, cache_t, win_t, kvs_new, kvw_new, oc, gates, qpos):
    ns, npg = page_table.shape
    wlen = win_t.shape[2]
    assert qpos == npg * PAGE_SIZE
    per_b = lambda shape: pl.BlockSpec((None,) + shape, lambda b, pt: (b, 0, 0))
    return pl.pallas_call(
        functools.partial(_dec_attend_kernel, npg, qpos, wlen),
        grid_spec=_paged_grid_spec(
            (ns,),
            _page_specs(npg, 2 * KVW)
            + [per_b((N_HEADS, KVW)), per_b((N_HEADS, LANES)), per_b((2 * KVW, wlen)), per_b((1, 2 * KVW)),
               per_b((1, 2 * KVW)), per_b((2 * KVW, 1)), per_b((N_HEADS, KVW)), per_b((N_HEADS, LANES)),
               pl.BlockSpec((N_HEADS, 1), lambda b, pt: (0, 0))],
            [per_b((N_HEADS, KVW)), per_b((2 * KVW, wlen))],
            []),
        out_shape=[jax.ShapeDtypeStruct((ns, N_HEADS, KVW), F32), jax.ShapeDtypeStruct(win_t.shape, F32)],
        compiler_params=_cparams("parallel"),
        name="nsa_decode_attend",
    )(page_table.reshape(-1), *([cache_t] * npg), qbd, selb, win_t, kvs_new.reshape(ns, 1, 2 * KVW),
      kvw_new.reshape(ns, 1, 2 * KVW), kvw_new.reshape(ns, 2 * KVW, 1), oc, gates, _dec_slopes())


def _nsa_sample(rows, x, g, mod, w_in, w_ck, w_cv, cache_c, cache_s, win_buf, page_table):
    ns = rows.nseq
    qpos = page_table.shape[1] * PAGE_SIZE
    q, kvc, kvs, kvw, gates = _norm_proj(
        rows, x, g, mod, (0, 1), _nsa_weights(w_in), jnp.asarray(NO_BIAS), _nsa_proj_outs((F32,)))
    kcv = _cmp_decode(page_table, _feat_major_pages(cache_c), w_ck, w_cv)
    q4 = q.reshape(ns, NSA_KV_HEADS, NSA_GROUP, HEAD_DIM)
    eye = jnp.eye(NSA_KV_HEADS, dtype=q.dtype)
    qbd = jnp.einsum("bkgd,kj->bgkjd", q4, eye).reshape(ns, N_HEADS, KVW)
    oc, selb = _nsa_decode_select(qbd, kcv, qpos)
    g4 = gates.reshape(ns, NSA_KV_HEADS, LANES)[:, :, :NSA_GROUP * 3].reshape(ns, NSA_KV_HEADS, NSA_GROUP, 3)
    g_rows = jnp.pad(g4.transpose(0, 2, 1, 3).reshape(ns, N_HEADS, 3), ((0, 0), (0, 0), (0, LANES - 3)))
    o_bd, win_out = _nsa_decode_attend(
        page_table, qbd, selb, _feat_major_pages(cache_s), _feat_major_pages(win_buf), kvs, kvw, oc, g_rows, qpos)
    o5 = o_bd.reshape(ns, NSA_GROUP, NSA_KV_HEADS, NSA_KV_HEADS, HEAD_DIM)
    o = jnp.einsum("bgkkd->bkgd", o5).reshape(ns, D_MODEL).astype(BF16)
    win_out = jnp.moveaxis(win_out.reshape(win_buf.shape[:1] + win_buf.shape[2:] + win_buf.shape[1:2]), -1, 1)
    return o, kvc, kvs, win_out


def _fox_dec_kernel(npg, pt_ref, *refs):
    kv_pages, lf_pages = refs[:npg], refs[npg:2 * npg]
    q_ref, kvn_ref, lfn_ref, o_ref = refs[2 * npg:]
    own = (lax.broadcasted_iota(jnp.int32, (FOX_HEADS, D_MODEL), 1) // HEAD_DIM
           == lax.broadcasted_iota(jnp.int32, (FOX_HEADS, D_MODEL), 0))
    q = jnp.broadcast_to(q_ref[...].astype(F32), (FOX_HEADS, D_MODEL))
    qbd = jnp.where(own, q, 0.0).astype(BF16)
    upper = (lax.broadcasted_iota(jnp.int32, (PAGE_SIZE, PAGE_SIZE), 0)
             <= lax.broadcasted_iota(jnp.int32, (PAGE_SIZE, PAGE_SIZE), 1)).astype(BF16)
    carry = jnp.zeros((FOX_HEADS, 1), F32)
    cums, scores = [], []
    for kv_page, lf_page in zip(kv_pages, lf_pages):
        a, b, c = _split3(lf_page[...])
        cum = carry + (_bdot(a, upper) + _bdot(b, upper) + _bdot(c, upper))
        carry = cum[:, PAGE_SIZE - 1:PAGE_SIZE]
        cums.append(cum)
        scores.append(_bdot(qbd, kv_page[0:D_MODEL, :].astype(BF16)))
    cum_new = carry + lfn_ref[...]
    s = jnp.concatenate(scores, axis=1) + (cum_new - jnp.concatenate(cums, axis=1))
    kvn = kvn_ref[...]
    s_new = _new_key_mask(_dot_nt(qbd, _new_key_tile(kvn[:, 0:D_MODEL])))
    p, p_new, l = _softmax_with_new_key(s, s_new)
    p = p.astype(BF16)
    acc = _bdot(p_new.astype(BF16), _new_key_tile(kvn[:, D_MODEL:2 * D_MODEL]))
    for i, kv_page in enumerate(kv_pages):
        acc = acc + _dot_nt(p[:, i * PAGE_SIZE:(i + 1) * PAGE_SIZE], kv_page[D_MODEL:2 * D_MODEL, :].astype(BF16))
    o_ref[...] = jnp.sum(jnp.where(own, acc / l, 0.0), axis=0, keepdims=True).astype(o_ref.dtype)


def _fox_decode(page_table, q, cache_kv_t, cache_logf_t, kv_new, logf_new):
    ns, npg = page_table.shape
    per_b = lambda shape: pl.BlockSpec((None,) + shape, lambda b, pt: (b, 0, 0))
    out = pl.pallas_call(
        functools.partial(_fox_dec_kernel, npg),
        grid_spec=_paged_grid_spec(
            (ns,),
            _page_specs(npg, 2 * D_MODEL) + _page_specs(npg, FOX_HEADS)
            + [per_b((1, D_MODEL)), per_b((1, 2 * D_MODEL)), per_b((FOX_HEADS, 1))],
            per_b((1, D_MODEL)), []),
        out_shape=jax.ShapeDtypeStruct((ns, 1, D_MODEL), BF16),
        compiler_params=_cparams("parallel"),
        name="fox_decode",
    )(page_table.reshape(-1), *([cache_kv_t] * npg), *([cache_logf_t] * npg), q.reshape(ns, 1, D_MODEL),
      kv_new.reshape(ns, 1, 2 * D_MODEL), logf_new.reshape(ns, FOX_HEADS, 1))
    return out.reshape(ns, D_MODEL)


def _fox_sample(rows, x, g, mod, w_in, b_f, cache_kv, cache_logf, page_table):
    w, bias = _fox_weights(w_in, b_f)
    q, kv, logf = _norm_proj(rows, x, g, mod, (0, 1), w, bias, _fox_proj_outs((F32,)))
    logf = logf[:, :FOX_HEADS]
    o = _fox_decode(page_table, q, _feat_major_pages(cache_kv), _feat_major_pages(cache_logf), kv, logf)
    return o, kv, logf


PROMPT_TM = 512
POOL_TM = 256


def kernel(x_prompt, x_sample, cache_l0_cmp_kv, cache_l0_sel_kv, state_l0_win_kv, cache_l1_kv, cache_l1_logf, state_l2_pool, cache_l3_cmp_kv, cache_l3_sel_kv, state_l3_win_kv, state_ffn_conv, page_table, c_prompt, c_sample, mod_w, mod_b, norm_g, l0_nsa_w_in, l0_nsa_w_ck, l0_nsa_w_cv, l0_nsa_w_o, l1_fox_w_in, l1_fox_b_f, l1_fox_w_o, l2_pool_w, l2_pool_b, l2_pool_scale, l3_nsa_w_in, l3_nsa_w_ck, l3_nsa_w_cv, l3_nsa_w_o, ffn_w_gu, ffn_conv_w, ffn_conv_b, ffn_w_d):
    b, t, _ = x_prompt.shape
    ns = x_sample.shape[0]
    past_len = page_table.shape[1] * PAGE_SIZE
    rp = _Rows(b, t, min(PROMPT_TM, t))
    rs = _Rows(ns, 1, ns)
    nsa = {0: (cache_l0_cmp_kv, cache_l0_sel_kv, state_l0_win_kv, l0_nsa_w_in, l0_nsa_w_ck, l0_nsa_w_cv, l0_nsa_w_o),
           3: (cache_l3_cmp_kv, cache_l3_sel_kv, state_l3_win_kv, l3_nsa_w_in, l3_nsa_w_ck, l3_nsa_w_cv, l3_nsa_w_o)}

    c_all = jnp.concatenate([c_prompt, c_sample], axis=0)
    c_all = jnp.pad(c_all, ((0, -c_all.shape[0] % SUBLANES), (0, 0)))
    mod = _modulation(c_all, mod_w, mod_b)

    xp = x_prompt.reshape(b * t, D_MODEL)
    xs = x_sample.reshape(ns, D_MODEL)
    st = {}
    conv_p, conv_s = [], []
    kv5 = lambda a, n: a.reshape(n, -1, 2, NSA_KV_HEADS, HEAD_DIM)
    for i in range(DEPTH):
        mp, ms = mod[i, :b], mod[i, b:b + ns]
        g = norm_g[i]
        kind = i % 3
        if kind == 0:
            c_c, c_s, s_w, w_in, w_ck, w_cv, w_o = nsa[i]
            op, kvc_t, kvs_t, kvw_t = _nsa_prompt(rp, xp, g[0], mp, w_in, w_ck, w_cv)
            os_, kvc_s, kvs_s, win_s = _nsa_sample(rs, xs, g[0], ms, w_in, w_ck, w_cv, c_c, c_s, s_w, page_table)
            leaf = lambda a: _leaf_from_feat_major(a, (2, NSA_KV_HEADS, HEAD_DIM))
            st[i] = (leaf(kvc_t), kv5(kvc_s, ns), leaf(kvs_t), kv5(kvs_s, ns),
                     leaf(kvw_t[:, :, -min(NSA_WINDOW, t):]), win_s)
            w_ob = w_o.astype(BF16)
            xp = _out_proj_residual(rp, op, w_ob, xp, g[1], mp, 2)
            xs = _out_proj_residual(rs, os_, w_ob, xs, g[1], ms, 2)
        elif kind == 1:
            op, kv_p, lf_p = _fox_prompt(rp, xp, g[0], mp, l1_fox_w_in, l1_fox_b_f)
            os_, kv_s, lf_s = _fox_sample(rs, xs, g[0], ms, l1_fox_w_in, l1_fox_b_f, cache_l1_kv, cache_l1_logf, page_table)
            st[i] = (_leaf_from_feat_major(kv_p, (2, FOX_HEADS, HEAD_DIM)), kv_s.reshape(ns, 1, 2, FOX_HEADS, HEAD_DIM),
                     lf_p.reshape(b, t, FOX_HEADS), lf_s.reshape(ns, 1, FOX_HEADS))
            w_ob = l1_fox_w_o.astype(BF16)
            xp = _out_proj_residual(rp, op, w_ob, xp, g[1], mp, 2)
            xs = _out_proj_residual(rs, os_, w_ob, xs, g[1], ms, 2)
        else:
            hp = _norm_only(rp, xp, g[0], mp, (0, 1))
            hs = _norm_only(rs, xs, g[0], ms, (0, 1))
            yp = _pool_mix(hp, b, t, min(POOL_TM, t), 0, l2_pool_w, l2_pool_b.reshape(-1), l2_pool_scale)
            ext = jnp.concatenate([state_l2_pool, hs[:, None, :]], axis=1)
            n_ext = POOL_STATE + 1
            ys = _pool_mix(ext.reshape(ns * n_ext, D_MODEL), ns, n_ext, n_ext, past_len - POOL_STATE,
                           l2_pool_w, l2_pool_b.reshape(-1), l2_pool_scale).reshape(ns, n_ext, D_MODEL)[:, -1]
            st[i] = (hp.reshape(b, t, D_MODEL)[:, -POOL_STATE:], ext[:, -POOL_STATE:])
            xp = _gated_residual(rp, yp, xp, g[1], mp, 2)
            xs = _gated_residual(rs, ys, xs, g[1], ms, 2)
        w_gu, w_d = ffn_w_gu[i].astype(BF16), ffn_w_d[i].astype(BF16)
        xp, cp = _conv_ffn(rp, xp, g[2], g[3], mp, w_gu, ffn_conv_w[i], ffn_conv_b[i], w_d, None)
        xs, cs = _conv_ffn(rs, xs, g[2], g[3], ms, w_gu, ffn_conv_w[i], ffn_conv_b[i], w_d, state_ffn_conv[i])
        conv_p.append(cp)
        conv_s.append(cs)
    return (xp.reshape(b, t, D_MODEL), xs.reshape(ns, 1, D_MODEL),
            *st[0], *st[1], *st[2], *st[3],
            jnp.stack(conv_p), jnp.stack(conv_s))
```

```python
import functools

import jax
import jax.numpy as jnp
import numpy as np
from jax import lax
from jax.experimental import pallas as pl
from jax.experimental.pallas import tpu as pltpu

D_MODEL = 1024
DEPTH = 4
PAGE_SIZE = 128
HEAD_DIM = 64
N_HEADS = D_MODEL // HEAD_DIM
NSA_KV_HEADS = 4
NSA_GROUP = N_HEADS // NSA_KV_HEADS
NSA_BLOCK = 64
NSA_TOPN = 16
NSA_WINDOW = 512
FOX_HEADS = D_MODEL // HEAD_DIM
POOL_WINDOWS = (2, 4, 8, 16)
POOL_GROUP_DIM = D_MODEL // len(POOL_WINDOWS)
POOL_STATE = max(POOL_WINDOWS) - 1
D_FF = 2816
CONV_W = 3
N_MOD = 6
RMS_EPS = 1e-6
NEG = -1e30
BIG = 1e30
ATTN_SCALE = HEAD_DIM ** -0.5

LANES = 128
SUBLANES = 8
VMEM_LIMIT = 56 * 1024 * 1024

KVW = NSA_KV_HEADS * HEAD_DIM
NSA_Q0, NSA_C0, NSA_S0, NSA_W0, NSA_G0 = 0, D_MODEL, D_MODEL + 2 * KVW, D_MODEL + 4 * KVW, D_MODEL + 6 * KVW
NSA_COLS = NSA_G0 + NSA_KV_HEADS * LANES
FOX_LF0 = 3 * D_MODEL
FOX_COLS = FOX_LF0 + LANES

F32 = jnp.float32
BF16 = jnp.bfloat16


def _cparams(*sem):
    return pltpu.CompilerParams(dimension_semantics=sem, vmem_limit_bytes=VMEM_LIMIT)


def _bdot(a, b):
    return jnp.dot(a, b, preferred_element_type=F32)


def _dot_nt(a, b):
    return lax.dot_general(a, b, (((1,), (1,)), ((), ())), preferred_element_type=F32)


def _rms_rows(x, g):
    return x * lax.rsqrt(jnp.mean(x * x, axis=-1, keepdims=True) + RMS_EPS) * g


def _split3(x):
    a = x.astype(BF16)
    r = x - a.astype(F32)
    b = r.astype(BF16)
    c = (r - b.astype(F32)).astype(BF16)
    return a, b, c


def _mod_kernel(c_ref, w_ref, b_ref, o_ref):
    c = c_ref[...]
    a = (c * jax.nn.sigmoid(c)).astype(BF16)
    o_ref[...] = _bdot(a, w_ref[...].astype(BF16)) + b_ref[...]


def _modulation(c_all, mod_w, mod_b):
    rows = c_all.shape[0]
    n = N_MOD * D_MODEL
    tn = D_MODEL
    return pl.pallas_call(
        _mod_kernel,
        grid=(DEPTH, n // tn),
        in_specs=[pl.BlockSpec((rows, D_MODEL), lambda i, j: (0, 0)),
                  pl.BlockSpec((None, D_MODEL, tn), lambda i, j: (i, 0, j)),
                  pl.BlockSpec((None, 1, tn), lambda i, j: (i, 0, j))],
        out_specs=pl.BlockSpec((None, rows, tn), lambda i, j: (i, 0, j)),
        out_shape=jax.ShapeDtypeStruct((DEPTH, rows, n), F32),
        compiler_params=_cparams("parallel", "parallel"),
        name="modulation",
    )(c_all, mod_w, mod_b.reshape(DEPTH, 1, n))


def _norm_h(x, g, shift, scale):
    return _rms_rows(x, g) * (1.0 + scale) + shift


class _Rows:
    def __init__(self, nseq, t, tm):
        assert t % tm == 0 or t == 1
        self.nseq, self.t = nseq, t
        self.decode = t == 1
        self.tm = nseq if self.decode else tm
        self.m = nseq * t
        self.tiles_per_seq = 1 if self.decode else t // tm
        self.ntiles = self.m // self.tm

    def mod_spec(self, chunk):
        if self.decode:
            return pl.BlockSpec((self.tm, D_MODEL), lambda i, *_: (0, chunk))
        tps = self.tiles_per_seq
        return pl.BlockSpec((None, 1, D_MODEL), lambda i, *_: (i // tps, 0, chunk))

    def mod_arr(self, mod):
        return mod if self.decode else mod.reshape(self.nseq, 1, N_MOD * D_MODEL)

    def row_spec(self, width, col=0):
        return pl.BlockSpec((self.tm, width), lambda i, *_: (i, col))


def _proj_kernel(outs, x_ref, g_ref, sh_ref, sc_ref, w_ref, bias_ref, *o_refs):
    h = _norm_h(x_ref[...], g_ref[...], sh_ref[...], sc_ref[...])
    hb = h.astype(BF16)
    refs = iter(o_refs)
    for c0, c1, kind, dtypes in outs:
        if kind == "h":
            p = h
        else:
            p = _bdot(hb, w_ref[:, c0:c1])
        if kind == "scale":
            p = p * ATTN_SCALE
        elif kind == "sigmoid":
            p = jax.nn.sigmoid(p)
        elif kind == "logsigmoid":
            p = jax.nn.log_sigmoid(p + bias_ref[...])
        for dt in dtypes:
            if dt == FEAT_MAJOR:
                next(refs)[...] = p.T
            else:
                next(refs)[...] = p.astype(dt)


FEAT_MAJOR = "feature-major f32"


def _norm_proj(rows, x, g, mod, chunks, w, bias, outs):
    ncols = w.shape[1]
    kern = functools.partial(_proj_kernel, outs)
    full = lambda shape: pl.BlockSpec(shape, lambda i: (0,) * len(shape))
    marr = rows.mod_arr(mod)
    flat = [(c1 - c0, dt) for c0, c1, _, dts in outs for dt in dts]
    tps = rows.tiles_per_seq

    def spec(wd, dt):
        if dt == FEAT_MAJOR:
            return pl.BlockSpec((None, wd, rows.tm), lambda i: (i // tps, 0, i % tps))
        return rows.row_spec(wd)

    def shape(wd, dt):
        if dt == FEAT_MAJOR:
            return jax.ShapeDtypeStruct((rows.nseq, wd, rows.t), F32)
        return jax.ShapeDtypeStruct((rows.m, wd), dt)

    return pl.pallas_call(
        kern,
        grid=(rows.ntiles,),
        in_specs=[rows.row_spec(D_MODEL), full((1, D_MODEL)),
                  rows.mod_spec(chunks[0]), rows.mod_spec(chunks[1]),
                  full((D_MODEL, ncols)), full((1, LANES))],
        out_specs=[spec(wd, dt) for wd, dt in flat],
        out_shape=[shape(wd, dt) for wd, dt in flat],
        compiler_params=_cparams("parallel"),
        name="norm_proj",
    )(x, g.reshape(1, D_MODEL), marr, marr, w, bias)


def _oproj_kernel(o_ref, w_ref, x_ref, g_ref, gate_ref, y_ref):
    y = _bdot(o_ref[...], w_ref[...])
    y_ref[...] = x_ref[...] + gate_ref[...] * _rms_rows(y, g_ref[...])


def _out_proj_residual(rows, o, w, x, g, mod, gate_chunk):
    k = w.shape[0]
    full = lambda shape: pl.BlockSpec(shape, lambda i: (0,) * len(shape))
    return pl.pallas_call(
        _oproj_kernel,
        grid=(rows.ntiles,),
        in_specs=[rows.row_spec(k), full((k, D_MODEL)), rows.row_spec(D_MODEL),
                  full((1, D_MODEL)), rows.mod_spec(gate_chunk)],
        out_specs=rows.row_spec(D_MODEL),
        out_shape=jax.ShapeDtypeStruct((rows.m, D_MODEL), F32),
        compiler_params=_cparams("parallel"),
        name="out_proj_residual",
    )(o, w, x, g.reshape(1, D_MODEL), rows.mod_arr(mod))


FFN_TF = 256
FFN_NJ = D_FF // FFN_TF
HALO = SUBLANES


def _ffn_kernel(decode, tiles_per_seq, x_ref, g2_ref, sh_ref, sc_ref, wg_ref, wu_ref, cw_ref, cb_ref,
                wd_ref, g3_ref, gate_ref, p0_ref, p1_ref, y_ref, st_ref, h_scr, act_scr, gs_scr, carry_scr):
    i, j = pl.program_id(0), pl.program_id(1)
    tm = x_ref.shape[0]
    tf = wg_ref.shape[1]

    @pl.when(j == 0)
    def _():
        h_scr[...] = _norm_h(x_ref[...], g2_ref[...], sh_ref[...], sc_ref[...]).astype(BF16)

    h = h_scr[...]
    gcol = _bdot(h, wg_ref[...])
    ucol = _bdot(h, wu_ref[...])
    cw = cw_ref[...]
    if decode:
        a = cb_ref[...] + cw[0:1] * p0_ref[...] + cw[1:2] * p1_ref[...] + cw[2:3] * gcol
        st_ref[...] = gcol
    else:
        first = (i % tiles_per_seq) == 0

        @pl.when(first)
        def _():
            gs_scr[0:HALO, :] = jnp.zeros((HALO, gs_scr.shape[1]), F32)

        @pl.when(jnp.logical_not(first))
        def _():
            gs_scr[0:HALO, :] = carry_scr[j]

        gs_scr[HALO:HALO + tm, :] = gcol
        a = (cb_ref[...] + cw[0:1] * gs_scr[HALO - 2:HALO - 2 + tm, :]
             + cw[1:2] * gs_scr[HALO - 1:HALO - 1 + tm, :] + cw[2:3] * gcol)
        tail = gcol[tm - HALO:tm, :]
        carry_scr[j] = tail
        st_ref[...] = tail
    act_scr[:, pl.ds(pl.multiple_of(j * tf, tf), tf)] = (a * jax.nn.sigmoid(a) * ucol).astype(BF16)

    @pl.when(j == pl.num_programs(1) - 1)
    def _():
        y = _bdot(act_scr[...], wd_ref[...])
        y_ref[...] = x_ref[...] + gate_ref[...] * _rms_rows(y, g3_ref[...])


def _conv_ffn(rows, x, g2, g3, mod, w_gu, conv_w, conv_b, w_d, prev):
    tm, tf = rows.tm, FFN_TF
    decode = rows.decode
    tps = rows.tiles_per_seq
    kern = functools.partial(_ffn_kernel, decode, tps)
    full = lambda shape: pl.BlockSpec(shape, lambda i, j: (0,) * len(shape))
    col = lambda r: pl.BlockSpec((r, tf), lambda i, j: (0, j))
    if decode:
        p0, p1 = prev[:, 0], prev[:, 1]
        pspec = pl.BlockSpec((tm, tf), lambda i, j: (0, j))
        st_spec = pl.BlockSpec((tm, tf), lambda i, j: (0, j))
        st_shape = jax.ShapeDtypeStruct((rows.m, D_FF), F32)
    else:
        p0 = p1 = jnp.zeros((SUBLANES, LANES), F32)
        pspec = full((SUBLANES, LANES))
        st_spec = pl.BlockSpec((None, HALO, tf), lambda i, j: (i, 0, j))
        st_shape = jax.ShapeDtypeStruct((rows.ntiles, HALO, D_FF), F32)
    marr = rows.mod_arr(mod)
    y, st = pl.pallas_call(
        kern,
        grid=(rows.ntiles, FFN_NJ),
        in_specs=[rows.row_spec(D_MODEL), full((1, D_MODEL)), rows.mod_spec(3), rows.mod_spec(4),
                  pl.BlockSpec((D_MODEL, tf), lambda i, j: (0, j)),
                  pl.BlockSpec((D_MODEL, tf), lambda i, j: (0, FFN_NJ + j)),
                  col(CONV_W), col(1),
                  full((D_FF, D_MODEL)),
                  full((1, D_MODEL)), rows.mod_spec(5), pspec, pspec],
        out_specs=[rows.row_spec(D_MODEL), st_spec],
        out_shape=[jax.ShapeDtypeStruct((rows.m, D_MODEL), F32), st_shape],
        scratch_shapes=[pltpu.VMEM((tm, D_MODEL), BF16), pltpu.VMEM((tm, D_FF), BF16),
                        pltpu.VMEM((HALO + tm, tf), F32), pltpu.VMEM((FFN_NJ, HALO, tf), F32)],
        compiler_params=_cparams("arbitrary", "arbitrary"),
        name="conv_ffn",
    )(x, g2.reshape(1, D_MODEL), marr, marr, w_gu, w_gu, conv_w, conv_b.reshape(1, D_FF),
      w_d, g3.reshape(1, D_MODEL), marr, p0, p1)
    if decode:
        return y, jnp.stack([prev[:, 1], st], axis=1)
    last = st.reshape(rows.nseq, tps, HALO, D_FF)[:, -1]
    return y, last[:, HALO - (CONV_W - 1):]


def _mm_kernel(a_ref, w_ref, o_ref):
    o_ref[...] = _bdot(a_ref[...], w_ref[...])


def _matmul(a, w, tm):
    m, k = a.shape
    n = w.shape[1]
    return pl.pallas_call(
        _mm_kernel,
        grid=(m // tm,),
        in_specs=[pl.BlockSpec((tm, k), lambda i: (i, 0)), pl.BlockSpec((k, n), lambda i: (0, 0))],
        out_specs=pl.BlockSpec((tm, n), lambda i: (i, 0)),
        out_shape=jax.ShapeDtypeStruct((m, n), F32),
        compiler_params=_cparams("parallel"),
        name="matmul",
    )(a, w)


Q_TILE = 128
KEY_CHUNK = 128
GQ = NSA_GROUP * Q_TILE
SEL_UNROLL = 2
M_INIT = -3e38


def _alibi_slopes():
    h = np.arange(1, N_HEADS + 1, dtype=np.float32)
    return np.exp2(-8.0 * h / N_HEADS).astype(np.float32).reshape(NSA_KV_HEADS, NSA_GROUP)


def _half_masks():
    lane = lax.broadcasted_iota(jnp.int32, (Q_TILE, LANES), 1)
    return lane < HEAD_DIM


def _stack_group_heads(q, lo, par=None):
    tiles = []
    for g in range(NSA_GROUP):
        t = q[:, (g // 2) * LANES:(g // 2 + 1) * LANES]
        keep = lo if g % 2 == 0 else jnp.logical_not(lo)
        t = jnp.where(keep, t, jnp.zeros_like(t))
        if par is not None:
            rolled = pltpu.roll(t.astype(F32), HEAD_DIM, axis=1).astype(BF16)
            t = jnp.where(par == g % 2, t, rolled)
        tiles.append(t)
    return jnp.concatenate(tiles, axis=0)


def _nsa_cmp_kernel(nb, q_ref, kcc_ref, vcc_ref, slope_ref, oc_ref, selb_ref, any_ref):
    j = pl.program_id(2)
    q0 = j * Q_TILE
    lo = _half_masks()
    qs = _stack_group_heads(q_ref[...], lo)
    st = _dot_nt(kcc_ref[...], qs)
    n_i = lax.broadcasted_iota(jnp.int32, (nb, GQ), 0)
    qpos = q0 + (lax.broadcasted_iota(jnp.int32, (nb, GQ), 1) & (Q_TILE - 1))
    ends = n_i * NSA_BLOCK + (NSA_BLOCK - 1)
    mask = ends <= qpos
    s = st - slope_ref[...] * (qpos - ends).astype(F32)
    s = jnp.where(mask, s, NEG)
    e = jnp.exp(s - jnp.max(s, axis=0, keepdims=True))
    p = jnp.where(mask, e / jnp.sum(e, axis=0, keepdims=True), 0.0)
    acc = _bdot(p.T.astype(BF16), vcc_ref[...])
    for t in range(NSA_GROUP // 2):
        a0 = acc[(2 * t) * Q_TILE:(2 * t + 1) * Q_TILE]
        a1 = acc[(2 * t + 1) * Q_TILE:(2 * t + 2) * Q_TILE]
        oc_ref[:, t * LANES:(t + 1) * LANES] = jnp.where(lo, a0, a1)

    imp = p[:, 0:Q_TILE]
    for g in range(1, NSA_GROUP):
        imp = imp + p[:, g * Q_TILE:(g + 1) * Q_TILE]
    n2 = lax.broadcasted_iota(jnp.int32, (nb, Q_TILE), 0)
    cur = (q0 + lax.broadcasted_iota(jnp.int32, (nb, Q_TILE), 1)) // NSA_BLOCK
    forced = (n2 == 0) | (n2 == cur) | (n2 == cur - 1)
    score = jnp.where(forced, BIG, jnp.where(n2 <= cur, imp, NEG))

    def pick_next(_, carry):
        sc, sel = carry
        m = jnp.max(sc, axis=0, keepdims=True)
        first = jnp.min(jnp.where(sc == m, n2, nb), axis=0, keepdims=True)
        pick = n2 == first
        sel = jnp.where(pick & (m > 0.5 * NEG), 1.0, sel)
        return jnp.where(pick, -jnp.inf, sc), sel

    _, sel = lax.fori_loop(0, min(NSA_TOPN, nb), pick_next, (score, jnp.zeros((nb, Q_TILE), F32)), unroll=True)
    sel_t = sel.T
    if nb < LANES:
        sel_t = jnp.concatenate([sel_t, jnp.zeros((Q_TILE, LANES - nb), F32)], axis=1)
    selb_ref[...] = jnp.where(sel_t > 0.0, 0.0, NEG).astype(BF16)
    any_ref[...] = jnp.max(sel_t, axis=0, keepdims=True)


def _nsa_compressed(b, t, q, kcc, vcc):
    nb = t // NSA_BLOCK
    nq = t // Q_TILE
    slopes = jnp.asarray(np.repeat(_alibi_slopes(), Q_TILE, axis=1).reshape(NSA_KV_HEADS, 1, GQ))
    return pl.pallas_call(
        functools.partial(_nsa_cmp_kernel, nb),
        grid=(b, NSA_KV_HEADS, nq),
        in_specs=[pl.BlockSpec((Q_TILE, KVW), lambda bi, k, j: (bi * nq + j, k)),
                  pl.BlockSpec((None, None, nb, LANES), lambda bi, k, j: (bi, k, 0, 0)),
                  pl.BlockSpec((None, None, nb, LANES), lambda bi, k, j: (bi, k, 0, 0)),
                  pl.BlockSpec((None, 1, GQ), lambda bi, k, j: (k, 0, 0))],
        out_specs=[pl.BlockSpec((Q_TILE, KVW), lambda bi, k, j: (bi * nq + j, k)),
                   pl.BlockSpec((None, None, Q_TILE, LANES), lambda bi, k, j: (bi, k, j, 0)),
                   pl.BlockSpec((None, None, None, 1, LANES), lambda bi, k, j: (bi, k, j, 0, 0))],
        out_shape=[jax.ShapeDtypeStruct((b * t, D_MODEL), F32),
                   jax.ShapeDtypeStruct((b, NSA_KV_HEADS, t, LANES), BF16),
                   jax.ShapeDtypeStruct((b, NSA_KV_HEADS, nq, 1, LANES), F32)],
        compiler_params=_cparams("parallel", "parallel", "parallel"),
        name="nsa_compressed",
    )(q, kcc, vcc, slopes)


def _softmax_step(s, v, m_ref, l_ref, acc_ref):
    m_prev = m_ref[...]
    m_new = jnp.maximum(m_prev, jnp.max(s, axis=-1, keepdims=True))
    alpha = jnp.exp(m_prev - m_new)
    p = jnp.exp(s - m_new)
    l_ref[...] = alpha * l_ref[...] + jnp.sum(p, axis=-1, keepdims=True)
    acc_ref[...] = alpha * acc_ref[...] + _bdot(p.astype(BF16), v)
    m_ref[...] = m_new


def _nsa_sw_kernel(nq, flags_ref, q_ref, selb_ref, ks_ref, vst_ref, kw_ref, vwt_ref, posx_ref, onehot_ref, qx_ref,
                   oc_ref, gate_ref, slope_ref, o_ref, ksa_scr, kwa_scr, qaug_scr, ms_scr, as_scr, mw_scr, aw_scr,
                   sta_scr, stb_scr, list_scr):
    k, j = pl.program_id(1), pl.program_id(2)
    par = k % 2
    lane_b = lax.broadcasted_iota(jnp.int32, (KEY_CHUNK, LANES), 1)

    @pl.when(j == 0)
    def _():
        def fill(c, carry):
            r0 = pl.multiple_of(c * KEY_CHUNK, KEY_CHUNK)
            own = (lane_b // HEAD_DIM) == par
            px = posx_ref[...]
            ksa_scr[pl.ds(r0, KEY_CHUNK), 0:LANES] = onehot_ref[pl.ds(r0, KEY_CHUNK), :]
            ksa_scr[pl.ds(r0, KEY_CHUNK), LANES:2 * LANES] = jnp.where(own, ks_ref[pl.ds(r0, KEY_CHUNK), :], px)
            kwa_scr[pl.ds(r0, KEY_CHUNK), :] = jnp.where(own, kw_ref[pl.ds(r0, KEY_CHUNK), :], px)
            return carry

        lax.fori_loop(0, nq, fill, 0)

    lo = _half_masks()
    qq = _stack_group_heads(q_ref[...], lo, par)
    own_q = (lax.broadcasted_iota(jnp.int32, (GQ, LANES), 1) // HEAD_DIM) == par
    qaug_scr[:, 0:LANES] = jnp.concatenate([selb_ref[...]] * NSA_GROUP, axis=0)
    qaug_scr[:, LANES:2 * LANES] = jnp.where(own_q, qq, qx_ref[...])
    slope = slope_ref[...]
    for m_ref, a_ref in ((ms_scr, as_scr), (mw_scr, aw_scr)):
        m_ref[...] = jnp.full(m_ref.shape, M_INIT, F32)
        a_ref[...] = jnp.zeros(a_ref.shape, F32)

    key_l = lax.broadcasted_iota(jnp.int32, (KEY_CHUNK, GQ), 0)
    q_l = lax.broadcasted_iota(jnp.int32, (KEY_CHUNK, GQ), 1) & (Q_TILE - 1)
    own_v = (lax.broadcasted_iota(jnp.int32, (LANES, KEY_CHUNK), 0) // HEAD_DIM) == par

    def step(pieces, m_ref, a_ref):
        shifts = [jnp.where(live, slope * ((c - j) * KEY_CHUNK).astype(F32), -jnp.inf) for _, c, live, _ in pieces]
        m_prev = m_ref[...]
        m_new = m_prev
        for (st, _, _, _), shift in zip(pieces, shifts):
            m_new = jnp.maximum(m_new, jnp.max(st, axis=0, keepdims=True) + shift)
        p = jnp.concatenate([jnp.exp(st - (m_new - shift)).astype(BF16)
                             for (st, _, _, _), shift in zip(pieces, shifts)], axis=0)
        vo = jnp.concatenate([jnp.where(own_v, vt, 1.0).astype(BF16) for _, _, _, vt in pieces], axis=1)
        a_ref[...] = jnp.exp(m_prev - m_new) * a_ref[...] + _bdot(vo, p)
        m_ref[...] = m_new

    def sel_piece(c, live, diag=False):
        k0 = pl.multiple_of(c * KEY_CHUNK, KEY_CHUNK)
        st = _dot_nt(ksa_scr[pl.ds(k0, KEY_CHUNK), :], qaug_scr[...])
        if diag:
            st = jnp.where(key_l <= q_l, st, NEG)
        return st, c, live, vst_ref[:, pl.ds(k0, KEY_CHUNK)]

    word0 = ((pl.program_id(0) * NSA_KV_HEADS + k) * nq + j) * _flag_words(nq)

    def scan(c, n):
        act = (flags_ref[word0 + c // 32] >> (c % 32)) & 1

        @pl.when(act == 1)
        def _():
            list_scr[n] = c

        return n + act

    n_act = lax.fori_loop(0, j, scan, 0)

    n_groups = (n_act + SEL_UNROLL - 1) // SEL_UNROLL

    def group_chunks(g):
        out = []
        for u in range(SEL_UNROLL):
            idx = g * SEL_UNROLL + u
            c = list_scr[jnp.minimum(idx, n_act - 1)]
            out.append((c, idx < n_act, pl.multiple_of(c * KEY_CHUNK, KEY_CHUNK)))
        return out

    def sel_scores(g, st_ref):
        for u, (_, _, k0) in enumerate(group_chunks(g)):
            st_ref[u] = _dot_nt(ksa_scr[pl.ds(k0, KEY_CHUNK), :], qaug_scr[...])

    def sel_update(g, st_ref):
        step([(st_ref[u], c, live, vst_ref[:, pl.ds(k0, KEY_CHUNK)])
              for u, (c, live, k0) in enumerate(group_chunks(g))], ms_scr, as_scr)

    @pl.when(n_groups > 0)
    def _():
        sel_scores(0, sta_scr)

    def sel_body(i, carry):
        sel_scores(2 * i + 1, stb_scr)
        sel_update(2 * i, sta_scr)
        sel_scores(2 * i + 2, sta_scr)
        sel_update(2 * i + 1, stb_scr)
        return carry

    lax.fori_loop(0, (n_groups + 1) // 2, sel_body, 0)
    step([sel_piece(j, True, diag=True)], ms_scr, as_scr)

    n_win = NSA_WINDOW // KEY_CHUNK
    pieces = []
    for dc in range(n_win + 1):
        c = j - n_win + dc
        cc = jnp.maximum(c, 0)
        k0 = pl.multiple_of(cc * KEY_CHUNK, KEY_CHUNK)
        st = _dot_nt(kwa_scr[pl.ds(k0, KEY_CHUNK), :], qaug_scr[:, LANES:2 * LANES])
        if dc == 0:
            st = jnp.where(key_l > q_l, st, NEG)
        elif dc == n_win:
            st = jnp.where(key_l <= q_l, st, NEG)
        pieces.append((st, cc, c >= 0, vwt_ref[:, pl.ds(k0, KEY_CHUNK)]))
    step(pieces, mw_scr, aw_scr)

    own0 = pl.multiple_of(par * HEAD_DIM, HEAD_DIM)
    oth0 = pl.multiple_of((1 - par) * HEAD_DIM, HEAD_DIM)
    o_s = as_scr[pl.ds(own0, HEAD_DIM), :] / as_scr[pl.ds(oth0, 1), :]
    o_w = aw_scr[pl.ds(own0, HEAD_DIM), :] / aw_scr[pl.ds(oth0, 1), :]
    gates_t = gate_ref[...].T
    oc_t = oc_ref[...].T
    mix = []
    for g in range(NSA_GROUP):
        cols = slice(g * Q_TILE, (g + 1) * Q_TILE)
        mix.append(gates_t[3 * g:3 * g + 1] * oc_t[g * HEAD_DIM:(g + 1) * HEAD_DIM]
                   + gates_t[3 * g + 1:3 * g + 2] * o_s[:, cols] + gates_t[3 * g + 2:3 * g + 3] * o_w[:, cols])
    o_ref[...] = jnp.concatenate(mix, axis=0).T.astype(o_ref.dtype)


def _flag_words(nchunks):
    return -(-nchunks // 32)


def _bf16_pieces(x):
    def rnd(v):
        return np.asarray(v, np.float32).astype(jnp.bfloat16).astype(np.float32)
    a = rnd(x)
    b = rnd(x - a)
    return a, b, rnd(x - a - b)


def _nsa_fold_constants(t):
    posx = np.zeros((KEY_CHUNK, LANES), np.float32)
    loc = np.arange(KEY_CHUNK, dtype=np.float32)
    slopes = _alibi_slopes()
    qx = np.zeros((NSA_KV_HEADS, GQ, LANES), np.float32)
    ql = np.tile(np.arange(Q_TILE, dtype=np.float32), NSA_GROUP)
    for k in range(NSA_KV_HEADS):
        srow = np.repeat(slopes[k], Q_TILE)
        pieces = _bf16_pieces(srow) + _bf16_pieces(-srow * ql)
        for base in (0, HEAD_DIM):
            for i, pc in enumerate(pieces):
                qx[k, :, base + i] = pc
    for base in (0, HEAD_DIM):
        posx[:, base:base + 3] = loc[:, None]
        posx[:, base + 3:base + 6] = 1.0
    onehot = (np.arange(t)[:, None] // NSA_BLOCK == np.arange(LANES)[None, :]).astype(np.float32)
    return jnp.asarray(posx, BF16), jnp.asarray(onehot, BF16), jnp.asarray(qx, BF16)


def _pack_chunk_flags(any_sel, nchunks):
    blocks_per_chunk = KEY_CHUNK // NSA_BLOCK
    f = any_sel[:, :, :, 0, :nchunks * blocks_per_chunk]
    f = f.reshape(f.shape[:3] + (nchunks, blocks_per_chunk)).max(axis=-1) > 0.0
    words = _flag_words(nchunks)
    f = jnp.pad(f, ((0, 0), (0, 0), (0, 0), (0, words * 32 - nchunks)))
    bits = f.reshape(f.shape[:3] + (words, 32)).astype(jnp.uint32) << jnp.arange(32, dtype=jnp.uint32)
    return lax.bitcast_convert_type(bits.sum(axis=-1, dtype=jnp.uint32), jnp.int32).reshape(-1)


def _nsa_selected_window(b, t, q, selb, any_sel, kvs, kvs_t, kvw, kvw_t, oc, gates):
    assert Q_TILE == KEY_CHUNK
    nq = t // Q_TILE
    slopes = jnp.asarray(np.repeat(_alibi_slopes(), Q_TILE, axis=1).reshape(NSA_KV_HEADS, 1, GQ))
    posx, onehot, qx = _nsa_fold_constants(t)
    flags = _pack_chunk_flags(any_sel, nq)
    pairs = KVW // LANES
    qspec = pl.BlockSpec((Q_TILE, KVW), lambda bi, k, j, fl: (bi * nq + j, k))
    kspec = pl.BlockSpec((None, t, LANES), lambda bi, k, j, fl: (bi, 0, k // 2))
    vspec = pl.BlockSpec((None, LANES, t), lambda bi, k, j, fl: (bi, pairs + k // 2, 0))
    const = lambda shape: pl.BlockSpec(shape, lambda bi, k, j, fl: (0,) * len(shape))
    return pl.pallas_call(
        functools.partial(_nsa_sw_kernel, nq),
        grid_spec=pltpu.PrefetchScalarGridSpec(
            num_scalar_prefetch=1,
            grid=(b, NSA_KV_HEADS, nq),
            in_specs=[qspec,
                      pl.BlockSpec((None, None, Q_TILE, LANES), lambda bi, k, j, fl: (bi, k, j, 0)),
                      kspec, vspec, kspec, vspec,
                      const((KEY_CHUNK, LANES)), const((t, LANES)),
                      pl.BlockSpec((None, GQ, LANES), lambda bi, k, j, fl: (k, 0, 0)),
                      qspec,
                      pl.BlockSpec((Q_TILE, LANES), lambda bi, k, j, fl: (bi * nq + j, k)),
                      pl.BlockSpec((None, 1, GQ), lambda bi, k, j, fl: (k, 0, 0))],
            out_specs=qspec,
            scratch_shapes=[pltpu.VMEM((t, 2 * LANES), BF16), pltpu.VMEM((t, LANES), BF16),
                            pltpu.VMEM((GQ, 2 * LANES), BF16),
                            pltpu.VMEM((1, GQ), F32), pltpu.VMEM((LANES, GQ), F32),
                            pltpu.VMEM((1, GQ), F32), pltpu.VMEM((LANES, GQ), F32),
                            pltpu.VMEM((SEL_UNROLL, KEY_CHUNK, GQ), F32), pltpu.VMEM((SEL_UNROLL, KEY_CHUNK, GQ), F32),
                            pltpu.SMEM((nq,), jnp.int32)]),
        out_shape=jax.ShapeDtypeStruct((b * t, D_MODEL), BF16),
        compiler_params=_cparams("parallel", "arbitrary", "arbitrary"),
        name="nsa_selected_window",
    )(flags, q, selb, kvs, kvs_t, kvw, kvw_t, posx, onehot, qx, oc, gates, slopes)


def _nsa_weights(w_in):
    w = w_in.astype(BF16)
    ng = NSA_GROUP * 3
    gcols = [jnp.pad(w[:, NSA_G0 + k * ng:NSA_G0 + (k + 1) * ng], ((0, 0), (0, LANES - ng)))
             for k in range(NSA_KV_HEADS)]
    return jnp.concatenate([w[:, :NSA_G0]] + gcols, axis=1)


def _nsa_proj_outs(kv_dtypes):
    return ((NSA_Q0, NSA_C0, "scale", (BF16,)),
            (NSA_C0, NSA_S0, "", kv_dtypes), (NSA_S0, NSA_W0, "", kv_dtypes), (NSA_W0, NSA_G0, "", kv_dtypes),
            (NSA_G0, NSA_COLS, "sigmoid", (F32,)))


NO_BIAS = np.zeros((1, LANES), np.float32)


def _leaf_from_feat_major(a, feat_shape):
    return jnp.moveaxis(a.reshape(a.shape[:1] + tuple(feat_shape) + a.shape[2:]), -1, 1)


def _compress_blocks(kvc, nseq, nb, w_ck, w_cv):
    blocks = kvc.reshape(nseq, nb, NSA_BLOCK, 2, NSA_KV_HEADS, HEAD_DIM).transpose(3, 0, 1, 4, 2, 5)
    blocks = blocks.reshape(2, nseq * nb * NSA_KV_HEADS, NSA_BLOCK * HEAD_DIM)
    tm = min(256, blocks.shape[1])
    out = []
    for a, w in ((blocks[0], w_ck), (blocks[1], w_cv)):
        c = _matmul(a, w.astype(BF16), tm).reshape(nseq, nb, NSA_KV_HEADS, HEAD_DIM).transpose(0, 2, 1, 3)
        out.append(jnp.concatenate([c, c], axis=-1).astype(BF16))
    return out


def _nsa_prompt(rows, x, g, mod, w_in, w_ck, w_cv):
    b, t = rows.nseq, rows.t
    q, kvc_t, kvc_b, kvs_t, kvs_b, kvw_t, kvw_b, gates = _norm_proj(
        rows, x, g, mod, (0, 1), _nsa_weights(w_in), jnp.asarray(NO_BIAS), _nsa_proj_outs((FEAT_MAJOR, BF16)))
    kcc, vcc = _compress_blocks(kvc_b, b, t // NSA_BLOCK, w_ck, w_cv)
    oc, selb, any_sel = _nsa_compressed(b, t, q, kcc, vcc)
    o = _nsa_selected_window(b, t, q, selb, any_sel, kvs_b.reshape(b, t, 2 * KVW), kvs_t,
                             kvw_b.reshape(b, t, 2 * KVW), kvw_t, oc, gates)
    return o, kvc_t, kvs_t, kvw_t


FOX_TQ = 512
FOX_TK = 512
CUM_TILE = 256
FOX_PAIRS = FOX_HEADS // 2

def _fox_proj_outs(kv_dtypes):
    return ((0, D_MODEL, "scale", (BF16,)), (D_MODEL, FOX_LF0, "", kv_dtypes), (FOX_LF0, FOX_COLS, "logsigmoid", (F32,)))


def _fox_weights(w_in, b_f):
    w = jnp.pad(w_in.astype(BF16), ((0, 0), (0, FOX_COLS - w_in.shape[1])))
    bias = jnp.pad(b_f.astype(F32), (0, LANES - FOX_HEADS)).reshape(1, LANES)
    return w, bias


def _tri_cumsum(x):
    n = x.shape[0]
    tri = (lax.broadcasted_iota(jnp.int32, (n, n), 1) <= lax.broadcasted_iota(jnp.int32, (n, n), 0)).astype(BF16)
    a, b, c = _split3(x)
    return _bdot(tri, a) + _bdot(tri, b) + _bdot(tri, c)


def _cumsum_kernel(x_ref, o_ref, carry_scr):
    @pl.when(pl.program_id(1) == 0)
    def _():
        carry_scr[...] = jnp.zeros_like(carry_scr)

    cum = _tri_cumsum(x_ref[...]) + carry_scr[0:1, :]
    o_ref[...] = cum
    carry_scr[...] = jnp.broadcast_to(cum[-1:, :], carry_scr.shape)


def _cumsum_rows(x, nseq, t):
    nt = t // CUM_TILE
    return pl.pallas_call(
        _cumsum_kernel,
        grid=(nseq, nt),
        in_specs=[pl.BlockSpec((CUM_TILE, LANES), lambda s, i: (s * nt + i, 0))],
        out_specs=pl.BlockSpec((CUM_TILE, LANES), lambda s, i: (s * nt + i, 0)),
        out_shape=jax.ShapeDtypeStruct(x.shape, F32),
        scratch_shapes=[pltpu.VMEM((SUBLANES, LANES), F32)],
        compiler_params=_cparams("parallel", "arbitrary"),
        name="cumsum_rows",
    )(x)


def _fox_kernel(nk, q_ref, k_ref, vt_ref, ccol_ref, crow_ref, o_ref, ka_scr, base_scr, qa_scr, m_scr, acc_scr,
                sta_scr, stb_scr):
    qi = pl.program_id(2)
    tq = q_ref.shape[0]
    lane = lax.broadcasted_iota(jnp.int32, (FOX_TK, LANES), 1)

    def own(e):
        return (lane // HEAD_DIM) == e

    def other_lane(e, i):
        return lane == (1 - e) * HEAD_DIM + i

    @pl.when(qi == 0)
    def _():
        def fill(c, carry):
            k0 = pl.multiple_of(c * FOX_TK, FOX_TK)
            kc = k_ref[pl.ds(k0, FOX_TK), :].astype(F32)
            for e in range(2):
                col = ccol_ref[pl.ds(k0, FOX_TK), e:e + 1]
                base = col[0:1, :]
                ext = jnp.zeros((FOX_TK, LANES), F32)
                for i, piece in enumerate(_split3(base - col)):
                    ext = jnp.where(other_lane(e, i), piece.astype(F32), ext)
                ka_scr[e, pl.ds(k0, FOX_TK), :] = jnp.where(own(e), kc, ext).astype(BF16)
                base_scr[e, pl.ds(c, 1), :] = jnp.broadcast_to(base, (1, LANES))
            return carry

        lax.fori_loop(0, nk, fill, 0)

    q = q_ref[...].astype(F32)
    ones3 = jnp.zeros((tq, LANES), F32)
    for e in range(2):
        ext = ones3
        for i in range(3):
            ext = jnp.where(other_lane(e, i), 1.0, ext)
        qa_scr[e] = jnp.where(own(e), q, ext).astype(BF16)
    m_scr[...] = jnp.full(m_scr.shape, M_INIT, F32)
    acc_scr[...] = jnp.zeros(acc_scr.shape, F32)
    q0 = pl.multiple_of(qi * tq, tq)
    row = lax.broadcasted_iota(jnp.int32, (LANES, FOX_TK), 0)

    def scores(c, st_ref):
        k0 = pl.multiple_of(c * FOX_TK, FOX_TK)
        for e in range(2):
            st_ref[e] = _dot_nt(ka_scr[e, pl.ds(k0, FOX_TK), :], qa_scr[e])

    def update(c, live, st_ref, diag=False):
        k0 = pl.multiple_of(c * FOX_TK, FOX_TK)
        vt = vt_ref[:, pl.ds(k0, FOX_TK)]
        for e in range(2):
            st = st_ref[e]
            if diag:
                st = jnp.where(lax.broadcasted_iota(jnp.int32, st.shape, 0)
                               <= lax.broadcasted_iota(jnp.int32, st.shape, 1), st, NEG)
            base = base_scr[e, pl.ds(c, 1), :]
            shift = crow_ref[e:e + 1, pl.ds(q0, tq)] - jnp.concatenate([base] * (tq // LANES), axis=1)
            shift = jnp.where(live, shift, -jnp.inf)
            m_prev = m_scr[e]
            m_new = jnp.maximum(m_prev, jnp.max(st, axis=0, keepdims=True) + shift)
            p = jnp.exp(st - (m_new - shift)).astype(BF16)
            vo = jnp.where((row // HEAD_DIM) == e, vt, 1.0).astype(BF16)
            acc_scr[e] = jnp.exp(m_prev - m_new) * acc_scr[e] + _bdot(vo, p)
            m_scr[e] = m_new

    scores(0, sta_scr)

    def body(i, carry):
        c = 2 * i
        scores(jnp.minimum(c + 1, qi), stb_scr)
        update(c, True, sta_scr)
        scores(jnp.minimum(c + 2, qi), sta_scr)
        update(jnp.minimum(c + 1, qi), c + 1 < qi, stb_scr)
        return carry

    lax.fori_loop(0, (qi + 1) // 2, body, 0)
    update(qi, True, sta_scr, diag=True)
    a0, a1 = acc_scr[0], acc_scr[1]
    r = lax.broadcasted_iota(jnp.int32, a0.shape, 0)
    o_t = jnp.where(r < HEAD_DIM, a0 / a0[HEAD_DIM:HEAD_DIM + 1], a1 / a1[0:1])
    o_ref[...] = o_t.T.astype(o_ref.dtype)


def _fox_attention(b, t, q, k, kv_t, cum_col, cum_row):
    assert FOX_TQ == FOX_TK and t % FOX_TQ == 0
    tq = FOX_TQ
    nq = t // tq
    vrow0 = D_MODEL // LANES
    return pl.pallas_call(
        functools.partial(_fox_kernel, nq),
        grid=(b, FOX_PAIRS, nq),
        in_specs=[pl.BlockSpec((tq, LANES), lambda bi, hp, qi: (bi * nq + qi, hp)),
                  pl.BlockSpec((None, t, LANES), lambda bi, hp, qi: (bi, 0, hp)),
                  pl.BlockSpec((None, LANES, t), lambda bi, hp, qi: (bi, vrow0 + hp, 0)),
                  pl.BlockSpec((None, None, t, 2), lambda bi, hp, qi: (bi, hp, 0, 0)),
                  pl.BlockSpec((None, None, 2, t), lambda bi, hp, qi: (bi, hp, 0, 0))],
        out_specs=pl.BlockSpec((tq, LANES), lambda bi, hp, qi: (bi * nq + qi, hp)),
        out_shape=jax.ShapeDtypeStruct((b * t, D_MODEL), BF16),
        scratch_shapes=[pltpu.VMEM((2, t, LANES), BF16), pltpu.VMEM((2, max(nq, SUBLANES), LANES), F32),
                        pltpu.VMEM((2, tq, LANES), BF16), pltpu.VMEM((2, 1, tq), F32),
                        pltpu.VMEM((2, LANES, tq), F32),
                        pltpu.VMEM((2, FOX_TK, tq), F32), pltpu.VMEM((2, FOX_TK, tq), F32)],
        compiler_params=_cparams("parallel", "arbitrary", "arbitrary"),
        name="fox_attention",
    )(q, k, kv_t, cum_col, cum_row)


def _fox_prompt(rows, x, g, mod, w_in, b_f):
    b, t = rows.nseq, rows.t
    w, bias = _fox_weights(w_in, b_f)
    q, kv_t, kv_b, logf = _norm_proj(rows, x, g, mod, (0, 1), w, bias, _fox_proj_outs((FEAT_MAJOR, BF16)))
    cum = _cumsum_rows(logf, b, t)[:, :FOX_HEADS].reshape(b, t, FOX_PAIRS, 2)
    o = _fox_attention(b, t, q, kv_b.reshape(b, t, 2 * D_MODEL), kv_t,
                       cum.transpose(0, 2, 1, 3), cum.transpose(0, 2, 3, 1))
    return o, kv_t, logf[:, :FOX_HEADS]


def _pool_kernel(tps, pos0, h_ref, w_ref, b_ref, scale_ref, o_ref, ext_scr):
    i = pl.program_id(0)
    tm = h_ref.shape[0]
    halo = 2 * SUBLANES
    first = (i % tps) == 0

    @pl.when(first)
    def _():
        ext_scr[0:halo, :] = jnp.zeros((halo, D_MODEL), F32)

    @pl.when(jnp.logical_not(first))
    def _():
        ext_scr[0:halo, :] = ext_scr[tm:tm + halo, :]

    ext_scr[halo:halo + tm, :] = h_ref[...]
    pos = pos0 + (i % tps) * tm + lax.broadcasted_iota(jnp.int32, (tm, 1), 0)
    for gi, w in enumerate(POOL_WINDOWS):
        c0, c1 = gi * POOL_GROUP_DIM, (gi + 1) * POOL_GROUP_DIM
        win = ext_scr[halo:halo + tm, c0:c1]
        for back in range(1, w):
            win = win + ext_scr[halo - back:halo - back + tm, c0:c1]
        cnt = jnp.minimum(w, pos + 1).astype(F32)
        mixed = win / cnt - h_ref[:, c0:c1]
        y = _bdot(mixed.astype(BF16), w_ref[gi]) + b_ref[:, c0:c1]
        o_ref[:, c0:c1] = y * scale_ref[:, c0:c1]


def _pool_mix(h_ext, nseq, t, tm, pos0, w_g, b_g, scale):
    assert POOL_STATE < 2 * SUBLANES
    tps = t // tm
    full = lambda shape: pl.BlockSpec(shape, lambda i: (0,) * len(shape))
    return pl.pallas_call(
        functools.partial(_pool_kernel, tps, pos0),
        grid=(nseq * tps,),
        in_specs=[pl.BlockSpec((tm, D_MODEL), lambda i: (i, 0)),
                  full((len(POOL_WINDOWS), POOL_GROUP_DIM, POOL_GROUP_DIM)), full((1, D_MODEL)), full((1, D_MODEL))],
        out_specs=pl.BlockSpec((tm, D_MODEL), lambda i: (i, 0)),
        out_shape=jax.ShapeDtypeStruct((nseq * t, D_MODEL), F32),
        scratch_shapes=[pltpu.VMEM((tm + 2 * SUBLANES, D_MODEL), F32)],
        compiler_params=_cparams("arbitrary"),
        name="pool_mix",
    )(h_ext, w_g.astype(BF16), b_g.reshape(1, D_MODEL), scale.reshape(1, D_MODEL))


def _residual_kernel(o_ref, x_ref, g_ref, gate_ref, y_ref):
    y_ref[...] = x_ref[...] + gate_ref[...] * _rms_rows(o_ref[...], g_ref[...])


def _gated_residual(rows, o, x, g, mod, gate_chunk):
    full = lambda shape: pl.BlockSpec(shape, lambda i: (0,) * len(shape))
    return pl.pallas_call(
        _residual_kernel,
        grid=(rows.ntiles,),
        in_specs=[rows.row_spec(D_MODEL), rows.row_spec(D_MODEL), full((1, D_MODEL)), rows.mod_spec(gate_chunk)],
        out_specs=rows.row_spec(D_MODEL),
        out_shape=jax.ShapeDtypeStruct((rows.m, D_MODEL), F32),
        compiler_params=_cparams("parallel"),
        name="gated_residual",
    )(o, x, g.reshape(1, D_MODEL), rows.mod_arr(mod))


H_ONLY = ((0, D_MODEL, "h", (F32,)),)


def _norm_only(rows, x, g, mod, chunks):
    dummy = jnp.zeros((D_MODEL, LANES), BF16)
    return _norm_proj(rows, x, g, mod, chunks, dummy, jnp.asarray(NO_BIAS), H_ONLY)[0]


DEC_B = 8
NEW_ROWS = SUBLANES


def _paged_grid_spec(grid, in_specs, out_specs, scratch_shapes):
    return pltpu.PrefetchScalarGridSpec(num_scalar_prefetch=1, grid=grid, in_specs=in_specs,
                                        out_specs=out_specs, scratch_shapes=scratch_shapes)


def _feat_major_pages(cache):
    n, p = cache.shape[:2]
    return jnp.moveaxis(cache, 1, -1).reshape(n, -1, p)


def _page_specs(npg, rows):
    return [pl.BlockSpec((None, rows, PAGE_SIZE), functools.partial(lambda b, pt, i: (pt[b * npg + i], 0, 0), i=i))
            for i in range(npg)]


def _cmp_dec_kernel(npg, pt_ref, *refs):
    pages, (wk_ref, wv_ref, o_ref, lhs_scr) = refs[:npg], refs[npg:]
    groups = 2 * NSA_KV_HEADS
    for d in range(HEAD_DIM):
        for i, page in enumerate(pages):
            lhs_scr[i * groups:(i + 1) * groups, d * PAGE_SIZE:(d + 1) * PAGE_SIZE] = page[pl.ds(d, groups, stride=HEAD_DIM), :]
    lhs = lhs_scr[...].astype(BF16)
    is_key = (lax.broadcasted_iota(jnp.int32, o_ref.shape, 0) % groups) < NSA_KV_HEADS
    o_ref[...] = jnp.where(is_key, _bdot(lhs, wk_ref[...]), _bdot(lhs, wv_ref[...])).astype(o_ref.dtype)


def _cmp_dec_weight(w):
    w3 = w.astype(BF16).reshape(NSA_BLOCK, HEAD_DIM, HEAD_DIM)
    halves = PAGE_SIZE // NSA_BLOCK
    eye = jnp.eye(halves, dtype=BF16)
    return jnp.einsum("rde,hg->dhrge", w3, eye).reshape(HEAD_DIM * PAGE_SIZE, halves * HEAD_DIM)


def _cmp_decode(page_table, cache_t, w_ck, w_cv):
    ns, npg = page_table.shape
    halves = PAGE_SIZE // NSA_BLOCK
    groups = 2 * NSA_KV_HEADS
    wspec = pl.BlockSpec((HEAD_DIM * PAGE_SIZE, halves * HEAD_DIM), lambda b, pt: (0, 0))
    out = pl.pallas_call(
        functools.partial(_cmp_dec_kernel, npg),
        grid_spec=_paged_grid_spec(
            (ns,), _page_specs(npg, 2 * KVW) + [wspec, wspec],
            pl.BlockSpec((None, npg * groups, halves * HEAD_DIM), lambda b, pt: (b, 0, 0)),
            [pltpu.VMEM((npg * groups, HEAD_DIM * PAGE_SIZE), F32)]),
        out_shape=jax.ShapeDtypeStruct((ns, npg * groups, halves * HEAD_DIM), BF16),
        compiler_params=_cparams("parallel"),
        name="nsa_decode_compress",
    )(page_table.reshape(-1), *([cache_t] * npg), _cmp_dec_weight(w_ck), _cmp_dec_weight(w_cv))
    out = out.reshape(ns, npg, 2, NSA_KV_HEADS, halves, HEAD_DIM).transpose(0, 1, 4, 2, 3, 5)
    return out.reshape(ns, npg * halves, 2 * KVW)


def _dec_select_kernel(qpos, nblk, qbd_ref, kcv_ref, slope_ref, oc_ref, selb_ref, imp_scr):
    lane_b = lax.broadcasted_iota(jnp.int32, (N_HEADS, nblk), 1)
    ends = lane_b * NSA_BLOCK + (NSA_BLOCK - 1)
    mask = ends <= qpos
    bias = slope_ref[...] * (qpos - ends).astype(F32)
    imp_scr[...] = jnp.zeros(imp_scr.shape, F32)
    for bi in range(DEC_B):
        s = _dot_nt(qbd_ref[bi], kcv_ref[bi, :, 0:KVW]) - bias
        s = jnp.where(mask, s, NEG)
        e = jnp.exp(s - jnp.max(s, axis=-1, keepdims=True))
        p = jnp.where(mask, e / jnp.sum(e, axis=-1, keepdims=True), 0.0)
        oc_ref[bi] = _bdot(p.astype(BF16), kcv_ref[bi, :, KVW:2 * KVW])
        imp = p[0:NSA_KV_HEADS]
        for g in range(1, NSA_GROUP):
            imp = imp + p[g * NSA_KV_HEADS:(g + 1) * NSA_KV_HEADS]
        imp_scr[bi * NSA_KV_HEADS:(bi + 1) * NSA_KV_HEADS, 0:nblk] = imp

    rows = DEC_B * NSA_KV_HEADS
    n2 = lax.broadcasted_iota(jnp.int32, (rows, LANES), 1)
    cur = qpos // NSA_BLOCK
    forced = (n2 == 0) | (n2 == cur) | (n2 == cur - 1)
    score = jnp.where(forced, BIG, jnp.where(n2 <= cur, imp_scr[...], NEG))
    score = jnp.where(n2 <= cur, score, -jnp.inf)

    def pick_next(_, carry):
        sc, sel = carry
        m = jnp.max(sc, axis=-1, keepdims=True)
        first = jnp.min(jnp.where(sc == m, n2, LANES), axis=-1, keepdims=True)
        pick = n2 == first
        sel = jnp.where(pick & (m > 0.5 * NEG), 1.0, sel)
        return jnp.where(pick, -jnp.inf, sc), sel

    _, sel = lax.fori_loop(0, min(NSA_TOPN, cur + 1), pick_next, (score, jnp.zeros((rows, LANES), F32)), unroll=True)
    selb = jnp.where(sel > 0.0, 0.0, NEG).astype(BF16)
    for bi in range(DEC_B):
        one = selb[bi * NSA_KV_HEADS:(bi + 1) * NSA_KV_HEADS]
        selb_ref[bi] = jnp.concatenate([one] * NSA_GROUP, axis=0)


def _dec_slopes():
    return jnp.asarray(_alibi_slopes().T.reshape(N_HEADS, 1))


def _nsa_decode_select(qbd, kcv, qpos):
    ns, nblk = kcv.shape[0], kcv.shape[1]
    assert qpos // NSA_BLOCK < LANES and ns % DEC_B == 0
    blk = lambda shape: pl.BlockSpec((DEC_B,) + shape, lambda i: (i, 0, 0))
    return pl.pallas_call(
        functools.partial(_dec_select_kernel, qpos, nblk),
        grid=(ns // DEC_B,),
        in_specs=[blk((N_HEADS, KVW)), blk((nblk, 2 * KVW)), pl.BlockSpec((N_HEADS, 1), lambda i: (0, 0))],
        out_specs=[blk((N_HEADS, KVW)), blk((N_HEADS, LANES))],
        out_shape=[jax.ShapeDtypeStruct((ns, N_HEADS, KVW), F32), jax.ShapeDtypeStruct((ns, N_HEADS, LANES), BF16)],
        scratch_shapes=[pltpu.VMEM((DEC_B * NSA_KV_HEADS, LANES), F32)],
        compiler_params=_cparams("parallel"),
        name="nsa_decode_select",
    )(qbd, kcv, _dec_slopes())


def _new_key_tile(row):
    r = lax.broadcasted_iota(jnp.int32, (NEW_ROWS, row.shape[1]), 0)
    return jnp.where(r == 0, jnp.broadcast_to(row, (NEW_ROWS, row.shape[1])), 0.0).astype(BF16)


def _new_key_mask(s):
    return jnp.where(lax.broadcasted_iota(jnp.int32, s.shape, 1) == 0, s, NEG)


def _softmax_with_new_key(s, s_new):
    m = jnp.maximum(jnp.max(s, axis=-1, keepdims=True), jnp.max(s_new, axis=-1, keepdims=True))
    p, p_new = jnp.exp(s - m), jnp.exp(s_new - m)
    return p, p_new, jnp.sum(p, axis=-1, keepdims=True) + jnp.sum(p_new, axis=-1, keepdims=True)


def _dec_attend_kernel(npg, qpos, wlen, pt_ref, *refs):
    pages = refs[:npg]
    (qbd_ref, selb_ref, win_ref, snew_ref, wnew_ref, wcol_ref, oc_ref, gate_ref, slope_ref,
     o_ref, wout_ref) = refs[npg:]
    slope = slope_ref[...]
    qbd = qbd_ref[...]

    qaug = jnp.concatenate([qbd, selb_ref[...]], axis=1)
    blk_row = lax.broadcasted_iota(jnp.int32, (LANES, PAGE_SIZE), 0)
    blk_of_lane = lax.broadcasted_iota(jnp.int32, (LANES, PAGE_SIZE), 1) // NSA_BLOCK
    scores = []
    for i, page in enumerate(pages):
        onehot_t = (blk_row == (PAGE_SIZE // NSA_BLOCK) * i + blk_of_lane).astype(BF16)
        kaug_t = jnp.concatenate([page[0:KVW, :].astype(BF16), onehot_t], axis=0)
        scores.append(_bdot(qaug, kaug_t))
    kpos = lax.broadcasted_iota(jnp.int32, (N_HEADS, npg * PAGE_SIZE), 1)
    s = jnp.concatenate(scores, axis=1) - slope * (qpos - kpos).astype(F32)
    snew = snew_ref[...]
    s_new = _new_key_mask(_dot_nt(qbd, _new_key_tile(snew[:, 0:KVW])))
    p, p_new, l = _softmax_with_new_key(s, s_new)
    p = p.astype(BF16)
    acc = _bdot(p_new.astype(BF16), _new_key_tile(snew[:, KVW:2 * KVW]))
    for i, page in enumerate(pages):
        acc = acc + _dot_nt(p[:, i * PAGE_SIZE:(i + 1) * PAGE_SIZE], page[KVW:2 * KVW, :].astype(BF16))
    o_s = acc / l

    win = win_ref[...]
    wpos = qpos - wlen + lax.broadcasted_iota(jnp.int32, (N_HEADS, wlen), 1)
    s = _bdot(qbd, win[0:KVW, :].astype(BF16)) - slope * (qpos - wpos).astype(F32)
    s = jnp.where(wpos > qpos - NSA_WINDOW, s, NEG)
    wnew = wnew_ref[...]
    s_new = _new_key_mask(_dot_nt(qbd, _new_key_tile(wnew[:, 0:KVW])))
    p, p_new, l = _softmax_with_new_key(s, s_new)
    o_w = (_dot_nt(p.astype(BF16), win[KVW:2 * KVW, :].astype(BF16))
           + _bdot(p_new.astype(BF16), _new_key_tile(wnew[:, KVW:2 * KVW]))) / l

    gates = gate_ref[...]
    o_ref[...] = gates[:, 0:1] * oc_ref[...] + gates[:, 1:2] * o_s + gates[:, 2:3] * o_w
    lane = lax.broadcasted_iota(jnp.int32, win.shape, 1)
    wout_ref[...] = jnp.where(lane == wlen - 1, wcol_ref[...], pltpu.roll(win, wlen - 1, axis=1))


def _nsa_decode_attend(page_table, qbd, selb, cache_t, win_t, kvs_new, kvw_new, oc, gates, qpos):
    ns, npg = page_table.shape
    wlen = win_t.shape[2]
    assert qpos == npg * PAGE_SIZE
    per_b = lambda shape: pl.BlockSpec((None,) + shape, lambda b, pt: (b, 0, 0))
    return pl.pallas_call(
        functools.partial(_dec_attend_kernel, npg, qpos, wlen),
        grid_spec=_paged_grid_spec(
            (ns,),
            _page_specs(npg, 2 * KVW)
            + [per_b((N_HEADS, KVW)), per_b((N_HEADS, LANES)), per_b((2 * KVW, wlen)), per_b((1, 2 * KVW)),
               per_b((1, 2 * KVW)), per_b((2 * KVW, 1)), per_b((N_HEADS, KVW)), per_b((N_HEADS, LANES)),
               pl.BlockSpec((N_HEADS, 1), lambda b, pt: (0, 0))],
            [per_b((N_HEADS, KVW)), per_b((2 * KVW, wlen))],
            []),
        out_shape=[jax.ShapeDtypeStruct((ns, N_HEADS, KVW), F32), jax.ShapeDtypeStruct(win_t.shape, F32)],
        compiler_params=_cparams("parallel"),
        name="nsa_decode_attend",
    )(page_table.reshape(-1), *([cache_t] * npg), qbd, selb, win_t, kvs_new.reshape(ns, 1, 2 * KVW),
      kvw_new.reshape(ns, 1, 2 * KVW), kvw_new.reshape(ns, 2 * KVW, 1), oc, gates, _dec_slopes())


def _nsa_sample(rows, x, g, mod, w_in, w_ck, w_cv, cache_c, cache_s, win_buf, page_table):
    ns = rows.nseq
    qpos = page_table.shape[1] * PAGE_SIZE
    q, kvc, kvs, kvw, gates = _norm_proj(
        rows, x, g, mod, (0, 1), _nsa_weights(w_in), jnp.asarray(NO_BIAS), _nsa_proj_outs((F32,)))
    kcv = _cmp_decode(page_table, _feat_major_pages(cache_c), w_ck, w_cv)
    q4 = q.reshape(ns, NSA_KV_HEADS, NSA_GROUP, HEAD_DIM)
    eye = jnp.eye(NSA_KV_HEADS, dtype=q.dtype)
    qbd = jnp.einsum("bkgd,kj->bgkjd", q4, eye).reshape(ns, N_HEADS, KVW)
    oc, selb = _nsa_decode_select(qbd, kcv, qpos)
    g4 = gates.reshape(ns, NSA_KV_HEADS, LANES)[:, :, :NSA_GROUP * 3].reshape(ns, NSA_KV_HEADS, NSA_GROUP, 3)
    g_rows = jnp.pad(g4.transpose(0, 2, 1, 3).reshape(ns, N_HEADS, 3), ((0, 0), (0, 0), (0, LANES - 3)))
    o_bd, win_out = _nsa_decode_attend(
        page_table, qbd, selb, _feat_major_pages(cache_s), _feat_major_pages(win_buf), kvs, kvw, oc, g_rows, qpos)
    o5 = o_bd.reshape(ns, NSA_GROUP, NSA_KV_HEADS, NSA_KV_HEADS, HEAD_DIM)
    o = jnp.einsum("bgkkd->bkgd", o5).reshape(ns, D_MODEL).astype(BF16)
    win_out = jnp.moveaxis(win_out.reshape(win_buf.shape[:1] + win_buf.shape[2:] + win_buf.shape[1:2]), -1, 1)
    return o, kvc, kvs, win_out


def _fox_dec_kernel(npg, pt_ref, *refs):
    kv_pages, lf_pages = refs[:npg], refs[npg:2 * npg]
    q_ref, kvn_ref, lfn_ref, o_ref = refs[2 * npg:]
    own = (lax.broadcasted_iota(jnp.int32, (FOX_HEADS, D_MODEL), 1) // HEAD_DIM
           == lax.broadcasted_iota(jnp.int32, (FOX_HEADS, D_MODEL), 0))
    q = jnp.broadcast_to(q_ref[...].astype(F32), (FOX_HEADS, D_MODEL))
    qbd = jnp.where(own, q, 0.0).astype(BF16)
    upper = (lax.broadcasted_iota(jnp.int32, (PAGE_SIZE, PAGE_SIZE), 0)
             <= lax.broadcasted_iota(jnp.int32, (PAGE_SIZE, PAGE_SIZE), 1)).astype(BF16)
    carry = jnp.zeros((FOX_HEADS, 1), F32)
    cums, scores = [], []
    for kv_page, lf_page in zip(kv_pages, lf_pages):
        a, b, c = _split3(lf_page[...])
        cum = carry + (_bdot(a, upper) + _bdot(b, upper) + _bdot(c, upper))
        carry = cum[:, PAGE_SIZE - 1:PAGE_SIZE]
        cums.append(cum)
        scores.append(_bdot(qbd, kv_page[0:D_MODEL, :].astype(BF16)))
    cum_new = carry + lfn_ref[...]
    s = jnp.concatenate(scores, axis=1) + (cum_new - jnp.concatenate(cums, axis=1))
    kvn = kvn_ref[...]
    s_new = _new_key_mask(_dot_nt(qbd, _new_key_tile(kvn[:, 0:D_MODEL])))
    p, p_new, l = _softmax_with_new_key(s, s_new)
    p = p.astype(BF16)
    acc = _bdot(p_new.astype(BF16), _new_key_tile(kvn[:, D_MODEL:2 * D_MODEL]))
    for i, kv_page in enumerate(kv_pages):
        acc = acc + _dot_nt(p[:, i * PAGE_SIZE:(i + 1) * PAGE_SIZE], kv_page[D_MODEL:2 * D_MODEL, :].astype(BF16))
    o_ref[...] = jnp.sum(jnp.where(own, acc / l, 0.0), axis=0, keepdims=True).astype(o_ref.dtype)


def _fox_decode(page_table, q, cache_kv_t, cache_logf_t, kv_new, logf_new):
    ns, npg = page_table.shape
    per_b = lambda shape: pl.BlockSpec((None,) + shape, lambda b, pt: (b, 0, 0))
    out = pl.pallas_call(
        functools.partial(_fox_dec_kernel, npg),
        grid_spec=_paged_grid_spec(
            (ns,),
            _page_specs(npg, 2 * D_MODEL) + _page_specs(npg, FOX_HEADS)
            + [per_b((1, D_MODEL)), per_b((1, 2 * D_MODEL)), per_b((FOX_HEADS, 1))],
            per_b((1, D_MODEL)), []),
        out_shape=jax.ShapeDtypeStruct((ns, 1, D_MODEL), BF16),
        compiler_params=_cparams("parallel"),
        name="fox_decode",
    )(page_table.reshape(-1), *([cache_kv_t] * npg), *([cache_logf_t] * npg), q.reshape(ns, 1, D_MODEL),
      kv_new.reshape(ns, 1, 2 * D_MODEL), logf_new.reshape(ns, FOX_HEADS, 1))
    return out.reshape(ns, D_MODEL)


def _fox_sample(rows, x, g, mod, w_in, b_f, cache_kv, cache_logf, page_table):
    w, bias = _fox_weights(w_in, b_f)
    q, kv, logf = _norm_proj(rows, x, g, mod, (0, 1), w, bias, _fox_proj_outs((F32,)))
    logf = logf[:, :FOX_HEADS]
    o = _fox_decode(page_table, q, _feat_major_pages(cache_kv), _feat_major_pages(cache_logf), kv, logf)
    return o, kv, logf


PROMPT_TM = 512
POOL_TM = 256


def kernel(x_prompt, x_sample, cache_l0_cmp_kv, cache_l0_sel_kv, state_l0_win_kv, cache_l1_kv, cache_l1_logf, state_l2_pool, cache_l3_cmp_kv, cache_l3_sel_kv, state_l3_win_kv, state_ffn_conv, page_table, c_prompt, c_sample, mod_w, mod_b, norm_g, l0_nsa_w_in, l0_nsa_w_ck, l0_nsa_w_cv, l0_nsa_w_o, l1_fox_w_in, l1_fox_b_f, l1_fox_w_o, l2_pool_w, l2_pool_b, l2_pool_scale, l3_nsa_w_in, l3_nsa_w_ck, l3_nsa_w_cv, l3_nsa_w_o, ffn_w_gu, ffn_conv_w, ffn_conv_b, ffn_w_d):
    b, t, _ = x_prompt.shape
    ns = x_sample.shape[0]
    past_len = page_table.shape[1] * PAGE_SIZE
    rp = _Rows(b, t, min(PROMPT_TM, t))
    rs = _Rows(ns, 1, ns)
    nsa = {0: (cache_l0_cmp_kv, cache_l0_sel_kv, state_l0_win_kv, l0_nsa_w_in, l0_nsa_w_ck, l0_nsa_w_cv, l0_nsa_w_o),
           3: (cache_l3_cmp_kv, cache_l3_sel_kv, state_l3_win_kv, l3_nsa_w_in, l3_nsa_w_ck, l3_nsa_w_cv, l3_nsa_w_o)}

    c_all = jnp.concatenate([c_prompt, c_sample], axis=0)
    c_all = jnp.pad(c_all, ((0, -c_all.shape[0] % SUBLANES), (0, 0)))
    mod = _modulation(c_all, mod_w, mod_b)

    xp = x_prompt.reshape(b * t, D_MODEL)
    xs = x_sample.reshape(ns, D_MODEL)
    st = {}
    conv_p, conv_s = [], []
    kv5 = lambda a, n: a.reshape(n, -1, 2, NSA_KV_HEADS, HEAD_DIM)
    for i in range(DEPTH):
        mp, ms = mod[i, :b], mod[i, b:b + ns]
        g = norm_g[i]
        kind = i % 3
        if kind == 0:
            c_c, c_s, s_w, w_in, w_ck, w_cv, w_o = nsa[i]
            op, kvc_t, kvs_t, kvw_t = _nsa_prompt(rp, xp, g[0], mp, w_in, w_ck, w_cv)
            os_, kvc_s, kvs_s, win_s = _nsa_sample(rs, xs, g[0], ms, w_in, w_ck, w_cv, c_c, c_s, s_w, page_table)
            leaf = lambda a: _leaf_from_feat_major(a, (2, NSA_KV_HEADS, HEAD_DIM))
            st[i] = (leaf(kvc_t), kv5(kvc_s, ns), leaf(kvs_t), kv5(kvs_s, ns),
                     leaf(kvw_t[:, :, -min(NSA_WINDOW, t):]), win_s)
            w_ob = w_o.astype(BF16)
            xp = _out_proj_residual(rp, op, w_ob, xp, g[1], mp, 2)
            xs = _out_proj_residual(rs, os_, w_ob, xs, g[1], ms, 2)
        elif kind == 1:
            op, kv_p, lf_p = _fox_prompt(rp, xp, g[0], mp, l1_fox_w_in, l1_fox_b_f)
            os_, kv_s, lf_s = _fox_sample(rs, xs, g[0], ms, l1_fox_w_in, l1_fox_b_f, cache_l1_kv, cache_l1_logf, page_table)
            st[i] = (_leaf_from_feat_major(kv_p, (2, FOX_HEADS, HEAD_DIM)), kv_s.reshape(ns, 1, 2, FOX_HEADS, HEAD_DIM),
                     lf_p.reshape(b, t, FOX_HEADS), lf_s.reshape(ns, 1, FOX_HEADS))
            w_ob = l1_fox_w_o.astype(BF16)
            xp = _out_proj_residual(rp, op, w_ob, xp, g[1], mp, 2)
            xs = _out_proj_residual(rs, os_, w_ob, xs, g[1], ms, 2)
        else:
            hp = _norm_only(rp, xp, g[0], mp, (0, 1))
            hs = _norm_only(rs, xs, g[0], ms, (0, 1))
            yp = _pool_mix(hp, b, t, min(POOL_TM, t), 0, l2_pool_w, l2_pool_b.reshape(-1), l2_pool_scale)
            ext = jnp.concatenate([state_l2_pool, hs[:, None, :]], axis=1)
            n_ext = POOL_STATE + 1
            ys = _pool_mix(ext.reshape(ns * n_ext, D_MODEL), ns, n_ext, n_ext, past_len - POOL_STATE,
                           l2_pool_w, l2_pool_b.reshape(-1), l2_pool_scale).reshape(ns, n_ext, D_MODEL)[:, -1]
            st[i] = (hp.reshape(b, t, D_MODEL)[:, -POOL_STATE:], ext[:, -POOL_STATE:])
            xp = _gated_residual(rp, yp, xp, g[1], mp, 2)
            xs = _gated_residual(rs, ys, xs, g[1], ms, 2)
        w_gu, w_d = ffn_w_gu[i].astype(BF16), ffn_w_d[i].astype(BF16)
        xp, cp = _conv_ffn(rp, xp, g[2], g[3], mp, w_gu, ffn_conv_w[i], ffn_conv_b[i], w_d, None)
        xs, cs = _conv_ffn(rs, xs, g[2], g[3], ms, w_gu, ffn_conv_w[i], ffn_conv_b[i], w_d, state_ffn_conv[i])
        conv_p.append(cp)
        conv_s.append(cs)
    return (xp.reshape(b, t, D_MODEL), xs.reshape(ns, 1, D_MODEL),
            *st[0], *st[1], *st[2], *st[3],
            jnp.stack(conv_p), jnp.stack(conv_s))
```

```python
import functools

import jax
import jax.numpy as jnp
import numpy as np
from jax import lax
from jax.experimental import pallas as pl
from jax.experimental.pallas import tpu as pltpu

D_MODEL = 1024
DEPTH = 4
PAGE_SIZE = 128
HEAD_DIM = 64
N_HEADS = D_MODEL // HEAD_DIM
NSA_KV_HEADS = 4
NSA_GROUP = N_HEADS // NSA_KV_HEADS
NSA_BLOCK = 64
NSA_TOPN = 16
NSA_WINDOW = 512
FOX_HEADS = D_MODEL // HEAD_DIM
POOL_WINDOWS = (2, 4, 8, 16)
POOL_GROUP_DIM = D_MODEL // len(POOL_WINDOWS)
POOL_STATE = max(POOL_WINDOWS) - 1
D_FF = 2816
CONV_W = 3
N_MOD = 6
RMS_EPS = 1e-6
NEG = -1e30
BIG = 1e30
ATTN_SCALE = HEAD_DIM ** -0.5
LOG2E = float(np.log2(np.e))

LANES = 128
SUBLANES = 8
VMEM_LIMIT = 56 * 1024 * 1024

KVW = NSA_KV_HEADS * HEAD_DIM
NSA_Q0, NSA_C0, NSA_S0, NSA_W0, NSA_G0 = 0, D_MODEL, D_MODEL + 2 * KVW, D_MODEL + 4 * KVW, D_MODEL + 6 * KVW
NSA_COLS = NSA_G0 + NSA_KV_HEADS * LANES
FOX_LF0 = 3 * D_MODEL
FOX_COLS = FOX_LF0 + LANES

F32 = jnp.float32
BF16 = jnp.bfloat16


def _cparams(*sem):
    return pltpu.CompilerParams(dimension_semantics=sem, vmem_limit_bytes=VMEM_LIMIT)


def _bdot(a, b):
    return jnp.dot(a, b, preferred_element_type=F32)


def _dot_nt(a, b):
    return lax.dot_general(a, b, (((1,), (1,)), ((), ())), preferred_element_type=F32)


def _rms_rows(x, g):
    return x * lax.rsqrt(jnp.mean(x * x, axis=-1, keepdims=True) + RMS_EPS) * g


def _split3(x):
    a = x.astype(BF16)
    r = x - a.astype(F32)
    b = r.astype(BF16)
    c = (r - b.astype(F32)).astype(BF16)
    return a, b, c


def _mod_kernel(c_ref, w_ref, b_ref, o_ref):
    c = c_ref[...]
    a = (c * jax.nn.sigmoid(c)).astype(BF16)
    o_ref[...] = _bdot(a, w_ref[...].astype(BF16)) + b_ref[...]


def _modulation(c_all, mod_w, mod_b):
    rows = c_all.shape[0]
    n = N_MOD * D_MODEL
    tn = D_MODEL
    return pl.pallas_call(
        _mod_kernel,
        grid=(DEPTH, n // tn),
        in_specs=[pl.BlockSpec((rows, D_MODEL), lambda i, j: (0, 0)),
                  pl.BlockSpec((None, D_MODEL, tn), lambda i, j: (i, 0, j)),
                  pl.BlockSpec((None, 1, tn), lambda i, j: (i, 0, j))],
        out_specs=pl.BlockSpec((None, rows, tn), lambda i, j: (i, 0, j)),
        out_shape=jax.ShapeDtypeStruct((DEPTH, rows, n), F32),
        compiler_params=_cparams("parallel", "parallel"),
        name="modulation",
    )(c_all, mod_w, mod_b.reshape(DEPTH, 1, n))


def _norm_h(x, g, shift, scale):
    return _rms_rows(x, g) * (1.0 + scale) + shift


class _Rows:
    def __init__(self, nseq, t, tm):
        assert t % tm == 0 or t == 1
        self.nseq, self.t = nseq, t
        self.decode = t == 1
        self.tm = nseq if self.decode else tm
        self.m = nseq * t
        self.tiles_per_seq = 1 if self.decode else t // tm
        self.ntiles = self.m // self.tm

    def mod_spec(self, chunk):
        if self.decode:
            return pl.BlockSpec((self.tm, D_MODEL), lambda i, *_: (0, chunk))
        tps = self.tiles_per_seq
        return pl.BlockSpec((None, 1, D_MODEL), lambda i, *_: (i // tps, 0, chunk))

    def mod_arr(self, mod):
        return mod if self.decode else mod.reshape(self.nseq, 1, N_MOD * D_MODEL)

    def row_spec(self, width, col=0):
        return pl.BlockSpec((self.tm, width), lambda i, *_: (i, col))


def _proj_kernel(outs, x_ref, g_ref, sh_ref, sc_ref, w_ref, bias_ref, *o_refs):
    h = _norm_h(x_ref[...], g_ref[...], sh_ref[...], sc_ref[...])
    hb = h.astype(BF16)
    refs = iter(o_refs)
    for c0, c1, kind, dtypes in outs:
        if kind == "h":
            p = h
        else:
            p = _bdot(hb, w_ref[:, c0:c1])
        if kind == "qscales":
            next(refs)[...] = (p * ATTN_SCALE).astype(BF16)
            next(refs)[...] = (p * (ATTN_SCALE * LOG2E)).astype(BF16)
            continue
        if kind == "scale":
            p = p * ATTN_SCALE
        elif kind == "scale2":
            p = p * (ATTN_SCALE * LOG2E)
        elif kind == "sigmoid":
            p = jax.nn.sigmoid(p)
        elif kind == "logsigmoid":
            p = jax.nn.log_sigmoid(p + bias_ref[...])
        for dt in dtypes:
            if dt == FEAT_MAJOR:
                next(refs)[...] = p.T
            else:
                next(refs)[...] = p.astype(dt)


FEAT_MAJOR = "feature-major f32"


def _norm_proj(rows, x, g, mod, chunks, w, bias, outs):
    ncols = w.shape[1]
    kern = functools.partial(_proj_kernel, outs)
    full = lambda shape: pl.BlockSpec(shape, lambda i: (0,) * len(shape))
    marr = rows.mod_arr(mod)
    flat = [(c1 - c0, dt) for c0, c1, _, dts in outs for dt in dts]
    tps = rows.tiles_per_seq

    def spec(wd, dt):
        if dt == FEAT_MAJOR:
            return pl.BlockSpec((None, wd, rows.tm), lambda i: (i // tps, 0, i % tps))
        return rows.row_spec(wd)

    def shape(wd, dt):
        if dt == FEAT_MAJOR:
            return jax.ShapeDtypeStruct((rows.nseq, wd, rows.t), F32)
        return jax.ShapeDtypeStruct((rows.m, wd), dt)

    return pl.pallas_call(
        kern,
        grid=(rows.ntiles,),
        in_specs=[rows.row_spec(D_MODEL), full((1, D_MODEL)),
                  rows.mod_spec(chunks[0]), rows.mod_spec(chunks[1]),
                  full((D_MODEL, ncols)), full((1, LANES))],
        out_specs=[spec(wd, dt) for wd, dt in flat],
        out_shape=[shape(wd, dt) for wd, dt in flat],
        compiler_params=_cparams("parallel"),
        name="norm_proj",
    )(x, g.reshape(1, D_MODEL), marr, marr, w, bias)


def _oproj_kernel(o_ref, w_ref, x_ref, g_ref, gate_ref, y_ref):
    y = _bdot(o_ref[...], w_ref[...])
    y_ref[...] = x_ref[...] + gate_ref[...] * _rms_rows(y, g_ref[...])


def _out_proj_residual(rows, o, w, x, g, mod, gate_chunk):
    k = w.shape[0]
    full = lambda shape: pl.BlockSpec(shape, lambda i: (0,) * len(shape))
    return pl.pallas_call(
        _oproj_kernel,
        grid=(rows.ntiles,),
        in_specs=[rows.row_spec(k), full((k, D_MODEL)), rows.row_spec(D_MODEL),
                  full((1, D_MODEL)), rows.mod_spec(gate_chunk)],
        out_specs=rows.row_spec(D_MODEL),
        out_shape=jax.ShapeDtypeStruct((rows.m, D_MODEL), F32),
        compiler_params=_cparams("parallel"),
        name="out_proj_residual",
    )(o, w, x, g.reshape(1, D_MODEL), rows.mod_arr(mod))


FFN_TF = 256
FFN_NJ = D_FF // FFN_TF
HALO = SUBLANES


def _ffn_kernel(decode, tiles_per_seq, x_ref, g2_ref, sh_ref, sc_ref, wg_ref, wu_ref, cw_ref, cb_ref,
                wd_ref, g3_ref, gate_ref, p0_ref, p1_ref, y_ref, st_ref, h_scr, act_scr, gs_scr, carry_scr):
    i, j = pl.program_id(0), pl.program_id(1)
    tm = x_ref.shape[0]
    tf = wg_ref.shape[1]

    @pl.when(j == 0)
    def _():
        h_scr[...] = _norm_h(x_ref[...], g2_ref[...], sh_ref[...], sc_ref[...]).astype(BF16)

    h = h_scr[...]
    gcol = _bdot(h, wg_ref[...])
    ucol = _bdot(h, wu_ref[...])
    cw = cw_ref[...]
    if decode:
        a = cb_ref[...] + cw[0:1] * p0_ref[...] + cw[1:2] * p1_ref[...] + cw[2:3] * gcol
        st_ref[...] = gcol
    else:
        first = (i % tiles_per_seq) == 0

        @pl.when(first)
        def _():
            gs_scr[0:HALO, :] = jnp.zeros((HALO, gs_scr.shape[1]), F32)

        @pl.when(jnp.logical_not(first))
        def _():
            gs_scr[0:HALO, :] = carry_scr[j]

        gs_scr[HALO:HALO + tm, :] = gcol
        a = (cb_ref[...] + cw[0:1] * gs_scr[HALO - 2:HALO - 2 + tm, :]
             + cw[1:2] * gs_scr[HALO - 1:HALO - 1 + tm, :] + cw[2:3] * gcol)
        tail = gcol[tm - HALO:tm, :]
        carry_scr[j] = tail
        st_ref[...] = tail
    act_scr[:, pl.ds(pl.multiple_of(j * tf, tf), tf)] = (a * jax.nn.sigmoid(a) * ucol).astype(BF16)

    @pl.when(j == pl.num_programs(1) - 1)
    def _():
        y = _bdot(act_scr[...], wd_ref[...])
        y_ref[...] = x_ref[...] + gate_ref[...] * _rms_rows(y, g3_ref[...])


def _conv_ffn(rows, x, g2, g3, mod, w_gu, conv_w, conv_b, w_d, prev):
    tm, tf = rows.tm, FFN_TF
    decode = rows.decode
    tps = rows.tiles_per_seq
    kern = functools.partial(_ffn_kernel, decode, tps)
    full = lambda shape: pl.BlockSpec(shape, lambda i, j: (0,) * len(shape))
    col = lambda r: pl.BlockSpec((r, tf), lambda i, j: (0, j))
    if decode:
        p0, p1 = prev[:, 0], prev[:, 1]
        pspec = pl.BlockSpec((tm, tf), lambda i, j: (0, j))
        st_spec = pl.BlockSpec((tm, tf), lambda i, j: (0, j))
        st_shape = jax.ShapeDtypeStruct((rows.m, D_FF), F32)
    else:
        p0 = p1 = jnp.zeros((SUBLANES, LANES), F32)
        pspec = full((SUBLANES, LANES))
        st_spec = pl.BlockSpec((None, HALO, tf), lambda i, j: (i, 0, j))
        st_shape = jax.ShapeDtypeStruct((rows.ntiles, HALO, D_FF), F32)
    marr = rows.mod_arr(mod)
    y, st = pl.pallas_call(
        kern,
        grid=(rows.ntiles, FFN_NJ),
        in_specs=[rows.row_spec(D_MODEL), full((1, D_MODEL)), rows.mod_spec(3), rows.mod_spec(4),
                  pl.BlockSpec((D_MODEL, tf), lambda i, j: (0, j)),
                  pl.BlockSpec((D_MODEL, tf), lambda i, j: (0, FFN_NJ + j)),
                  col(CONV_W), col(1),
                  full((D_FF, D_MODEL)),
                  full((1, D_MODEL)), rows.mod_spec(5), pspec, pspec],
        out_specs=[rows.row_spec(D_MODEL), st_spec],
        out_shape=[jax.ShapeDtypeStruct((rows.m, D_MODEL), F32), st_shape],
        scratch_shapes=[pltpu.VMEM((tm, D_MODEL), BF16), pltpu.VMEM((tm, D_FF), BF16),
                        pltpu.VMEM((HALO + tm, tf), F32), pltpu.VMEM((FFN_NJ, HALO, tf), F32)],
        compiler_params=_cparams("arbitrary", "arbitrary"),
        name="conv_ffn",
    )(x, g2.reshape(1, D_MODEL), marr, marr, w_gu, w_gu, conv_w, conv_b.reshape(1, D_FF),
      w_d, g3.reshape(1, D_MODEL), marr, p0, p1)
    if decode:
        return y, jnp.stack([prev[:, 1], st], axis=1)
    last = st.reshape(rows.nseq, tps, HALO, D_FF)[:, -1]
    return y, last[:, HALO - (CONV_W - 1):]


def _mm_kernel(a_ref, w_ref, o_ref):
    o_ref[...] = _bdot(a_ref[...], w_ref[...])


def _matmul(a, w, tm):
    m, k = a.shape
    n = w.shape[1]
    return pl.pallas_call(
        _mm_kernel,
        grid=(m // tm,),
        in_specs=[pl.BlockSpec((tm, k), lambda i: (i, 0)), pl.BlockSpec((k, n), lambda i: (0, 0))],
        out_specs=pl.BlockSpec((tm, n), lambda i: (i, 0)),
        out_shape=jax.ShapeDtypeStruct((m, n), F32),
        compiler_params=_cparams("parallel"),
        name="matmul",
    )(a, w)


Q_TILE = 128
KEY_CHUNK = 128
GQ = NSA_GROUP * Q_TILE
SEL_UNROLL = 2
M_INIT = -3e38


def _alibi_slopes():
    h = np.arange(1, N_HEADS + 1, dtype=np.float32)
    return np.exp2(-8.0 * h / N_HEADS).astype(np.float32).reshape(NSA_KV_HEADS, NSA_GROUP)


def _half_masks():
    lane = lax.broadcasted_iota(jnp.int32, (Q_TILE, LANES), 1)
    return lane < HEAD_DIM


def _stack_group_heads(q, lo, par=None):
    tiles = []
    for g in range(NSA_GROUP):
        t = q[:, (g // 2) * LANES:(g // 2 + 1) * LANES]
        keep = lo if g % 2 == 0 else jnp.logical_not(lo)
        t = jnp.where(keep, t, jnp.zeros_like(t))
        if par is not None:
            rolled = pltpu.roll(t.astype(F32), HEAD_DIM, axis=1).astype(BF16)
            t = jnp.where(par == g % 2, t, rolled)
        tiles.append(t)
    return jnp.concatenate(tiles, axis=0)


def _nsa_cmp_kernel(nb, q_ref, kcc_ref, vcc_ref, slope_ref, oc_ref, selb_ref, any_ref):
    j = pl.program_id(2)
    q0 = j * Q_TILE
    lo = _half_masks()
    qs = _stack_group_heads(q_ref[...], lo)
    st = _dot_nt(kcc_ref[...], qs)
    n_i = lax.broadcasted_iota(jnp.int32, (nb, GQ), 0)
    qpos = q0 + (lax.broadcasted_iota(jnp.int32, (nb, GQ), 1) & (Q_TILE - 1))
    ends = n_i * NSA_BLOCK + (NSA_BLOCK - 1)
    mask = ends <= qpos
    s = st - slope_ref[...] * (qpos - ends).astype(F32)
    s = jnp.where(mask, s, NEG)
    e = jnp.exp(s - jnp.max(s, axis=0, keepdims=True))
    p = jnp.where(mask, e / jnp.sum(e, axis=0, keepdims=True), 0.0)
    acc = _bdot(p.T.astype(BF16), vcc_ref[...])
    for t in range(NSA_GROUP // 2):
        a0 = acc[(2 * t) * Q_TILE:(2 * t + 1) * Q_TILE]
        a1 = acc[(2 * t + 1) * Q_TILE:(2 * t + 2) * Q_TILE]
        oc_ref[:, t * LANES:(t + 1) * LANES] = jnp.where(lo, a0, a1)

    imp = p[:, 0:Q_TILE]
    for g in range(1, NSA_GROUP):
        imp = imp + p[:, g * Q_TILE:(g + 1) * Q_TILE]
    n2 = lax.broadcasted_iota(jnp.int32, (nb, Q_TILE), 0)
    cur = (q0 + lax.broadcasted_iota(jnp.int32, (nb, Q_TILE), 1)) // NSA_BLOCK
    forced = (n2 == 0) | (n2 == cur) | (n2 == cur - 1)
    cur_row = cur[0:1]
    budget = min(NSA_TOPN, nb) - (1 + (cur_row >= 1).astype(jnp.int32) + (cur_row >= 2).astype(jnp.int32))
    score = jnp.where(forced, -jnp.inf, jnp.where(n2 <= cur, imp, NEG))

    def pick_next(r, carry):
        sc, sel = carry
        m = jnp.max(sc, axis=0, keepdims=True)
        first = jnp.min(jnp.where(sc == m, n2, nb), axis=0, keepdims=True)
        pick = n2 == first
        sel = jnp.where(pick & (m > 0.5 * NEG) & (r < budget), 1.0, sel)
        return jnp.where(pick, -jnp.inf, sc), sel

    common = max(min(NSA_TOPN, nb) - 3, 0)
    carry = lax.fori_loop(0, common, pick_next, (score, forced.astype(F32)), unroll=True)
    _, sel = lax.cond(j == 0, lambda c: lax.fori_loop(common, common + 2, pick_next, c, unroll=True), lambda c: c, carry)
    sel_t = sel.T
    if nb < LANES:
        sel_t = jnp.concatenate([sel_t, jnp.zeros((Q_TILE, LANES - nb), F32)], axis=1)
    selb_ref[...] = jnp.where(sel_t > 0.0, 0.0, NEG).astype(BF16)
    any_ref[...] = jnp.max(sel_t, axis=0, keepdims=True)


def _nsa_compressed(b, t, q, kcc, vcc):
    nb = t // NSA_BLOCK
    nq = t // Q_TILE
    slopes = jnp.asarray(np.repeat(_alibi_slopes(), Q_TILE, axis=1).reshape(NSA_KV_HEADS, 1, GQ))
    return pl.pallas_call(
        functools.partial(_nsa_cmp_kernel, nb),
        grid=(b, NSA_KV_HEADS, nq),
        in_specs=[pl.BlockSpec((Q_TILE, KVW), lambda bi, k, j: (bi * nq + j, k)),
                  pl.BlockSpec((None, None, nb, LANES), lambda bi, k, j: (bi, k, 0, 0)),
                  pl.BlockSpec((None, None, nb, LANES), lambda bi, k, j: (bi, k, 0, 0)),
                  pl.BlockSpec((None, 1, GQ), lambda bi, k, j: (k, 0, 0))],
        out_specs=[pl.BlockSpec((Q_TILE, KVW), lambda bi, k, j: (bi * nq + j, k)),
                   pl.BlockSpec((None, None, Q_TILE, LANES), lambda bi, k, j: (bi, k, j, 0)),
                   pl.BlockSpec((None, None, None, 1, LANES), lambda bi, k, j: (bi, k, j, 0, 0))],
        out_shape=[jax.ShapeDtypeStruct((b * t, D_MODEL), F32),
                   jax.ShapeDtypeStruct((b, NSA_KV_HEADS, t, LANES), BF16),
                   jax.ShapeDtypeStruct((b, NSA_KV_HEADS, nq, 1, LANES), F32)],
        compiler_params=_cparams("parallel", "parallel", "parallel"),
        name="nsa_compressed",
    )(q, kcc, vcc, slopes)


def _softmax_step(s, v, m_ref, l_ref, acc_ref):
    m_prev = m_ref[...]
    m_new = jnp.maximum(m_prev, jnp.max(s, axis=-1, keepdims=True))
    alpha = jnp.exp(m_prev - m_new)
    p = jnp.exp(s - m_new)
    l_ref[...] = alpha * l_ref[...] + jnp.sum(p, axis=-1, keepdims=True)
    acc_ref[...] = alpha * acc_ref[...] + _bdot(p.astype(BF16), v)
    m_ref[...] = m_new


def _nsa_sw_kernel(nq, flags_ref, q_ref, selb_ref, ks_ref, vst_ref, kw_ref, vwt_ref, posx_ref, onehot_ref, qx_ref,
                   oc_ref, gate_ref, slope_ref, o_ref, ksa_scr, kwa_scr, qaug_scr, ms_scr, as_scr, mw_scr, aw_scr,
                   sta_scr, stb_scr, vsa_scr, vwa_scr, list_scr):
    k, j = pl.program_id(1), pl.program_id(2)
    par = k % 2
    lane_b = lax.broadcasted_iota(jnp.int32, (KEY_CHUNK, LANES), 1)

    @pl.when(j == 0)
    def _():
        def fill(c, carry):
            r0 = pl.multiple_of(c * KEY_CHUNK, KEY_CHUNK)
            own = (lane_b // HEAD_DIM) == par
            px = posx_ref[...]
            ksa_scr[pl.ds(r0, KEY_CHUNK), 0:LANES] = onehot_ref[pl.ds(r0, KEY_CHUNK), :]
            ksa_scr[pl.ds(r0, KEY_CHUNK), LANES:2 * LANES] = jnp.where(own, ks_ref[pl.ds(r0, KEY_CHUNK), :], px)
            kwa_scr[pl.ds(r0, KEY_CHUNK), :] = jnp.where(own, kw_ref[pl.ds(r0, KEY_CHUNK), :], px)
            own_rows = (lax.broadcasted_iota(jnp.int32, (LANES, KEY_CHUNK), 0) // HEAD_DIM) == par
            vsa_scr[:, pl.ds(r0, KEY_CHUNK)] = jnp.where(own_rows, vst_ref[:, pl.ds(r0, KEY_CHUNK)], 1.0).astype(BF16)
            vwa_scr[:, pl.ds(r0, KEY_CHUNK)] = jnp.where(own_rows, vwt_ref[:, pl.ds(r0, KEY_CHUNK)], 1.0).astype(BF16)
            return carry

        lax.fori_loop(0, nq, fill, 0)

    lo = _half_masks()
    qq = _stack_group_heads(q_ref[...], lo, par)
    own_q = (lax.broadcasted_iota(jnp.int32, (GQ, LANES), 1) // HEAD_DIM) == par
    qaug_scr[:, 0:LANES] = jnp.concatenate([selb_ref[...]] * NSA_GROUP, axis=0)
    qaug_scr[:, LANES:2 * LANES] = jnp.where(own_q, qq, qx_ref[...])
    slope = slope_ref[...]
    for m_ref, a_ref in ((ms_scr, as_scr), (mw_scr, aw_scr)):
        m_ref[...] = jnp.full(m_ref.shape, M_INIT, F32)
        a_ref[...] = jnp.zeros(a_ref.shape, F32)

    key_l = lax.broadcasted_iota(jnp.int32, (KEY_CHUNK, GQ), 0)
    q_l = lax.broadcasted_iota(jnp.int32, (KEY_CHUNK, GQ), 1) & (Q_TILE - 1)

    def step(pieces, m_ref, a_ref):
        shifts = [jnp.where(live, slope * ((c - j) * KEY_CHUNK).astype(F32), -jnp.inf) for _, c, live, _ in pieces]
        m_prev = m_ref[...]
        m_new = m_prev
        for (st, _, _, _), shift in zip(pieces, shifts):
            m_new = jnp.maximum(m_new, jnp.max(st, axis=0, keepdims=True) + shift)
        p = jnp.concatenate([jnp.exp2(st - (m_new - shift)).astype(BF16)
                             for (st, _, _, _), shift in zip(pieces, shifts)], axis=0)
        vo = jnp.concatenate([vt for _, _, _, vt in pieces], axis=1)
        a_ref[...] = jnp.exp2(m_prev - m_new) * a_ref[...] + _bdot(vo, p)
        m_ref[...] = m_new

    def sel_piece(c, live, diag=False):
        k0 = pl.multiple_of(c * KEY_CHUNK, KEY_CHUNK)
        st = _dot_nt(ksa_scr[pl.ds(k0, KEY_CHUNK), :], qaug_scr[...])
        if diag:
            st = jnp.where(key_l <= q_l, st, NEG)
        return st, c, live, vsa_scr[:, pl.ds(k0, KEY_CHUNK)]

    word0 = ((pl.program_id(0) * NSA_KV_HEADS + k) * nq + j) * _flag_words(nq)

    def scan(c, n):
        act = (flags_ref[word0 + c // 32] >> (c % 32)) & 1

        @pl.when(act == 1)
        def _():
            list_scr[n] = c

        return n + act

    n_act = lax.fori_loop(0, j, scan, 0)

    n_groups = (n_act + SEL_UNROLL - 1) // SEL_UNROLL

    def group_chunks(g):
        out = []
        for u in range(SEL_UNROLL):
            idx = g * SEL_UNROLL + u
            c = list_scr[jnp.minimum(idx, n_act - 1)]
            out.append((c, idx < n_act, pl.multiple_of(c * KEY_CHUNK, KEY_CHUNK)))
        return out

    def sel_scores(g, st_ref):
        for u, (_, _, k0) in enumerate(group_chunks(g)):
            st_ref[u] = _dot_nt(ksa_scr[pl.ds(k0, KEY_CHUNK), :], qaug_scr[...])

    def sel_update(g, st_ref):
        step([(st_ref[u], c, live, vsa_scr[:, pl.ds(k0, KEY_CHUNK)])
              for u, (c, live, k0) in enumerate(group_chunks(g))], ms_scr, as_scr)

    @pl.when(n_groups > 0)
    def _():
        sel_scores(0, sta_scr)

    def sel_body(i, carry):
        sel_scores(2 * i + 1, stb_scr)
        sel_update(2 * i, sta_scr)
        sel_scores(2 * i + 2, sta_scr)
        sel_update(2 * i + 1, stb_scr)
        return carry

    lax.fori_loop(0, (n_groups + 1) // 2, sel_body, 0)
    step([sel_piece(j, True, diag=True)], ms_scr, as_scr)

    n_win = NSA_WINDOW // KEY_CHUNK
    pieces = []
    for dc in range(n_win + 1):
        c = j - n_win + dc
        cc = jnp.maximum(c, 0)
        k0 = pl.multiple_of(cc * KEY_CHUNK, KEY_CHUNK)
        st = _dot_nt(kwa_scr[pl.ds(k0, KEY_CHUNK), :], qaug_scr[:, LANES:2 * LANES])
        if dc == 0:
            st = jnp.where(key_l > q_l, st, NEG)
        elif dc == n_win:
            st = jnp.where(key_l <= q_l, st, NEG)
        pieces.append((st, cc, c >= 0, vwa_scr[:, pl.ds(k0, KEY_CHUNK)]))
    step(pieces, mw_scr, aw_scr)

    own0 = pl.multiple_of(par * HEAD_DIM, HEAD_DIM)
    oth0 = pl.multiple_of((1 - par) * HEAD_DIM, HEAD_DIM)
    o_s = as_scr[pl.ds(own0, HEAD_DIM), :] / as_scr[pl.ds(oth0, 1), :]
    o_w = aw_scr[pl.ds(own0, HEAD_DIM), :] / aw_scr[pl.ds(oth0, 1), :]
    gates_t = gate_ref[...].T
    oc_t = oc_ref[...].T
    mix = []
    for g in range(NSA_GROUP):
        cols = slice(g * Q_TILE, (g + 1) * Q_TILE)
        mix.append(gates_t[3 * g:3 * g + 1] * oc_t[g * HEAD_DIM:(g + 1) * HEAD_DIM]
                   + gates_t[3 * g + 1:3 * g + 2] * o_s[:, cols] + gates_t[3 * g + 2:3 * g + 3] * o_w[:, cols])
    o_ref[...] = jnp.concatenate(mix, axis=0).T.astype(o_ref.dtype)


def _flag_words(nchunks):
    return -(-nchunks // 32)


def _bf16_pieces(x):
    def rnd(v):
        return np.asarray(v, np.float32).astype(jnp.bfloat16).astype(np.float32)
    a = rnd(x)
    b = rnd(x - a)
    return a, b, rnd(x - a - b)


def _nsa_fold_constants(t):
    posx = np.zeros((KEY_CHUNK, LANES), np.float32)
    loc = np.arange(KEY_CHUNK, dtype=np.float32)
    slopes = _alibi_slopes()
    qx = np.zeros((NSA_KV_HEADS, GQ, LANES), np.float32)
    ql = np.tile(np.arange(Q_TILE, dtype=np.float32), NSA_GROUP)
    for k in range(NSA_KV_HEADS):
        srow = np.repeat(slopes[k], Q_TILE)
        srow = (srow * np.float32(LOG2E)).astype(np.float32)
        pieces = _bf16_pieces(srow) + _bf16_pieces(-srow * ql)
        for base in (0, HEAD_DIM):
            for i, pc in enumerate(pieces):
                qx[k, :, base + i] = pc
    for base in (0, HEAD_DIM):
        posx[:, base:base + 3] = loc[:, None]
        posx[:, base + 3:base + 6] = 1.0
    onehot = (np.arange(t)[:, None] // NSA_BLOCK == np.arange(LANES)[None, :]).astype(np.float32)
    return jnp.asarray(posx, BF16), jnp.asarray(onehot, BF16), jnp.asarray(qx, BF16)


def _pack_chunk_flags(any_sel, nchunks):
    blocks_per_chunk = KEY_CHUNK // NSA_BLOCK
    f = any_sel[:, :, :, 0, :nchunks * blocks_per_chunk]
    f = f.reshape(f.shape[:3] + (nchunks, blocks_per_chunk)).max(axis=-1) > 0.0
    words = _flag_words(nchunks)
    f = jnp.pad(f, ((0, 0), (0, 0), (0, 0), (0, words * 32 - nchunks)))
    bits = f.reshape(f.shape[:3] + (words, 32)).astype(jnp.uint32) << jnp.arange(32, dtype=jnp.uint32)
    return lax.bitcast_convert_type(bits.sum(axis=-1, dtype=jnp.uint32), jnp.int32).reshape(-1)


def _nsa_selected_window(b, t, q, selb, any_sel, kvs, kvs_t, kvw, kvw_t, oc, gates):
    assert Q_TILE == KEY_CHUNK
    nq = t // Q_TILE
    slopes2 = (_alibi_slopes() * np.float32(LOG2E)).astype(np.float32)
    slopes = jnp.asarray(np.repeat(slopes2, Q_TILE, axis=1).reshape(NSA_KV_HEADS, 1, GQ))
    posx, onehot, qx = _nsa_fold_constants(t)
    flags = _pack_chunk_flags(any_sel, nq)
    pairs = KVW // LANES
    qspec = pl.BlockSpec((Q_TILE, KVW), lambda bi, k, j, fl: (bi * nq + j, k))
    kspec = pl.BlockSpec((None, t, LANES), lambda bi, k, j, fl: (bi, 0, k // 2))
    vspec = pl.BlockSpec((None, LANES, t), lambda bi, k, j, fl: (bi, pairs + k // 2, 0))
    const = lambda shape: pl.BlockSpec(shape, lambda bi, k, j, fl: (0,) * len(shape))
    return pl.pallas_call(
        functools.partial(_nsa_sw_kernel, nq),
        grid_spec=pltpu.PrefetchScalarGridSpec(
            num_scalar_prefetch=1,
            grid=(b, NSA_KV_HEADS, nq),
            in_specs=[qspec,
                      pl.BlockSpec((None, None, Q_TILE, LANES), lambda bi, k, j, fl: (bi, k, j, 0)),
                      kspec, vspec, kspec, vspec,
                      const((KEY_CHUNK, LANES)), const((t, LANES)),
                      pl.BlockSpec((None, GQ, LANES), lambda bi, k, j, fl: (k, 0, 0)),
                      qspec,
                      pl.BlockSpec((Q_TILE, LANES), lambda bi, k, j, fl: (bi * nq + j, k)),
                      pl.BlockSpec((None, 1, GQ), lambda bi, k, j, fl: (k, 0, 0))],
            out_specs=qspec,
            scratch_shapes=[pltpu.VMEM((t, 2 * LANES), BF16), pltpu.VMEM((t, LANES), BF16),
                            pltpu.VMEM((GQ, 2 * LANES), BF16),
                            pltpu.VMEM((1, GQ), F32), pltpu.VMEM((LANES, GQ), F32),
                            pltpu.VMEM((1, GQ), F32), pltpu.VMEM((LANES, GQ), F32),
                            pltpu.VMEM((SEL_UNROLL, KEY_CHUNK, GQ), F32), pltpu.VMEM((SEL_UNROLL, KEY_CHUNK, GQ), F32),
                            pltpu.VMEM((LANES, t), BF16), pltpu.VMEM((LANES, t), BF16),
                            pltpu.SMEM((nq,), jnp.int32)]),
        out_shape=jax.ShapeDtypeStruct((b * t, D_MODEL), BF16),
        compiler_params=_cparams("parallel", "arbitrary", "arbitrary"),
        name="nsa_selected_window",
    )(flags, q, selb, kvs, kvs_t, kvw, kvw_t, posx, onehot, qx, oc, gates, slopes)


def _nsa_weights(w_in):
    w = w_in.astype(BF16)
    ng = NSA_GROUP * 3
    gcols = [jnp.pad(w[:, NSA_G0 + k * ng:NSA_G0 + (k + 1) * ng], ((0, 0), (0, LANES - ng)))
             for k in range(NSA_KV_HEADS)]
    return jnp.concatenate([w[:, :NSA_G0]] + gcols, axis=1)


def _nsa_proj_outs(kv_dtypes, exp2_queries=False):
    return ((NSA_Q0, NSA_C0, "qscales", (BF16, BF16)) if exp2_queries else (NSA_Q0, NSA_C0, "scale", (BF16,)),
            (NSA_C0, NSA_S0, "", kv_dtypes), (NSA_S0, NSA_W0, "", kv_dtypes), (NSA_W0, NSA_G0, "", kv_dtypes),
            (NSA_G0, NSA_COLS, "sigmoid", (F32,)))


NO_BIAS = np.zeros((1, LANES), np.float32)


def _leaf_from_feat_major(a, feat_shape):
    return jnp.moveaxis(a.reshape(a.shape[:1] + tuple(feat_shape) + a.shape[2:]), -1, 1)


def _compress_blocks(kvc, nseq, nb, w_ck, w_cv):
    blocks = kvc.reshape(nseq, nb, NSA_BLOCK, 2, NSA_KV_HEADS, HEAD_DIM).transpose(3, 0, 1, 4, 2, 5)
    blocks = blocks.reshape(2, nseq * nb * NSA_KV_HEADS, NSA_BLOCK * HEAD_DIM)
    tm = min(256, blocks.shape[1])
    out = []
    for a, w in ((blocks[0], w_ck), (blocks[1], w_cv)):
        c = _matmul(a, w.astype(BF16), tm).reshape(nseq, nb, NSA_KV_HEADS, HEAD_DIM).transpose(0, 2, 1, 3)
        out.append(jnp.concatenate([c, c], axis=-1).astype(BF16))
    return out


def _nsa_prompt(rows, x, g, mod, w_in, w_ck, w_cv):
    b, t = rows.nseq, rows.t
    q, q2, kvc_t, kvc_b, kvs_t, kvs_b, kvw_t, kvw_b, gates = _norm_proj(
        rows, x, g, mod, (0, 1), _nsa_weights(w_in), jnp.asarray(NO_BIAS), _nsa_proj_outs((FEAT_MAJOR, BF16), True))
    kcc, vcc = _compress_blocks(kvc_b, b, t // NSA_BLOCK, w_ck, w_cv)
    oc, selb, any_sel = _nsa_compressed(b, t, q, kcc, vcc)
    o = _nsa_selected_window(b, t, q2, selb, any_sel, kvs_b.reshape(b, t, 2 * KVW), kvs_t,
                             kvw_b.reshape(b, t, 2 * KVW), kvw_t, oc, gates)
    return o, kvc_t, kvs_t, kvw_t


FOX_TQ = 512
FOX_TK = 512
CUM_TILE = 256
FOX_PAIRS = FOX_HEADS // 2

def _fox_proj_outs(kv_dtypes, q_kind):
    return ((0, D_MODEL, q_kind, (BF16,)), (D_MODEL, FOX_LF0, "", kv_dtypes), (FOX_LF0, FOX_COLS, "logsigmoid", (F32,)))


def _fox_weights(w_in, b_f):
    w = jnp.pad(w_in.astype(BF16), ((0, 0), (0, FOX_COLS - w_in.shape[1])))
    bias = jnp.pad(b_f.astype(F32), (0, LANES - FOX_HEADS)).reshape(1, LANES)
    return w, bias


def _tri_cumsum(x):
    n = x.shape[0]
    tri = (lax.broadcasted_iota(jnp.int32, (n, n), 1) <= lax.broadcasted_iota(jnp.int32, (n, n), 0)).astype(BF16)
    a, b, c = _split3(x)
    return _bdot(tri, a) + _bdot(tri, b) + _bdot(tri, c)


def _cumsum_kernel(x_ref, o_ref, carry_scr):
    @pl.when(pl.program_id(1) == 0)
    def _():
        carry_scr[...] = jnp.zeros_like(carry_scr)

    cum = _tri_cumsum(x_ref[...]) + carry_scr[0:1, :]
    o_ref[...] = cum
    carry_scr[...] = jnp.broadcast_to(cum[-1:, :], carry_scr.shape)


def _cumsum_rows(x, nseq, t):
    nt = t // CUM_TILE
    return pl.pallas_call(
        _cumsum_kernel,
        grid=(nseq, nt),
        in_specs=[pl.BlockSpec((CUM_TILE, LANES), lambda s, i: (s * nt + i, 0))],
        out_specs=pl.BlockSpec((CUM_TILE, LANES), lambda s, i: (s * nt + i, 0)),
        out_shape=jax.ShapeDtypeStruct(x.shape, F32),
        scratch_shapes=[pltpu.VMEM((SUBLANES, LANES), F32)],
        compiler_params=_cparams("parallel", "arbitrary"),
        name="cumsum_rows",
    )(x)


def _fox_kernel(nk, q_ref, k_ref, vt_ref, ccol_ref, crow_ref, o_ref, ka_scr, base_scr, qa_scr, m_scr, acc_scr,
                sta_scr, stb_scr, va_scr):
    qi = pl.program_id(2)
    tq = q_ref.shape[0]
    lane = lax.broadcasted_iota(jnp.int32, (FOX_TK, LANES), 1)
    row = lax.broadcasted_iota(jnp.int32, (LANES, FOX_TK), 0)

    def own(e):
        return (lane // HEAD_DIM) == e

    def other_lane(e, i):
        return lane == (1 - e) * HEAD_DIM + i

    @pl.when(qi == 0)
    def _():
        def fill(c, carry):
            k0 = pl.multiple_of(c * FOX_TK, FOX_TK)
            kc = k_ref[pl.ds(k0, FOX_TK), :].astype(F32)
            for e in range(2):
                col = ccol_ref[pl.ds(k0, FOX_TK), e:e + 1]
                base = col[0:1, :]
                ext = jnp.zeros((FOX_TK, LANES), F32)
                for i, piece in enumerate(_split3((base - col) * LOG2E)):
                    ext = jnp.where(other_lane(e, i), piece.astype(F32), ext)
                ka_scr[e, pl.ds(k0, FOX_TK), :] = jnp.where(own(e), kc, ext).astype(BF16)
                base_scr[e, pl.ds(c, 1), :] = jnp.broadcast_to(base, (1, LANES))
                va_scr[e, :, pl.ds(k0, FOX_TK)] = jnp.where((row // HEAD_DIM) == e, vt_ref[:, pl.ds(k0, FOX_TK)],
                                                            1.0).astype(BF16)
            return carry

        lax.fori_loop(0, nk, fill, 0)

    q = q_ref[...].astype(F32)
    ones3 = jnp.zeros((tq, LANES), F32)
    for e in range(2):
        ext = ones3
        for i in range(3):
            ext = jnp.where(other_lane(e, i), 1.0, ext)
        qa_scr[e] = jnp.where(own(e), q, ext).astype(BF16)
    m_scr[...] = jnp.full(m_scr.shape, M_INIT, F32)
    acc_scr[...] = jnp.zeros(acc_scr.shape, F32)
    q0 = pl.multiple_of(qi * tq, tq)

    def scores(c, st_ref):
        k0 = pl.multiple_of(c * FOX_TK, FOX_TK)
        for e in range(2):
            st_ref[e] = _dot_nt(ka_scr[e, pl.ds(k0, FOX_TK), :], qa_scr[e])

    def update(c, live, st_ref, diag=False):
        k0 = pl.multiple_of(c * FOX_TK, FOX_TK)
        for e in range(2):
            st = st_ref[e]
            if diag:
                st = jnp.where(lax.broadcasted_iota(jnp.int32, st.shape, 0)
                               <= lax.broadcasted_iota(jnp.int32, st.shape, 1), st, NEG)
            base = base_scr[e, pl.ds(c, 1), :]
            shift = (crow_ref[e:e + 1, pl.ds(q0, tq)] - jnp.concatenate([base] * (tq // LANES), axis=1)) * LOG2E
            shift = jnp.where(live, shift, -jnp.inf)
            m_prev = m_scr[e]
            m_new = jnp.maximum(m_prev, jnp.max(st, axis=0, keepdims=True) + shift)
            p = jnp.exp2(st - (m_new - shift)).astype(BF16)
            acc_scr[e] = jnp.exp2(m_prev - m_new) * acc_scr[e] + _bdot(va_scr[e, :, pl.ds(k0, FOX_TK)], p)
            m_scr[e] = m_new

    scores(0, sta_scr)

    def body(i, carry):
        c = 2 * i
        scores(jnp.minimum(c + 1, qi), stb_scr)
        update(c, True, sta_scr)
        scores(jnp.minimum(c + 2, qi), sta_scr)
        update(jnp.minimum(c + 1, qi), c + 1 < qi, stb_scr)
        return carry

    lax.fori_loop(0, (qi + 1) // 2, body, 0)
    update(qi, True, sta_scr, diag=True)
    a0, a1 = acc_scr[0], acc_scr[1]
    r = lax.broadcasted_iota(jnp.int32, a0.shape, 0)
    o_t = jnp.where(r < HEAD_DIM, a0 / a0[HEAD_DIM:HEAD_DIM + 1], a1 / a1[0:1])
    o_ref[...] = o_t.T.astype(o_ref.dtype)


def _fox_attention(b, t, q, k, kv_t, cum_col, cum_row):
    assert FOX_TQ == FOX_TK and t % FOX_TQ == 0
    tq = FOX_TQ
    nq = t // tq
    vrow0 = D_MODEL // LANES
    return pl.pallas_call(
        functools.partial(_fox_kernel, nq),
        grid=(b, FOX_PAIRS, nq),
        in_specs=[pl.BlockSpec((tq, LANES), lambda bi, hp, qi: (bi * nq + qi, hp)),
                  pl.BlockSpec((None, t, LANES), lambda bi, hp, qi: (bi, 0, hp)),
                  pl.BlockSpec((None, LANES, t), lambda bi, hp, qi: (bi, vrow0 + hp, 0)),
                  pl.BlockSpec((None, None, t, 2), lambda bi, hp, qi: (bi, hp, 0, 0)),
                  pl.BlockSpec((None, None, 2, t), lambda bi, hp, qi: (bi, hp, 0, 0))],
        out_specs=pl.BlockSpec((tq, LANES), lambda bi, hp, qi: (bi * nq + qi, hp)),
        out_shape=jax.ShapeDtypeStruct((b * t, D_MODEL), BF16),
        scratch_shapes=[pltpu.VMEM((2, t, LANES), BF16), pltpu.VMEM((2, max(nq, SUBLANES), LANES), F32),
                        pltpu.VMEM((2, tq, LANES), BF16), pltpu.VMEM((2, 1, tq), F32),
                        pltpu.VMEM((2, LANES, tq), F32),
                        pltpu.VMEM((2, FOX_TK, tq), F32), pltpu.VMEM((2, FOX_TK, tq), F32),
                        pltpu.VMEM((2, LANES, t), BF16)],
        compiler_params=_cparams("parallel", "arbitrary", "arbitrary"),
        name="fox_attention",
    )(q, k, kv_t, cum_col, cum_row)


def _fox_prompt(rows, x, g, mod, w_in, b_f):
    b, t = rows.nseq, rows.t
    w, bias = _fox_weights(w_in, b_f)
    q, kv_t, kv_b, logf = _norm_proj(rows, x, g, mod, (0, 1), w, bias, _fox_proj_outs((FEAT_MAJOR, BF16), "scale2"))
    cum = _cumsum_rows(logf, b, t)[:, :FOX_HEADS].reshape(b, t, FOX_PAIRS, 2)
    o = _fox_attention(b, t, q, kv_b.reshape(b, t, 2 * D_MODEL), kv_t,
                       cum.transpose(0, 2, 1, 3), cum.transpose(0, 2, 3, 1))
    return o, kv_t, logf[:, :FOX_HEADS]


def _pool_kernel(tps, pos0, h_ref, w_ref, b_ref, scale_ref, o_ref, ext_scr):
    i = pl.program_id(0)
    tm = h_ref.shape[0]
    halo = 2 * SUBLANES
    first = (i % tps) == 0

    @pl.when(first)
    def _():
        ext_scr[0:halo, :] = jnp.zeros((halo, D_MODEL), F32)

    @pl.when(jnp.logical_not(first))
    def _():
        ext_scr[0:halo, :] = ext_scr[tm:tm + halo, :]

    ext_scr[halo:halo + tm, :] = h_ref[...]
    pos = pos0 + (i % tps) * tm + lax.broadcasted_iota(jnp.int32, (tm, 1), 0)
    for gi, w in enumerate(POOL_WINDOWS):
        c0, c1 = gi * POOL_GROUP_DIM, (gi + 1) * POOL_GROUP_DIM
        win = ext_scr[halo:halo + tm, c0:c1]
        for back in range(1, w):
            win = win + ext_scr[halo - back:halo - back + tm, c0:c1]
        cnt = jnp.minimum(w, pos + 1).astype(F32)
        mixed = win / cnt - h_ref[:, c0:c1]
        y = _bdot(mixed.astype(BF16), w_ref[gi]) + b_ref[:, c0:c1]
        o_ref[:, c0:c1] = y * scale_ref[:, c0:c1]


def _pool_mix(h_ext, nseq, t, tm, pos0, w_g, b_g, scale):
    assert POOL_STATE < 2 * SUBLANES
    tps = t // tm
    full = lambda shape: pl.BlockSpec(shape, lambda i: (0,) * len(shape))
    return pl.pallas_call(
        functools.partial(_pool_kernel, tps, pos0),
        grid=(nseq * tps,),
        in_specs=[pl.BlockSpec((tm, D_MODEL), lambda i: (i, 0)),
                  full((len(POOL_WINDOWS), POOL_GROUP_DIM, POOL_GROUP_DIM)), full((1, D_MODEL)), full((1, D_MODEL))],
        out_specs=pl.BlockSpec((tm, D_MODEL), lambda i: (i, 0)),
        out_shape=jax.ShapeDtypeStruct((nseq * t, D_MODEL), F32),
        scratch_shapes=[pltpu.VMEM((tm + 2 * SUBLANES, D_MODEL), F32)],
        compiler_params=_cparams("arbitrary"),
        name="pool_mix",
    )(h_ext, w_g.astype(BF16), b_g.reshape(1, D_MODEL), scale.reshape(1, D_MODEL))


def _residual_kernel(o_ref, x_ref, g_ref, gate_ref, y_ref):
    y_ref[...] = x_ref[...] + gate_ref[...] * _rms_rows(o_ref[...], g_ref[...])


def _gated_residual(rows, o, x, g, mod, gate_chunk):
    full = lambda shape: pl.BlockSpec(shape, lambda i: (0,) * len(shape))
    return pl.pallas_call(
        _residual_kernel,
        grid=(rows.ntiles,),
        in_specs=[rows.row_spec(D_MODEL), rows.row_spec(D_MODEL), full((1, D_MODEL)), rows.mod_spec(gate_chunk)],
        out_specs=rows.row_spec(D_MODEL),
        out_shape=jax.ShapeDtypeStruct((rows.m, D_MODEL), F32),
        compiler_params=_cparams("parallel"),
        name="gated_residual",
    )(o, x, g.reshape(1, D_MODEL), rows.mod_arr(mod))


H_ONLY = ((0, D_MODEL, "h", (F32,)),)


def _norm_only(rows, x, g, mod, chunks):
    dummy = jnp.zeros((D_MODEL, LANES), BF16)
    return _norm_proj(rows, x, g, mod, chunks, dummy, jnp.asarray(NO_BIAS), H_ONLY)[0]


DEC_B = 8
NEW_ROWS = SUBLANES


def _paged_grid_spec(grid, in_specs, out_specs, scratch_shapes):
    return pltpu.PrefetchScalarGridSpec(num_scalar_prefetch=1, grid=grid, in_specs=in_specs,
                                        out_specs=out_specs, scratch_shapes=scratch_shapes)


def _feat_major_pages(cache):
    n, p = cache.shape[:2]
    return jnp.moveaxis(cache, 1, -1).reshape(n, -1, p)


def _page_specs(npg, rows):
    return [pl.BlockSpec((None, rows, PAGE_SIZE), functools.partial(lambda b, pt, i: (pt[b * npg + i], 0, 0), i=i))
            for i in range(npg)]


def _cmp_dec_kernel(npg, pt_ref, *refs):
    pages, (wk_ref, wv_ref, o_ref, lhs_scr) = refs[:npg], refs[npg:]
    groups = 2 * NSA_KV_HEADS
    for d in range(HEAD_DIM):
        for i, page in enumerate(pages):
            lhs_scr[i * groups:(i + 1) * groups, d * PAGE_SIZE:(d + 1) * PAGE_SIZE] = page[pl.ds(d, groups, stride=HEAD_DIM), :]
    lhs = lhs_scr[...].astype(BF16)
    is_key = (lax.broadcasted_iota(jnp.int32, o_ref.shape, 0) % groups) < NSA_KV_HEADS
    o_ref[...] = jnp.where(is_key, _bdot(lhs, wk_ref[...]), _bdot(lhs, wv_ref[...])).astype(o_ref.dtype)


def _cmp_dec_weight(w):
    w3 = w.astype(BF16).reshape(NSA_BLOCK, HEAD_DIM, HEAD_DIM)
    halves = PAGE_SIZE // NSA_BLOCK
    eye = jnp.eye(halves, dtype=BF16)
    return jnp.einsum("rde,hg->dhrge", w3, eye).reshape(HEAD_DIM * PAGE_SIZE, halves * HEAD_DIM)


def _cmp_decode(page_table, cache_t, w_ck, w_cv):
    ns, npg = page_table.shape
    halves = PAGE_SIZE // NSA_BLOCK
    groups = 2 * NSA_KV_HEADS
    wspec = pl.BlockSpec((HEAD_DIM * PAGE_SIZE, halves * HEAD_DIM), lambda b, pt: (0, 0))
    out = pl.pallas_call(
        functools.partial(_cmp_dec_kernel, npg),
        grid_spec=_paged_grid_spec(
            (ns,), _page_specs(npg, 2 * KVW) + [wspec, wspec],
            pl.BlockSpec((None, npg * groups, halves * HEAD_DIM), lambda b, pt: (b, 0, 0)),
            [pltpu.VMEM((npg * groups, HEAD_DIM * PAGE_SIZE), F32)]),
        out_shape=jax.ShapeDtypeStruct((ns, npg * groups, halves * HEAD_DIM), BF16),
        compiler_params=_cparams("parallel"),
        name="nsa_decode_compress",
    )(page_table.reshape(-1), *([cache_t] * npg), _cmp_dec_weight(w_ck), _cmp_dec_weight(w_cv))
    out = out.reshape(ns, npg, 2, NSA_KV_HEADS, halves, HEAD_DIM).transpose(0, 1, 4, 2, 3, 5)
    return out.reshape(ns, npg * halves, 2 * KVW)


def _dec_select_kernel(qpos, nblk, qbd_ref, kcv_ref, slope_ref, oc_ref, selb_ref, imp_scr):
    lane_b = lax.broadcasted_iota(jnp.int32, (N_HEADS, nblk), 1)
    ends = lane_b * NSA_BLOCK + (NSA_BLOCK - 1)
    mask = ends <= qpos
    bias = slope_ref[...] * (qpos - ends).astype(F32)
    imp_scr[...] = jnp.zeros(imp_scr.shape, F32)
    for bi in range(DEC_B):
        s = _dot_nt(qbd_ref[bi], kcv_ref[bi, :, 0:KVW]) - bias
        s = jnp.where(mask, s, NEG)
        e = jnp.exp(s - jnp.max(s, axis=-1, keepdims=True))
        p = jnp.where(mask, e / jnp.sum(e, axis=-1, keepdims=True), 0.0)
        oc_ref[bi] = _bdot(p.astype(BF16), kcv_ref[bi, :, KVW:2 * KVW])
        imp = p[0:NSA_KV_HEADS]
        for g in range(1, NSA_GROUP):
            imp = imp + p[g * NSA_KV_HEADS:(g + 1) * NSA_KV_HEADS]
        imp_scr[bi * NSA_KV_HEADS:(bi + 1) * NSA_KV_HEADS, 0:nblk] = imp

    rows = DEC_B * NSA_KV_HEADS
    n2 = lax.broadcasted_iota(jnp.int32, (rows, LANES), 1)
    cur = qpos // NSA_BLOCK
    forced = (n2 == 0) | (n2 == cur) | (n2 == cur - 1)
    score = jnp.where(forced, BIG, jnp.where(n2 <= cur, imp_scr[...], NEG))
    score = jnp.where(n2 <= cur, score, -jnp.inf)

    def pick_next(_, carry):
        sc, sel = carry
        m = jnp.max(sc, axis=-1, keepdims=True)
        first = jnp.min(jnp.where(sc == m, n2, LANES), axis=-1, keepdims=True)
        pick = n2 == first
        sel = jnp.where(pick & (m > 0.5 * NEG), 1.0, sel)
        return jnp.where(pick, -jnp.inf, sc), sel

    _, sel = lax.fori_loop(0, min(NSA_TOPN, cur + 1), pick_next, (score, jnp.zeros((rows, LANES), F32)), unroll=True)
    selb = jnp.where(sel > 0.0, 0.0, NEG).astype(BF16)
    for bi in range(DEC_B):
        one = selb[bi * NSA_KV_HEADS:(bi + 1) * NSA_KV_HEADS]
        selb_ref[bi] = jnp.concatenate([one] * NSA_GROUP, axis=0)


def _dec_slopes():
    return jnp.asarray(_alibi_slopes().T.reshape(N_HEADS, 1))


def _nsa_decode_select(qbd, kcv, qpos):
    ns, nblk = kcv.shape[0], kcv.shape[1]
    assert qpos // NSA_BLOCK < LANES and ns % DEC_B == 0
    blk = lambda shape: pl.BlockSpec((DEC_B,) + shape, lambda i: (i, 0, 0))
    return pl.pallas_call(
        functools.partial(_dec_select_kernel, qpos, nblk),
        grid=(ns // DEC_B,),
        in_specs=[blk((N_HEADS, KVW)), blk((nblk, 2 * KVW)), pl.BlockSpec((N_HEADS, 1), lambda i: (0, 0))],
        out_specs=[blk((N_HEADS, KVW)), blk((N_HEADS, LANES))],
        out_shape=[jax.ShapeDtypeStruct((ns, N_HEADS, KVW), F32), jax.ShapeDtypeStruct((ns, N_HEADS, LANES), BF16)],
        scratch_shapes=[pltpu.VMEM((DEC_B * NSA_KV_HEADS, LANES), F32)],
        compiler_params=_cparams("parallel"),
        name="nsa_decode_select",
    )(qbd, kcv, _dec_slopes())


def _new_key_tile(row):
    r = lax.broadcasted_iota(jnp.int32, (NEW_ROWS, row.shape[1]), 0)
    return jnp.where(r == 0, jnp.broadcast_to(row, (NEW_ROWS, row.shape[1])), 0.0).astype(BF16)


def _new_key_mask(s):
    return jnp.where(lax.broadcasted_iota(jnp.int32, s.shape, 1) == 0, s, NEG)


def _softmax_with_new_key(s, s_new):
    m = jnp.maximum(jnp.max(s, axis=-1, keepdims=True), jnp.max(s_new, axis=-1, keepdims=True))
    p, p_new = jnp.exp(s - m), jnp.exp(s_new - m)
    return p, p_new, jnp.sum(p, axis=-1, keepdims=True) + jnp.sum(p_new, axis=-1, keepdims=True)


def _dec_attend_kernel(npg, qpos, wlen, pt_ref, *refs):
    pages = refs[:npg]
    (qbd_ref, selb_ref, win_ref, snew_ref, wnew_ref, wcol_ref, oc_ref, gate_ref, slope_ref,
     o_ref, wout_ref) = refs[npg:]
    slope = slope_ref[...]
    qbd = qbd_ref[...]

    qaug = jnp.concatenate([qbd, selb_ref[...]], axis=1)
    blk_row = lax.broadcasted_iota(jnp.int32, (LANES, PAGE_SIZE), 0)
    blk_of_lane = lax.broadcasted_iota(jnp.int32, (LANES, PAGE_SIZE), 1) // NSA_BLOCK
    scores = []
    for i, page in enumerate(pages):
        onehot_t = (blk_row == (PAGE_SIZE // NSA_BLOCK) * i + blk_of_lane).astype(BF16)
        kaug_t = jnp.concatenate([page[0:KVW, :].astype(BF16), onehot_t], axis=0)
        scores.append(_bdot(qaug, kaug_t))
    kpos = lax.broadcasted_iota(jnp.int32, (N_HEADS, npg * PAGE_SIZE), 1)
    s = jnp.concatenate(scores, axis=1) - slope * (qpos - kpos).astype(F32)
    snew = snew_ref[...]
    s_new = _new_key_mask(_dot_nt(qbd, _new_key_tile(snew[:, 0:KVW])))
    p, p_new, l = _softmax_with_new_key(s, s_new)
    p = p.astype(BF16)
    acc = _bdot(p_new.astype(BF16), _new_key_tile(snew[:, KVW:2 * KVW]))
    for i, page in enumerate(pages):
        acc = acc + _dot_nt(p[:, i * PAGE_SIZE:(i + 1) * PAGE_SIZE], page[KVW:2 * KVW, :].astype(BF16))
    o_s = acc / l

    win = win_ref[...]
    wpos = qpos - wlen + lax.broadcasted_iota(jnp.int32, (N_HEADS, wlen), 1)
    s = _bdot(qbd, win[0:KVW, :].astype(BF16)) - slope * (qpos - wpos).astype(F32)
    s = jnp.where(wpos > qpos - NSA_WINDOW, s, NEG)
    wnew = wnew_ref[...]
    s_new = _new_key_mask(_dot_nt(qbd, _new_key_tile(wnew[:, 0:KVW])))
    p, p_new, l = _softmax_with_new_key(s, s_new)
    o_w = (_dot_nt(p.astype(BF16), win[KVW:2 * KVW, :].astype(BF16))
           + _bdot(p_new.astype(BF16), _new_key_tile(wnew[:, KVW:2 * KVW]))) / l

    gates = gate_ref[...]
    o_ref[...] = gates[:, 0:1] * oc_ref[...] + gates[:, 1:2] * o_s + gates[:, 2:3] * o_w
    lane = lax.broadcasted_iota(jnp.int32, win.shape, 1)
    wout_ref[...] = jnp.where(lane == wlen - 1, wcol_ref[...], pltpu.roll(win, wlen - 1, axis=1))


def _nsa_decode_attend(page_table, qbd, selb, cache_t, win_t, kvs_new, kvw_new, oc, gates, qpos):
    ns, npg = page_table.shape
    wlen = win_t.shape[2]
    assert qpos == npg * PAGE_SIZE
    per_b = lambda shape: pl.BlockSpec((None,) + shape, lambda b, pt: (b, 0, 0))
    return pl.pallas_call(
        functools.partial(_dec_attend_kernel, npg, qpos, wlen),
        grid_spec=_paged_grid_spec(
            (ns,),
            _page_specs(npg, 2 * KVW)
            + [per_b((N_HEADS, KVW)), per_b((N_HEADS, LANES)), per_b((2 * KVW, wlen)), per_b((1, 2 * KVW)),
               per_b((1, 2 * KVW)), per_b((2 * KVW, 1)), per_b((N_HEADS, KVW)), per_b((N_HEADS, LANES)),
               pl.BlockSpec((N_HEADS, 1), lambda b, pt: (0, 0))],
            [per_b((N_HEADS, KVW)), per_b((2 * KVW, wlen))],
            []),
        out_shape=[jax.ShapeDtypeStruct((ns, N_HEADS, KVW), F32), jax.ShapeDtypeStruct(win_t.shape, F32)],
        compiler_params=_cparams("parallel"),
        name="nsa_decode_attend",
    )(page_table.reshape(-1), *([cache_t] * npg), qbd, selb, win_t, kvs_new.reshape(ns, 1, 2 * KVW),
      kvw_new.reshape(ns, 1, 2 * KVW), kvw_new.reshape(ns, 2 * KVW, 1), oc, gates, _dec_slopes())


def _nsa_sample(rows, x, g, mod, w_in, w_ck, w_cv, cache_c, cache_s, win_buf, page_table):
    ns = rows.nseq
    qpos = page_table.shape[1] * PAGE_SIZE
    q, kvc, kvs, kvw, gates = _norm_proj(
        rows, x, g, mod, (0, 1), _nsa_weights(w_in), jnp.asarray(NO_BIAS), _nsa_proj_outs((F32,)))
    kcv = _cmp_decode(page_table, _feat_major_pages(cache_c), w_ck, w_cv)
    q4 = q.reshape(ns, NSA_KV_HEADS, NSA_GROUP, HEAD_DIM)
    eye = jnp.eye(NSA_KV_HEADS, dtype=q.dtype)
    qbd = jnp.einsum("bkgd,kj->bgkjd", q4, eye).reshape(ns, N_HEADS, KVW)
    oc, selb = _nsa_decode_select(qbd, kcv, qpos)
    g4 = gates.reshape(ns, NSA_KV_HEADS, LANES)[:, :, :NSA_GROUP * 3].reshape(ns, NSA_KV_HEADS, NSA_GROUP, 3)
    g_rows = jnp.pad(g4.transpose(0, 2, 1, 3).reshape(ns, N_HEADS, 3), ((0, 0), (0, 0), (0, LANES - 3)))
    o_bd, win_out = _nsa_decode_attend(
        page_table, qbd, selb, _feat_major_pages(cache_s), _feat_major_pages(win_buf), kvs, kvw, oc, g_rows, qpos)
    o5 = o_bd.reshape(ns, NSA_GROUP, NSA_KV_HEADS, NSA_KV_HEADS, HEAD_DIM)
    o = jnp.einsum("bgkkd->bkgd", o5).reshape(ns, D_MODEL).astype(BF16)
    win_out = jnp.moveaxis(win_out.reshape(win_buf.shape[:1] + win_buf.shape[2:] + win_buf.shape[1:2]), -1, 1)
    return o, kvc, kvs, win_out


def _fox_dec_kernel(npg, pt_ref, *refs):
    kv_pages, lf_pages = refs[:npg], refs[npg:2 * npg]
    q_ref, kvn_ref, lfn_ref, o_ref = refs[2 * npg:]
    own = (lax.broadcasted_iota(jnp.int32, (FOX_HEADS, D_MODEL), 1) // HEAD_DIM
           == lax.broadcasted_iota(jnp.int32, (FOX_HEADS, D_MODEL), 0))
    q = jnp.broadcast_to(q_ref[...].astype(F32), (FOX_HEADS, D_MODEL))
    qbd = jnp.where(own, q, 0.0).astype(BF16)
    upper = (lax.broadcasted_iota(jnp.int32, (PAGE_SIZE, PAGE_SIZE), 0)
             <= lax.broadcasted_iota(jnp.int32, (PAGE_SIZE, PAGE_SIZE), 1)).astype(BF16)
    carry = jnp.zeros((FOX_HEADS, 1), F32)
    cums, scores = [], []
    for kv_page, lf_page in zip(kv_pages, lf_pages):
        a, b, c = _split3(lf_page[...])
        cum = carry + (_bdot(a, upper) + _bdot(b, upper) + _bdot(c, upper))
        carry = cum[:, PAGE_SIZE - 1:PAGE_SIZE]
        cums.append(cum)
        scores.append(_bdot(qbd, kv_page[0:D_MODEL, :].astype(BF16)))
    cum_new = carry + lfn_ref[...]
    s = jnp.concatenate(scores, axis=1) + (cum_new - jnp.concatenate(cums, axis=1))
    kvn = kvn_ref[...]
    s_new = _new_key_mask(_dot_nt(qbd, _new_key_tile(kvn[:, 0:D_MODEL])))
    p, p_new, l = _softmax_with_new_key(s, s_new)
    p = p.astype(BF16)
    acc = _bdot(p_new.astype(BF16), _new_key_tile(kvn[:, D_MODEL:2 * D_MODEL]))
    for i, kv_page in enumerate(kv_pages):
        acc = acc + _dot_nt(p[:, i * PAGE_SIZE:(i + 1) * PAGE_SIZE], kv_page[D_MODEL:2 * D_MODEL, :].astype(BF16))
    o_ref[...] = jnp.sum(jnp.where(own, acc / l, 0.0), axis=0, keepdims=True).astype(o_ref.dtype)


def _fox_decode(page_table, q, cache_kv_t, cache_logf_t, kv_new, logf_new):
    ns, npg = page_table.shape
    per_b = lambda shape: pl.BlockSpec((None,) + shape, lambda b, pt: (b, 0, 0))
    out = pl.pallas_call(
        functools.partial(_fox_dec_kernel, npg),
        grid_spec=_paged_grid_spec(
            (ns,),
            _page_specs(npg, 2 * D_MODEL) + _page_specs(npg, FOX_HEADS)
            + [per_b((1, D_MODEL)), per_b((1, 2 * D_MODEL)), per_b((FOX_HEADS, 1))],
            per_b((1, D_MODEL)), []),
        out_shape=jax.ShapeDtypeStruct((ns, 1, D_MODEL), BF16),
        compiler_params=_cparams("parallel"),
        name="fox_decode",
    )(page_table.reshape(-1), *([cache_kv_t] * npg), *([cache_logf_t] * npg), q.reshape(ns, 1, D_MODEL),
      kv_new.reshape(ns, 1, 2 * D_MODEL), logf_new.reshape(ns, FOX_HEADS, 1))
    return out.reshape(ns, D_MODEL)


def _fox_sample(rows, x, g, mod, w_in, b_f, cache_kv, cache_logf, page_table):
    w, bias = _fox_weights(w_in, b_f)
    q, kv, logf = _norm_proj(rows, x, g, mod, (0, 1), w, bias, _fox_proj_outs((F32,), "scale"))
    logf = logf[:, :FOX_HEADS]
    o = _fox_decode(page_table, q, _feat_major_pages(cache_kv), _feat_major_pages(cache_logf), kv, logf)
    return o, kv, logf


PROMPT_TM = 512
POOL_TM = 256


def kernel(x_prompt, x_sample, cache_l0_cmp_kv, cache_l0_sel_kv, state_l0_win_kv, cache_l1_kv, cache_l1_logf, state_l2_pool, cache_l3_cmp_kv, cache_l3_sel_kv, state_l3_win_kv, state_ffn_conv, page_table, c_prompt, c_sample, mod_w, mod_b, norm_g, l0_nsa_w_in, l0_nsa_w_ck, l0_nsa_w_cv, l0_nsa_w_o, l1_fox_w_in, l1_fox_b_f, l1_fox_w_o, l2_pool_w, l2_pool_b, l2_pool_scale, l3_nsa_w_in, l3_nsa_w_ck, l3_nsa_w_cv, l3_nsa_w_o, ffn_w_gu, ffn_conv_w, ffn_conv_b, ffn_w_d):
    b, t, _ = x_prompt.shape
    ns = x_sample.shape[0]
    past_len = page_table.shape[1] * PAGE_SIZE
    rp = _Rows(b, t, min(PROMPT_TM, t))
    rs = _Rows(ns, 1, ns)
    nsa = {0: (cache_l0_cmp_kv, cache_l0_sel_kv, state_l0_win_kv, l0_nsa_w_in, l0_nsa_w_ck, l0_nsa_w_cv, l0_nsa_w_o),
           3: (cache_l3_cmp_kv, cache_l3_sel_kv, state_l3_win_kv, l3_nsa_w_in, l3_nsa_w_ck, l3_nsa_w_cv, l3_nsa_w_o)}

    c_all = jnp.concatenate([c_prompt, c_sample], axis=0)
    c_all = jnp.pad(c_all, ((0, -c_all.shape[0] % SUBLANES), (0, 0)))
    mod = _modulation(c_all, mod_w, mod_b)

    xp = x_prompt.reshape(b * t, D_MODEL)
    xs = x_sample.reshape(ns, D_MODEL)
    st = {}
    conv_p, conv_s = [], []
    kv5 = lambda a, n: a.reshape(n, -1, 2, NSA_KV_HEADS, HEAD_DIM)
    for i in range(DEPTH):
        mp, ms = mod[i, :b], mod[i, b:b + ns]
        g = norm_g[i]
        kind = i % 3
        if kind == 0:
            c_c, c_s, s_w, w_in, w_ck, w_cv, w_o = nsa[i]
            op, kvc_t, kvs_t, kvw_t = _nsa_prompt(rp, xp, g[0], mp, w_in, w_ck, w_cv)
            os_, kvc_s, kvs_s, win_s = _nsa_sample(rs, xs, g[0], ms, w_in, w_ck, w_cv, c_c, c_s, s_w, page_table)
            leaf = lambda a: _leaf_from_feat_major(a, (2, NSA_KV_HEADS, HEAD_DIM))
            st[i] = (leaf(kvc_t), kv5(kvc_s, ns), leaf(kvs_t), kv5(kvs_s, ns),
                     leaf(kvw_t[:, :, -min(NSA_WINDOW, t):]), win_s)
            w_ob = w_o.astype(BF16)
            xp = _out_proj_residual(rp, op, w_ob, xp, g[1], mp, 2)
            xs = _out_proj_residual(rs, os_, w_ob, xs, g[1], ms, 2)
        elif kind == 1:
            op, kv_p, lf_p = _fox_prompt(rp, xp, g[0], mp, l1_fox_w_in, l1_fox_b_f)
            os_, kv_s, lf_s = _fox_sample(rs, xs, g[0], ms, l1_fox_w_in, l1_fox_b_f, cache_l1_kv, cache_l1_logf, page_table)
            st[i] = (_leaf_from_feat_major(kv_p, (2, FOX_HEADS, HEAD_DIM)), kv_s.reshape(ns, 1, 2, FOX_HEADS, HEAD_DIM),
                     lf_p.reshape(b, t, FOX_HEADS), lf_s.reshape(ns, 1, FOX_HEADS))
            w_ob = l1_fox_w_o.astype(BF16)
            xp = _out_proj_residual(rp, op, w_ob, xp, g[1], mp, 2)
            xs = _out_proj_residual(rs, os_, w_ob, xs, g[1], ms, 2)
        else:
            hp = _norm_only(rp, xp, g[0], mp, (0, 1))
            hs = _norm_only(rs, xs, g[0], ms, (0, 1))
            yp = _pool_mix(hp, b, t, min(POOL_TM, t), 0, l2_pool_w, l2_pool_b.reshape(-1), l2_pool_scale)
            ext = jnp.concatenate([state_l2_pool, hs[:, None, :]], axis=1)
            n_ext = POOL_STATE + 1
            ys = _pool_mix(ext.reshape(ns * n_ext, D_MODEL), ns, n_ext, n_ext, past_len - POOL_STATE,
                           l2_pool_w, l2_pool_b.reshape(-1), l2_pool_scale).reshape(ns, n_ext, D_MODEL)[:, -1]
            st[i] = (hp.reshape(b, t, D_MODEL)[:, -POOL_STATE:], ext[:, -POOL_STATE:])
            xp = _gated_residual(rp, yp, xp, g[1], mp, 2)
            xs = _gated_residual(rs, ys, xs, g[1], ms, 2)
        w_gu, w_d = ffn_w_gu[i].astype(BF16), ffn_w_d[i].astype(BF16)
        xp, cp = _conv_ffn(rp, xp, g[2], g[3], mp, w_gu, ffn_conv_w[i], ffn_conv_b[i], w_d, None)
        xs, cs = _conv_ffn(rs, xs, g[2], g[3], ms, w_gu, ffn_conv_w[i], ffn_conv_b[i], w_d, state_ffn_conv[i])
        conv_p.append(cp)
        conv_s.append(cs)
    return (xp.reshape(b, t, D_MODEL), xs.reshape(ns, 1, D_MODEL),
            *st[0], *st[1], *st[2], *st[3],
            jnp.stack(conv_p), jnp.stack(conv_s))
```

```python
import functools

import jax
import jax.numpy as jnp
import numpy as np
from jax import lax
from jax.experimental import pallas as pl
from jax.experimental.pallas import tpu as pltpu

D_MODEL = 1024
DEPTH = 4
PAGE_SIZE = 128
HEAD_DIM = 64
N_HEADS = D_MODEL // HEAD_DIM
NSA_KV_HEADS = 4
NSA_GROUP = N_HEADS // NSA_KV_HEADS
NSA_BLOCK = 64
NSA_TOPN = 16
NSA_WINDOW = 512
FOX_HEADS = D_MODEL // HEAD_DIM
POOL_WINDOWS = (2, 4, 8, 16)
POOL_GROUP_DIM = D_MODEL // len(POOL_WINDOWS)
POOL_STATE = max(POOL_WINDOWS) - 1
D_FF = 2816
CONV_W = 3
N_MOD = 6
RMS_EPS = 1e-6
NEG = -1e30
BIG = 1e30
ATTN_SCALE = HEAD_DIM ** -0.5
LOG2E = float(np.log2(np.e))

LANES = 128
SUBLANES = 8
VMEM_LIMIT = 56 * 1024 * 1024

KVW = NSA_KV_HEADS * HEAD_DIM
NSA_Q0, NSA_C0, NSA_S0, NSA_W0, NSA_G0 = 0, D_MODEL, D_MODEL + 2 * KVW, D_MODEL + 4 * KVW, D_MODEL + 6 * KVW
NSA_COLS = NSA_G0 + NSA_KV_HEADS * LANES
FOX_LF0 = 3 * D_MODEL
FOX_COLS = FOX_LF0 + LANES

F32 = jnp.float32
BF16 = jnp.bfloat16


def _cparams(*sem):
    return pltpu.CompilerParams(dimension_semantics=sem, vmem_limit_bytes=VMEM_LIMIT)


def _bdot(a, b):
    return jnp.dot(a, b, preferred_element_type=F32)


def _dot_nt(a, b):
    return lax.dot_general(a, b, (((1,), (1,)), ((), ())), preferred_element_type=F32)


def _rms_rows(x, g):
    return x * lax.rsqrt(jnp.mean(x * x, axis=-1, keepdims=True) + RMS_EPS) * g


def _split3(x):
    a = x.astype(BF16)
    r = x - a.astype(F32)
    b = r.astype(BF16)
    c = (r - b.astype(F32)).astype(BF16)
    return a, b, c


def _mod_kernel(c_ref, w_ref, b_ref, o_ref):
    c = c_ref[...]
    a = (c * jax.nn.sigmoid(c)).astype(BF16)
    o_ref[...] = _bdot(a, w_ref[...].astype(BF16)) + b_ref[...]


def _modulation(c_all, mod_w, mod_b):
    rows = c_all.shape[0]
    n = N_MOD * D_MODEL
    tn = D_MODEL
    return pl.pallas_call(
        _mod_kernel,
        grid=(DEPTH, n // tn),
        in_specs=[pl.BlockSpec((rows, D_MODEL), lambda i, j: (0, 0)),
                  pl.BlockSpec((None, D_MODEL, tn), lambda i, j: (i, 0, j)),
                  pl.BlockSpec((None, 1, tn), lambda i, j: (i, 0, j))],
        out_specs=pl.BlockSpec((None, rows, tn), lambda i, j: (i, 0, j)),
        out_shape=jax.ShapeDtypeStruct((DEPTH, rows, n), F32),
        compiler_params=_cparams("parallel", "parallel"),
        name="modulation",
    )(c_all, mod_w, mod_b.reshape(DEPTH, 1, n))


def _norm_h(x, g, shift, scale):
    return _rms_rows(x, g) * (1.0 + scale) + shift


class _Rows:
    def __init__(self, nseq, t, tm):
        assert t % tm == 0 or t == 1
        self.nseq, self.t = nseq, t
        self.decode = t == 1
        self.tm = nseq if self.decode else tm
        self.m = nseq * t
        self.tiles_per_seq = 1 if self.decode else t // tm
        self.ntiles = self.m // self.tm

    def mod_spec(self, chunk):
        if self.decode:
            return pl.BlockSpec((self.tm, D_MODEL), lambda i, *_: (0, chunk))
        tps = self.tiles_per_seq
        return pl.BlockSpec((None, 1, D_MODEL), lambda i, *_: (i // tps, 0, chunk))

    def mod_arr(self, mod):
        return mod if self.decode else mod.reshape(self.nseq, 1, N_MOD * D_MODEL)

    def row_spec(self, width, col=0):
        return pl.BlockSpec((self.tm, width), lambda i, *_: (i, col))


def _proj_kernel(outs, x_ref, g_ref, sh_ref, sc_ref, w_ref, bias_ref, *o_refs):
    h = _norm_h(x_ref[...], g_ref[...], sh_ref[...], sc_ref[...])
    hb = h.astype(BF16)
    refs = iter(o_refs)
    for c0, c1, kind, dtypes in outs:
        if kind == "h":
            p = h
        else:
            p = _bdot(hb, w_ref[:, c0:c1])
        if kind == "qscales":
            next(refs)[...] = (p * ATTN_SCALE).astype(BF16)
            next(refs)[...] = (p * (ATTN_SCALE * LOG2E)).astype(BF16)
            continue
        if kind == "scale":
            p = p * ATTN_SCALE
        elif kind == "scale2":
            p = p * (ATTN_SCALE * LOG2E)
        elif kind == "sigmoid":
            p = jax.nn.sigmoid(p)
        elif kind == "logsigmoid":
            p = jax.nn.log_sigmoid(p + bias_ref[...])
        for dt in dtypes:
            if dt == FEAT_MAJOR:
                next(refs)[...] = p.T
            else:
                next(refs)[...] = p.astype(dt)


FEAT_MAJOR = "feature-major f32"


def _norm_proj(rows, x, g, mod, chunks, w, bias, outs):
    ncols = w.shape[1]
    kern = functools.partial(_proj_kernel, outs)
    full = lambda shape: pl.BlockSpec(shape, lambda i: (0,) * len(shape))
    marr = rows.mod_arr(mod)
    flat = [(c1 - c0, dt) for c0, c1, _, dts in outs for dt in dts]
    tps = rows.tiles_per_seq

    def spec(wd, dt):
        if dt == FEAT_MAJOR:
            return pl.BlockSpec((None, wd, rows.tm), lambda i: (i // tps, 0, i % tps))
        return rows.row_spec(wd)

    def shape(wd, dt):
        if dt == FEAT_MAJOR:
            return jax.ShapeDtypeStruct((rows.nseq, wd, rows.t), F32)
        return jax.ShapeDtypeStruct((rows.m, wd), dt)

    return pl.pallas_call(
        kern,
        grid=(rows.ntiles,),
        in_specs=[rows.row_spec(D_MODEL), full((1, D_MODEL)),
                  rows.mod_spec(chunks[0]), rows.mod_spec(chunks[1]),
                  full((D_MODEL, ncols)), full((1, LANES))],
        out_specs=[spec(wd, dt) for wd, dt in flat],
        out_shape=[shape(wd, dt) for wd, dt in flat],
        compiler_params=_cparams("parallel"),
        name="norm_proj",
    )(x, g.reshape(1, D_MODEL), marr, marr, w, bias)


def _oproj_kernel(o_ref, w_ref, x_ref, g_ref, gate_ref, y_ref):
    y = _bdot(o_ref[...], w_ref[...])
    y_ref[...] = x_ref[...] + gate_ref[...] * _rms_rows(y, g_ref[...])


def _out_proj_residual(rows, o, w, x, g, mod, gate_chunk):
    k = w.shape[0]
    full = lambda shape: pl.BlockSpec(shape, lambda i: (0,) * len(shape))
    return pl.pallas_call(
        _oproj_kernel,
        grid=(rows.ntiles,),
        in_specs=[rows.row_spec(k), full((k, D_MODEL)), rows.row_spec(D_MODEL),
                  full((1, D_MODEL)), rows.mod_spec(gate_chunk)],
        out_specs=rows.row_spec(D_MODEL),
        out_shape=jax.ShapeDtypeStruct((rows.m, D_MODEL), F32),
        compiler_params=_cparams("parallel"),
        name="out_proj_residual",
    )(o, w, x, g.reshape(1, D_MODEL), rows.mod_arr(mod))


FFN_TF = 256
FFN_NJ = D_FF // FFN_TF
HALO = SUBLANES


def _ffn_kernel(decode, tiles_per_seq, x_ref, g2_ref, sh_ref, sc_ref, wg_ref, wu_ref, cw_ref, cb_ref,
                wd_ref, g3_ref, gate_ref, p0_ref, p1_ref, y_ref, st_ref, h_scr, act_scr, gs_scr, carry_scr):
    i, j = pl.program_id(0), pl.program_id(1)
    tm = x_ref.shape[0]
    tf = wg_ref.shape[1]

    @pl.when(j == 0)
    def _():
        h_scr[...] = _norm_h(x_ref[...], g2_ref[...], sh_ref[...], sc_ref[...]).astype(BF16)

    h = h_scr[...]
    gcol = _bdot(h, wg_ref[...])
    ucol = _bdot(h, wu_ref[...])
    cw = cw_ref[...]
    if decode:
        a = cb_ref[...] + cw[0:1] * p0_ref[...] + cw[1:2] * p1_ref[...] + cw[2:3] * gcol
        st_ref[...] = gcol
    else:
        first = (i % tiles_per_seq) == 0

        @pl.when(first)
        def _():
            gs_scr[0:HALO, :] = jnp.zeros((HALO, gs_scr.shape[1]), F32)

        @pl.when(jnp.logical_not(first))
        def _():
            gs_scr[0:HALO, :] = carry_scr[j]

        gs_scr[HALO:HALO + tm, :] = gcol
        a = (cb_ref[...] + cw[0:1] * gs_scr[HALO - 2:HALO - 2 + tm, :]
             + cw[1:2] * gs_scr[HALO - 1:HALO - 1 + tm, :] + cw[2:3] * gcol)
        tail = gcol[tm - HALO:tm, :]
        carry_scr[j] = tail
        st_ref[...] = tail
    act_scr[:, pl.ds(pl.multiple_of(j * tf, tf), tf)] = (a * jax.nn.sigmoid(a) * ucol).astype(BF16)

    @pl.when(j == pl.num_programs(1) - 1)
    def _():
        y = _bdot(act_scr[...], wd_ref[...])
        y_ref[...] = x_ref[...] + gate_ref[...] * _rms_rows(y, g3_ref[...])


def _conv_ffn(rows, x, g2, g3, mod, w_gu, conv_w, conv_b, w_d, prev):
    tm, tf = rows.tm, FFN_TF
    decode = rows.decode
    tps = rows.tiles_per_seq
    kern = functools.partial(_ffn_kernel, decode, tps)
    full = lambda shape: pl.BlockSpec(shape, lambda i, j: (0,) * len(shape))
    col = lambda r: pl.BlockSpec((r, tf), lambda i, j: (0, j))
    if decode:
        p0, p1 = prev[:, 0], prev[:, 1]
        pspec = pl.BlockSpec((tm, tf), lambda i, j: (0, j))
        st_spec = pl.BlockSpec((tm, tf), lambda i, j: (0, j))
        st_shape = jax.ShapeDtypeStruct((rows.m, D_FF), F32)
    else:
        p0 = p1 = jnp.zeros((SUBLANES, LANES), F32)
        pspec = full((SUBLANES, LANES))
        st_spec = pl.BlockSpec((None, HALO, tf), lambda i, j: (i, 0, j))
        st_shape = jax.ShapeDtypeStruct((rows.ntiles, HALO, D_FF), F32)
    marr = rows.mod_arr(mod)
    y, st = pl.pallas_call(
        kern,
        grid=(rows.ntiles, FFN_NJ),
        in_specs=[rows.row_spec(D_MODEL), full((1, D_MODEL)), rows.mod_spec(3), rows.mod_spec(4),
                  pl.BlockSpec((D_MODEL, tf), lambda i, j: (0, j)),
                  pl.BlockSpec((D_MODEL, tf), lambda i, j: (0, FFN_NJ + j)),
                  col(CONV_W), col(1),
                  full((D_FF, D_MODEL)),
                  full((1, D_MODEL)), rows.mod_spec(5), pspec, pspec],
        out_specs=[rows.row_spec(D_MODEL), st_spec],
        out_shape=[jax.ShapeDtypeStruct((rows.m, D_MODEL), F32), st_shape],
        scratch_shapes=[pltpu.VMEM((tm, D_MODEL), BF16), pltpu.VMEM((tm, D_FF), BF16),
                        pltpu.VMEM((HALO + tm, tf), F32), pltpu.VMEM((FFN_NJ, HALO, tf), F32)],
        compiler_params=_cparams("arbitrary", "arbitrary"),
        name="conv_ffn",
    )(x, g2.reshape(1, D_MODEL), marr, marr, w_gu, w_gu, conv_w, conv_b.reshape(1, D_FF),
      w_d, g3.reshape(1, D_MODEL), marr, p0, p1)
    if decode:
        return y, jnp.stack([prev[:, 1], st], axis=1)
    last = st.reshape(rows.nseq, tps, HALO, D_FF)[:, -1]
    return y, last[:, HALO - (CONV_W - 1):]


def _mm_kernel(a_ref, w_ref, o_ref):
    o_ref[...] = _bdot(a_ref[...], w_ref[...])


def _matmul(a, w, tm):
    m, k = a.shape
    n = w.shape[1]
    return pl.pallas_call(
        _mm_kernel,
        grid=(m // tm,),
        in_specs=[pl.BlockSpec((tm, k), lambda i: (i, 0)), pl.BlockSpec((k, n), lambda i: (0, 0))],
        out_specs=pl.BlockSpec((tm, n), lambda i: (i, 0)),
        out_shape=jax.ShapeDtypeStruct((m, n), F32),
        compiler_params=_cparams("parallel"),
        name="matmul",
    )(a, w)


Q_TILE = 128
KEY_CHUNK = 128
GQ = NSA_GROUP * Q_TILE
SEL_UNROLL = 2
SCAN_UNROLL = 4
M_INIT = -3e38


def _alibi_slopes():
    h = np.arange(1, N_HEADS + 1, dtype=np.float32)
    return np.exp2(-8.0 * h / N_HEADS).astype(np.float32).reshape(NSA_KV_HEADS, NSA_GROUP)


def _half_masks():
    lane = lax.broadcasted_iota(jnp.int32, (Q_TILE, LANES), 1)
    return lane < HEAD_DIM


def _stack_group_heads(q, lo, par=None):
    tiles = []
    for g in range(NSA_GROUP):
        t = q[:, (g // 2) * LANES:(g // 2 + 1) * LANES]
        keep = lo if g % 2 == 0 else jnp.logical_not(lo)
        t = jnp.where(keep, t, jnp.zeros_like(t))
        if par is not None:
            rolled = pltpu.roll(t.astype(F32), HEAD_DIM, axis=1).astype(BF16)
            t = jnp.where(par == g % 2, t, rolled)
        tiles.append(t)
    return jnp.concatenate(tiles, axis=0)


def _nsa_cmp_kernel(nb, q_ref, kcc_ref, vcc_ref, slope_ref, oc_ref, selb_ref, any_ref):
    j = pl.program_id(2)
    q0 = j * Q_TILE
    lo = _half_masks()
    qs = _stack_group_heads(q_ref[...], lo)
    st = _dot_nt(kcc_ref[...], qs)
    n_i = lax.broadcasted_iota(jnp.int32, (nb, GQ), 0)
    qpos = q0 + (lax.broadcasted_iota(jnp.int32, (nb, GQ), 1) & (Q_TILE - 1))
    ends = n_i * NSA_BLOCK + (NSA_BLOCK - 1)
    mask = ends <= qpos
    s = st - slope_ref[...] * (qpos - ends).astype(F32)
    s = jnp.where(mask, s, NEG)
    e = jnp.exp(s - jnp.max(s, axis=0, keepdims=True))
    p = jnp.where(mask, e / jnp.sum(e, axis=0, keepdims=True), 0.0)
    acc = _bdot(p.T.astype(BF16), vcc_ref[...])
    for t in range(NSA_GROUP // 2):
        a0 = acc[(2 * t) * Q_TILE:(2 * t + 1) * Q_TILE]
        a1 = acc[(2 * t + 1) * Q_TILE:(2 * t + 2) * Q_TILE]
        oc_ref[:, t * LANES:(t + 1) * LANES] = jnp.where(lo, a0, a1)

    imp = p[:, 0:Q_TILE]
    for g in range(1, NSA_GROUP):
        imp = imp + p[:, g * Q_TILE:(g + 1) * Q_TILE]
    n2 = lax.broadcasted_iota(jnp.int32, (nb, Q_TILE), 0)
    cur = (q0 + lax.broadcasted_iota(jnp.int32, (nb, Q_TILE), 1)) // NSA_BLOCK
    forced = (n2 == 0) | (n2 == cur) | (n2 == cur - 1)
    cur_row = cur[0:1]
    budget = min(NSA_TOPN, nb) - (1 + (cur_row >= 1).astype(jnp.int32) + (cur_row >= 2).astype(jnp.int32))
    score = jnp.where(forced, -jnp.inf, jnp.where(n2 <= cur, imp, NEG))

    def pick_next(r, carry):
        sc, sel = carry
        m = jnp.max(sc, axis=0, keepdims=True)
        first = jnp.min(jnp.where(sc == m, n2, nb), axis=0, keepdims=True)
        pick = n2 == first
        sel = jnp.where(pick & (m > 0.5 * NEG) & (r < budget), 1.0, sel)
        return jnp.where(pick, -jnp.inf, sc), sel

    common = max(min(NSA_TOPN, nb) - 3, 0)
    carry = lax.fori_loop(0, common, pick_next, (score, forced.astype(F32)), unroll=True)
    _, sel = lax.cond(j == 0, lambda c: lax.fori_loop(common, common + 2, pick_next, c, unroll=True), lambda c: c, carry)
    sel_t = sel.T
    if nb < LANES:
        sel_t = jnp.concatenate([sel_t, jnp.zeros((Q_TILE, LANES - nb), F32)], axis=1)
    selb_ref[...] = jnp.where(sel_t > 0.0, 0.0, NEG).astype(BF16)
    any_ref[...] = jnp.max(sel_t, axis=0, keepdims=True)


def _nsa_compressed(b, t, q, kcc, vcc):
    nb = t // NSA_BLOCK
    nq = t // Q_TILE
    slopes = jnp.asarray(np.repeat(_alibi_slopes(), Q_TILE, axis=1).reshape(NSA_KV_HEADS, 1, GQ))
    return pl.pallas_call(
        functools.partial(_nsa_cmp_kernel, nb),
        grid=(b, NSA_KV_HEADS, nq),
        in_specs=[pl.BlockSpec((Q_TILE, KVW), lambda bi, k, j: (bi * nq + j, k)),
                  pl.BlockSpec((None, None, nb, LANES), lambda bi, k, j: (bi, k, 0, 0)),
                  pl.BlockSpec((None, None, nb, LANES), lambda bi, k, j: (bi, k, 0, 0)),
                  pl.BlockSpec((None, 1, GQ), lambda bi, k, j: (k, 0, 0))],
        out_specs=[pl.BlockSpec((Q_TILE, KVW), lambda bi, k, j: (bi * nq + j, k)),
                   pl.BlockSpec((None, None, Q_TILE, LANES), lambda bi, k, j: (bi, k, j, 0)),
                   pl.BlockSpec((None, None, None, 1, LANES), lambda bi, k, j: (bi, k, j, 0, 0))],
        out_shape=[jax.ShapeDtypeStruct((b * t, D_MODEL), F32),
                   jax.ShapeDtypeStruct((b, NSA_KV_HEADS, t, LANES), BF16),
                   jax.ShapeDtypeStruct((b, NSA_KV_HEADS, nq, 1, LANES), F32)],
        compiler_params=_cparams("parallel", "parallel", "parallel"),
        name="nsa_compressed",
    )(q, kcc, vcc, slopes)


def _nsa_sw_kernel(nq, flags_ref, q_ref, selb_ref, ks_ref, vst_ref, kw_ref, vwt_ref, posx_ref, onehot_ref, qx_ref,
                   oc_ref, gate_ref, slope_ref, o_ref, ksa_scr, kwa_scr, qaug_scr, ms_scr, as_scr, mw_scr, aw_scr,
                   sta_scr, stb_scr, vsa_scr, vwa_scr, list_scr):
    k, j = pl.program_id(1), pl.program_id(2)
    par = k % 2
    lane_b = lax.broadcasted_iota(jnp.int32, (KEY_CHUNK, LANES), 1)

    @pl.when(j == 0)
    def _():
        def fill(c, carry):
            r0 = pl.multiple_of(c * KEY_CHUNK, KEY_CHUNK)
            own = (lane_b // HEAD_DIM) == par
            px = posx_ref[...]
            ksa_scr[pl.ds(r0, KEY_CHUNK), 0:LANES] = onehot_ref[pl.ds(r0, KEY_CHUNK), :]
            ksa_scr[pl.ds(r0, KEY_CHUNK), LANES:2 * LANES] = jnp.where(own, ks_ref[pl.ds(r0, KEY_CHUNK), :], px)
            kwa_scr[pl.ds(r0, KEY_CHUNK), :] = jnp.where(own, kw_ref[pl.ds(r0, KEY_CHUNK), :], px)
            own_rows = (lax.broadcasted_iota(jnp.int32, (LANES, KEY_CHUNK), 0) // HEAD_DIM) == par
            vsa_scr[:, pl.ds(r0, KEY_CHUNK)] = jnp.where(own_rows, vst_ref[:, pl.ds(r0, KEY_CHUNK)], 1.0).astype(BF16)
            vwa_scr[:, pl.ds(r0, KEY_CHUNK)] = jnp.where(own_rows, vwt_ref[:, pl.ds(r0, KEY_CHUNK)], 1.0).astype(BF16)
            return carry

        lax.fori_loop(0, nq, fill, 0)

    lo = _half_masks()
    qq = _stack_group_heads(q_ref[...], lo, par)
    own_q = (lax.broadcasted_iota(jnp.int32, (GQ, LANES), 1) // HEAD_DIM) == par
    qaug_scr[:, 0:LANES] = jnp.concatenate([selb_ref[...]] * NSA_GROUP, axis=0)
    qaug_scr[:, LANES:2 * LANES] = jnp.where(own_q, qq, qx_ref[...])
    slope = slope_ref[...]
    for m_ref, a_ref in ((ms_scr, as_scr), (mw_scr, aw_scr)):
        m_ref[...] = jnp.full(m_ref.shape, M_INIT, F32)
        a_ref[...] = jnp.zeros(a_ref.shape, F32)

    key_l = lax.broadcasted_iota(jnp.int32, (KEY_CHUNK, GQ), 0)
    q_l = lax.broadcasted_iota(jnp.int32, (KEY_CHUNK, GQ), 1) & (Q_TILE - 1)

    def step(pieces, m_ref, a_ref):
        shifts = [jnp.where(live, slope * ((c - j) * KEY_CHUNK).astype(F32), -jnp.inf) for _, c, live, _ in pieces]
        m_prev = m_ref[...]
        m_new = m_prev
        for (st, _, _, _), shift in zip(pieces, shifts):
            m_new = jnp.maximum(m_new, jnp.max(st, axis=0, keepdims=True) + shift)
        p = jnp.concatenate([jnp.exp2(st - (m_new - shift)).astype(BF16)
                             for (st, _, _, _), shift in zip(pieces, shifts)], axis=0)
        vo = jnp.concatenate([vt for _, _, _, vt in pieces], axis=1)
        a_ref[...] = jnp.exp2(m_prev - m_new) * a_ref[...] + _bdot(vo, p)
        m_ref[...] = m_new

    def sel_piece(c, live, diag=False):
        k0 = pl.multiple_of(c * KEY_CHUNK, KEY_CHUNK)
        st = _dot_nt(ksa_scr[pl.ds(k0, KEY_CHUNK), :], qaug_scr[...])
        if diag:
            st = jnp.where(key_l <= q_l, st, NEG)
        return st, c, live, vsa_scr[:, pl.ds(k0, KEY_CHUNK)]

    word0 = ((pl.program_id(0) * NSA_KV_HEADS + k) * nq + j) * _flag_words(nq)

    def scan(i, n):
        for u in range(SCAN_UNROLL):
            c = jnp.minimum(i * SCAN_UNROLL + u, nq - 1)
            bit = (flags_ref[word0 + c // 32] >> (c % 32)) & 1
            list_scr[n] = c
            n = n + jnp.where(i * SCAN_UNROLL + u < j, bit, 0)
        return n

    n_act = lax.fori_loop(0, (j + SCAN_UNROLL - 1) // SCAN_UNROLL, scan, 0)

    n_groups = (n_act + SEL_UNROLL - 1) // SEL_UNROLL

    def group_chunks(g):
        out = []
        for u in range(SEL_UNROLL):
            idx = g * SEL_UNROLL + u
            c = list_scr[jnp.minimum(idx, n_act - 1)]
            out.append((c, idx < n_act, pl.multiple_of(c * KEY_CHUNK, KEY_CHUNK)))
        return out

    def sel_scores(g, st_ref):
        for u, (_, _, k0) in enumerate(group_chunks(g)):
            st_ref[u] = _dot_nt(ksa_scr[pl.ds(k0, KEY_CHUNK), :], qaug_scr[...])

    def sel_update(g, st_ref):
        step([(st_ref[u], c, live, vsa_scr[:, pl.ds(k0, KEY_CHUNK)])
              for u, (c, live, k0) in enumerate(group_chunks(g))], ms_scr, as_scr)

    @pl.when(n_groups > 0)
    def _():
        sel_scores(0, sta_scr)

    def sel_body(i, carry):
        sel_scores(2 * i + 1, stb_scr)
        sel_update(2 * i, sta_scr)
        sel_scores(2 * i + 2, sta_scr)
        sel_update(2 * i + 1, stb_scr)
        return carry

    lax.fori_loop(0, (n_groups + 1) // 2, sel_body, 0)
    step([sel_piece(j, True, diag=True)], ms_scr, as_scr)

    n_win = NSA_WINDOW // KEY_CHUNK
    pieces = []
    for dc in range(n_win + 1):
        c = j - n_win + dc
        cc = jnp.maximum(c, 0)
        k0 = pl.multiple_of(cc * KEY_CHUNK, KEY_CHUNK)
        st = _dot_nt(kwa_scr[pl.ds(k0, KEY_CHUNK), :], qaug_scr[:, LANES:2 * LANES])
        if dc == 0:
            st = jnp.where(key_l > q_l, st, NEG)
        elif dc == n_win:
            st = jnp.where(key_l <= q_l, st, NEG)
        pieces.append((st, cc, c >= 0, vwa_scr[:, pl.ds(k0, KEY_CHUNK)]))
    step(pieces, mw_scr, aw_scr)

    own0 = pl.multiple_of(par * HEAD_DIM, HEAD_DIM)
    oth0 = pl.multiple_of((1 - par) * HEAD_DIM, HEAD_DIM)
    o_s = as_scr[pl.ds(own0, HEAD_DIM), :] / as_scr[pl.ds(oth0, 1), :]
    o_w = aw_scr[pl.ds(own0, HEAD_DIM), :] / aw_scr[pl.ds(oth0, 1), :]
    gates_t = gate_ref[...].T
    oc_t = oc_ref[...].T
    mix = []
    for g in range(NSA_GROUP):
        cols = slice(g * Q_TILE, (g + 1) * Q_TILE)
        mix.append(gates_t[3 * g:3 * g + 1] * oc_t[g * HEAD_DIM:(g + 1) * HEAD_DIM]
                   + gates_t[3 * g + 1:3 * g + 2] * o_s[:, cols] + gates_t[3 * g + 2:3 * g + 3] * o_w[:, cols])
    o_ref[...] = jnp.concatenate(mix, axis=0).T.astype(o_ref.dtype)


def _flag_words(nchunks):
    return -(-nchunks // 32)


def _bf16_pieces(x):
    def rnd(v):
        return np.asarray(v, np.float32).astype(jnp.bfloat16).astype(np.float32)
    a = rnd(x)
    b = rnd(x - a)
    return a, b, rnd(x - a - b)


def _nsa_fold_constants(t):
    posx = np.zeros((KEY_CHUNK, LANES), np.float32)
    loc = np.arange(KEY_CHUNK, dtype=np.float32)
    slopes = _alibi_slopes()
    qx = np.zeros((NSA_KV_HEADS, GQ, LANES), np.float32)
    ql = np.tile(np.arange(Q_TILE, dtype=np.float32), NSA_GROUP)
    for k in range(NSA_KV_HEADS):
        srow = np.repeat(slopes[k], Q_TILE)
        srow = (srow * np.float32(LOG2E)).astype(np.float32)
        pieces = _bf16_pieces(srow) + _bf16_pieces(-srow * ql)
        for base in (0, HEAD_DIM):
            for i, pc in enumerate(pieces):
                qx[k, :, base + i] = pc
    for base in (0, HEAD_DIM):
        posx[:, base:base + 3] = loc[:, None]
        posx[:, base + 3:base + 6] = 1.0
    onehot = (np.arange(t)[:, None] // NSA_BLOCK == np.arange(LANES)[None, :]).astype(np.float32)
    return jnp.asarray(posx, BF16), jnp.asarray(onehot, BF16), jnp.asarray(qx, BF16)


def _pack_chunk_flags(any_sel, nchunks):
    blocks_per_chunk = KEY_CHUNK // NSA_BLOCK
    f = any_sel[:, :, :, 0, :nchunks * blocks_per_chunk]
    f = f.reshape(f.shape[:3] + (nchunks, blocks_per_chunk)).max(axis=-1) > 0.0
    words = _flag_words(nchunks)
    f = jnp.pad(f, ((0, 0), (0, 0), (0, 0), (0, words * 32 - nchunks)))
    bits = f.reshape(f.shape[:3] + (words, 32)).astype(jnp.uint32) << jnp.arange(32, dtype=jnp.uint32)
    return lax.bitcast_convert_type(bits.sum(axis=-1, dtype=jnp.uint32), jnp.int32).reshape(-1)


def _nsa_selected_window(b, t, q, selb, any_sel, kvs, kvs_t, kvw, kvw_t, oc, gates):
    assert Q_TILE == KEY_CHUNK
    nq = t // Q_TILE
    slopes2 = (_alibi_slopes() * np.float32(LOG2E)).astype(np.float32)
    slopes = jnp.asarray(np.repeat(slopes2, Q_TILE, axis=1).reshape(NSA_KV_HEADS, 1, GQ))
    posx, onehot, qx = _nsa_fold_constants(t)
    flags = _pack_chunk_flags(any_sel, nq)
    pairs = KVW // LANES
    qspec = pl.BlockSpec((Q_TILE, KVW), lambda bi, k, j, fl: (bi * nq + j, k))
    kspec = pl.BlockSpec((None, t, LANES), lambda bi, k, j, fl: (bi, 0, k // 2))
    vspec = pl.BlockSpec((None, LANES, t), lambda bi, k, j, fl: (bi, pairs + k // 2, 0))
    const = lambda shape: pl.BlockSpec(shape, lambda bi, k, j, fl: (0,) * len(shape))
    return pl.pallas_call(
        functools.partial(_nsa_sw_kernel, nq),
        grid_spec=pltpu.PrefetchScalarGridSpec(
            num_scalar_prefetch=1,
            grid=(b, NSA_KV_HEADS, nq),
            in_specs=[qspec,
                      pl.BlockSpec((None, None, Q_TILE, LANES), lambda bi, k, j, fl: (bi, k, j, 0)),
                      kspec, vspec, kspec, vspec,
                      const((KEY_CHUNK, LANES)), const((t, LANES)),
                      pl.BlockSpec((None, GQ, LANES), lambda bi, k, j, fl: (k, 0, 0)),
                      qspec,
                      pl.BlockSpec((Q_TILE, LANES), lambda bi, k, j, fl: (bi * nq + j, k)),
                      pl.BlockSpec((None, 1, GQ), lambda bi, k, j, fl: (k, 0, 0))],
            out_specs=qspec,
            scratch_shapes=[pltpu.VMEM((t, 2 * LANES), BF16), pltpu.VMEM((t, LANES), BF16),
                            pltpu.VMEM((GQ, 2 * LANES), BF16),
                            pltpu.VMEM((1, GQ), F32), pltpu.VMEM((LANES, GQ), F32),
                            pltpu.VMEM((1, GQ), F32), pltpu.VMEM((LANES, GQ), F32),
                            pltpu.VMEM((SEL_UNROLL, KEY_CHUNK, GQ), F32), pltpu.VMEM((SEL_UNROLL, KEY_CHUNK, GQ), F32),
                            pltpu.VMEM((LANES, t), BF16), pltpu.VMEM((LANES, t), BF16),
                            pltpu.SMEM((nq,), jnp.int32)]),
        out_shape=jax.ShapeDtypeStruct((b * t, D_MODEL), BF16),
        compiler_params=_cparams("parallel", "arbitrary", "arbitrary"),
        name="nsa_selected_window",
    )(flags, q, selb, kvs, kvs_t, kvw, kvw_t, posx, onehot, qx, oc, gates, slopes)


def _nsa_weights(w_in):
    w = w_in.astype(BF16)
    ng = NSA_GROUP * 3
    gcols = [jnp.pad(w[:, NSA_G0 + k * ng:NSA_G0 + (k + 1) * ng], ((0, 0), (0, LANES - ng)))
             for k in range(NSA_KV_HEADS)]
    return jnp.concatenate([w[:, :NSA_G0]] + gcols, axis=1)


def _nsa_proj_outs(kv_dtypes, exp2_queries=False):
    return ((NSA_Q0, NSA_C0, "qscales", (BF16, BF16)) if exp2_queries else (NSA_Q0, NSA_C0, "scale", (BF16,)),
            (NSA_C0, NSA_S0, "", kv_dtypes), (NSA_S0, NSA_W0, "", kv_dtypes), (NSA_W0, NSA_G0, "", kv_dtypes),
            (NSA_G0, NSA_COLS, "sigmoid", (F32,)))


NO_BIAS = np.zeros((1, LANES), np.float32)


def _leaf_from_feat_major(a, feat_shape):
    return jnp.moveaxis(a.reshape(a.shape[:1] + tuple(feat_shape) + a.shape[2:]), -1, 1)


def _compress_blocks(kvc, nseq, nb, w_ck, w_cv):
    blocks = kvc.reshape(nseq, nb, NSA_BLOCK, 2, NSA_KV_HEADS, HEAD_DIM).transpose(3, 0, 1, 4, 2, 5)
    blocks = blocks.reshape(2, nseq * nb * NSA_KV_HEADS, NSA_BLOCK * HEAD_DIM)
    tm = min(256, blocks.shape[1])
    out = []
    for a, w in ((blocks[0], w_ck), (blocks[1], w_cv)):
        c = _matmul(a, w.astype(BF16), tm).reshape(nseq, nb, NSA_KV_HEADS, HEAD_DIM).transpose(0, 2, 1, 3)
        out.append(jnp.concatenate([c, c], axis=-1).astype(BF16))
    return out


def _nsa_prompt(rows, x, g, mod, w_in, w_ck, w_cv):
    b, t = rows.nseq, rows.t
    q, q2, kvc_t, kvc_b, kvs_t, kvs_b, kvw_t, kvw_b, gates = _norm_proj(
        rows, x, g, mod, (0, 1), _nsa_weights(w_in), jnp.asarray(NO_BIAS), _nsa_proj_outs((FEAT_MAJOR, BF16), True))
    kcc, vcc = _compress_blocks(kvc_b, b, t // NSA_BLOCK, w_ck, w_cv)
    oc, selb, any_sel = _nsa_compressed(b, t, q, kcc, vcc)
    o = _nsa_selected_window(b, t, q2, selb, any_sel, kvs_b.reshape(b, t, 2 * KVW), kvs_t,
                             kvw_b.reshape(b, t, 2 * KVW), kvw_t, oc, gates)
    return o, kvc_t, kvs_t, kvw_t


FOX_TQ = 512
FOX_TK = 512
CUM_TILE = 256
FOX_PAIRS = FOX_HEADS // 2

def _fox_proj_outs(kv_dtypes, q_kind):
    return ((0, D_MODEL, q_kind, (BF16,)), (D_MODEL, FOX_LF0, "", kv_dtypes), (FOX_LF0, FOX_COLS, "logsigmoid", (F32,)))


def _fox_weights(w_in, b_f):
    w = jnp.pad(w_in.astype(BF16), ((0, 0), (0, FOX_COLS - w_in.shape[1])))
    bias = jnp.pad(b_f.astype(F32), (0, LANES - FOX_HEADS)).reshape(1, LANES)
    return w, bias


def _tri_cumsum(x):
    n = x.shape[0]
    tri = (lax.broadcasted_iota(jnp.int32, (n, n), 1) <= lax.broadcasted_iota(jnp.int32, (n, n), 0)).astype(BF16)
    a, b, c = _split3(x)
    return _bdot(tri, a) + _bdot(tri, b) + _bdot(tri, c)


def _cumsum_kernel(x_ref, o_ref, carry_scr):
    @pl.when(pl.program_id(1) == 0)
    def _():
        carry_scr[...] = jnp.zeros_like(carry_scr)

    cum = _tri_cumsum(x_ref[...]) + carry_scr[0:1, :]
    o_ref[...] = cum
    carry_scr[...] = jnp.broadcast_to(cum[-1:, :], carry_scr.shape)


def _cumsum_rows(x, nseq, t):
    nt = t // CUM_TILE
    return pl.pallas_call(
        _cumsum_kernel,
        grid=(nseq, nt),
        in_specs=[pl.BlockSpec((CUM_TILE, LANES), lambda s, i: (s * nt + i, 0))],
        out_specs=pl.BlockSpec((CUM_TILE, LANES), lambda s, i: (s * nt + i, 0)),
        out_shape=jax.ShapeDtypeStruct(x.shape, F32),
        scratch_shapes=[pltpu.VMEM((SUBLANES, LANES), F32)],
        compiler_params=_cparams("parallel", "arbitrary"),
        name="cumsum_rows",
    )(x)


def _fox_kernel(nk, q_ref, k_ref, vt_ref, ccol_ref, crow_ref, o_ref, ka_scr, base_scr, qa_scr, m_scr, acc_scr,
                sta_scr, stb_scr, va_scr):
    qi = pl.program_id(2)
    tq = q_ref.shape[0]
    lane = lax.broadcasted_iota(jnp.int32, (FOX_TK, LANES), 1)
    row = lax.broadcasted_iota(jnp.int32, (LANES, FOX_TK), 0)

    def own(e):
        return (lane // HEAD_DIM) == e

    def other_lane(e, i):
        return lane == (1 - e) * HEAD_DIM + i

    @pl.when(qi == 0)
    def _():
        def fill(c, carry):
            k0 = pl.multiple_of(c * FOX_TK, FOX_TK)
            kc = k_ref[pl.ds(k0, FOX_TK), :].astype(F32)
            for e in range(2):
                col = ccol_ref[pl.ds(k0, FOX_TK), e:e + 1]
                base = col[0:1, :]
                ext = jnp.zeros((FOX_TK, LANES), F32)
                for i, piece in enumerate(_split3((base - col) * LOG2E)):
                    ext = jnp.where(other_lane(e, i), piece.astype(F32), ext)
                ka_scr[e, pl.ds(k0, FOX_TK), :] = jnp.where(own(e), kc, ext).astype(BF16)
                base_scr[e, pl.ds(c, 1), :] = jnp.broadcast_to(base, (1, LANES))
                va_scr[e, :, pl.ds(k0, FOX_TK)] = jnp.where((row // HEAD_DIM) == e, vt_ref[:, pl.ds(k0, FOX_TK)],
                                                            1.0).astype(BF16)
            return carry

        lax.fori_loop(0, nk, fill, 0)

    q = q_ref[...].astype(F32)
    ones3 = jnp.zeros((tq, LANES), F32)
    for e in range(2):
        ext = ones3
        for i in range(3):
            ext = jnp.where(other_lane(e, i), 1.0, ext)
        qa_scr[e] = jnp.where(own(e), q, ext).astype(BF16)
    m_scr[...] = jnp.full(m_scr.shape, M_INIT, F32)
    acc_scr[...] = jnp.zeros(acc_scr.shape, F32)
    q0 = pl.multiple_of(qi * tq, tq)

    def scores(c, st_ref):
        k0 = pl.multiple_of(c * FOX_TK, FOX_TK)
        for e in range(2):
            st_ref[e] = _dot_nt(ka_scr[e, pl.ds(k0, FOX_TK), :], qa_scr[e])

    def update(c, live, st_ref, diag=False):
        k0 = pl.multiple_of(c * FOX_TK, FOX_TK)
        for e in range(2):
            st = st_ref[e]
            if diag:
                st = jnp.where(lax.broadcasted_iota(jnp.int32, st.shape, 0)
                               <= lax.broadcasted_iota(jnp.int32, st.shape, 1), st, NEG)
            base = base_scr[e, pl.ds(c, 1), :]
            shift = (crow_ref[e:e + 1, pl.ds(q0, tq)] - jnp.concatenate([base] * (tq // LANES), axis=1)) * LOG2E
            shift = jnp.where(live, shift, -jnp.inf)
            m_prev = m_scr[e]
            m_new = jnp.maximum(m_prev, jnp.max(st, axis=0, keepdims=True) + shift)
            p = jnp.exp2(st - (m_new - shift)).astype(BF16)
            acc_scr[e] = jnp.exp2(m_prev - m_new) * acc_scr[e] + _bdot(va_scr[e, :, pl.ds(k0, FOX_TK)], p)
            m_scr[e] = m_new

    scores(0, sta_scr)

    def body(i, carry):
        c = 2 * i
        scores(jnp.minimum(c + 1, qi), stb_scr)
        update(c, True, sta_scr)
        scores(jnp.minimum(c + 2, qi), sta_scr)
        update(jnp.minimum(c + 1, qi), c + 1 < qi, stb_scr)
        return carry

    lax.fori_loop(0, (qi + 1) // 2, body, 0)
    update(qi, True, sta_scr, diag=True)
    a0, a1 = acc_scr[0], acc_scr[1]
    r = lax.broadcasted_iota(jnp.int32, a0.shape, 0)
    o_t = jnp.where(r < HEAD_DIM, a0 / a0[HEAD_DIM:HEAD_DIM + 1], a1 / a1[0:1])
    o_ref[...] = o_t.T.astype(o_ref.dtype)


def _fox_attention(b, t, q, k, kv_t, cum_col, cum_row):
    assert FOX_TQ == FOX_TK and t % FOX_TQ == 0
    tq = FOX_TQ
    nq = t // tq
    vrow0 = D_MODEL // LANES
    return pl.pallas_call(
        functools.partial(_fox_kernel, nq),
        grid=(b, FOX_PAIRS, nq),
        in_specs=[pl.BlockSpec((tq, LANES), lambda bi, hp, qi: (bi * nq + qi, hp)),
                  pl.BlockSpec((None, t, LANES), lambda bi, hp, qi: (bi, 0, hp)),
                  pl.BlockSpec((None, LANES, t), lambda bi, hp, qi: (bi, vrow0 + hp, 0)),
                  pl.BlockSpec((None, None, t, 2), lambda bi, hp, qi: (bi, hp, 0, 0)),
                  pl.BlockSpec((None, None, 2, t), lambda bi, hp, qi: (bi, hp, 0, 0))],
        out_specs=pl.BlockSpec((tq, LANES), lambda bi, hp, qi: (bi * nq + qi, hp)),
        out_shape=jax.ShapeDtypeStruct((b * t, D_MODEL), BF16),
        scratch_shapes=[pltpu.VMEM((2, t, LANES), BF16), pltpu.VMEM((2, max(nq, SUBLANES), LANES), F32),
                        pltpu.VMEM((2, tq, LANES), BF16), pltpu.VMEM((2, 1, tq), F32),
                        pltpu.VMEM((2, LANES, tq), F32),
                        pltpu.VMEM((2, FOX_TK, tq), F32), pltpu.VMEM((2, FOX_TK, tq), F32),
                        pltpu.VMEM((2, LANES, t), BF16)],
        compiler_params=_cparams("parallel", "arbitrary", "arbitrary"),
        name="fox_attention",
    )(q, k, kv_t, cum_col, cum_row)


def _fox_prompt(rows, x, g, mod, w_in, b_f):
    b, t = rows.nseq, rows.t
    w, bias = _fox_weights(w_in, b_f)
    q, kv_t, kv_b, logf = _norm_proj(rows, x, g, mod, (0, 1), w, bias, _fox_proj_outs((FEAT_MAJOR, BF16), "scale2"))
    cum = _cumsum_rows(logf, b, t)[:, :FOX_HEADS].reshape(b, t, FOX_PAIRS, 2)
    o = _fox_attention(b, t, q, kv_b.reshape(b, t, 2 * D_MODEL), kv_t,
                       cum.transpose(0, 2, 1, 3), cum.transpose(0, 2, 3, 1))
    return o, kv_t, logf[:, :FOX_HEADS]


def _pool_kernel(tps, pos0, h_ref, w_ref, b_ref, scale_ref, o_ref, ext_scr):
    i = pl.program_id(0)
    tm = h_ref.shape[0]
    halo = 2 * SUBLANES
    first = (i % tps) == 0

    @pl.when(first)
    def _():
        ext_scr[0:halo, :] = jnp.zeros((halo, D_MODEL), F32)

    @pl.when(jnp.logical_not(first))
    def _():
        ext_scr[0:halo, :] = ext_scr[tm:tm + halo, :]

    ext_scr[halo:halo + tm, :] = h_ref[...]
    pos = pos0 + (i % tps) * tm + lax.broadcasted_iota(jnp.int32, (tm, 1), 0)
    for gi, w in enumerate(POOL_WINDOWS):
        c0, c1 = gi * POOL_GROUP_DIM, (gi + 1) * POOL_GROUP_DIM
        win = ext_scr[halo:halo + tm, c0:c1]
        for back in range(1, w):
            win = win + ext_scr[halo - back:halo - back + tm, c0:c1]
        cnt = jnp.minimum(w, pos + 1).astype(F32)
        mixed = win / cnt - h_ref[:, c0:c1]
        y = _bdot(mixed.astype(BF16), w_ref[gi]) + b_ref[:, c0:c1]
        o_ref[:, c0:c1] = y * scale_ref[:, c0:c1]


def _pool_mix(h_ext, nseq, t, tm, pos0, w_g, b_g, scale):
    assert POOL_STATE < 2 * SUBLANES
    tps = t // tm
    full = lambda shape: pl.BlockSpec(shape, lambda i: (0,) * len(shape))
    return pl.pallas_call(
        functools.partial(_pool_kernel, tps, pos0),
        grid=(nseq * tps,),
        in_specs=[pl.BlockSpec((tm, D_MODEL), lambda i: (i, 0)),
                  full((len(POOL_WINDOWS), POOL_GROUP_DIM, POOL_GROUP_DIM)), full((1, D_MODEL)), full((1, D_MODEL))],
        out_specs=pl.BlockSpec((tm, D_MODEL), lambda i: (i, 0)),
        out_shape=jax.ShapeDtypeStruct((nseq * t, D_MODEL), F32),
        scratch_shapes=[pltpu.VMEM((tm + 2 * SUBLANES, D_MODEL), F32)],
        compiler_params=_cparams("arbitrary"),
        name="pool_mix",
    )(h_ext, w_g.astype(BF16), b_g.reshape(1, D_MODEL), scale.reshape(1, D_MODEL))


def _residual_kernel(o_ref, x_ref, g_ref, gate_ref, y_ref):
    y_ref[...] = x_ref[...] + gate_ref[...] * _rms_rows(o_ref[...], g_ref[...])


def _gated_residual(rows, o, x, g, mod, gate_chunk):
    full = lambda shape: pl.BlockSpec(shape, lambda i: (0,) * len(shape))
    return pl.pallas_call(
        _residual_kernel,
        grid=(rows.ntiles,),
        in_specs=[rows.row_spec(D_MODEL), rows.row_spec(D_MODEL), full((1, D_MODEL)), rows.mod_spec(gate_chunk)],
        out_specs=rows.row_spec(D_MODEL),
        out_shape=jax.ShapeDtypeStruct((rows.m, D_MODEL), F32),
        compiler_params=_cparams("parallel"),
        name="gated_residual",
    )(o, x, g.reshape(1, D_MODEL), rows.mod_arr(mod))


H_ONLY = ((0, D_MODEL, "h", (F32,)),)


def _norm_only(rows, x, g, mod, chunks):
    dummy = jnp.zeros((D_MODEL, LANES), BF16)
    return _norm_proj(rows, x, g, mod, chunks, dummy, jnp.asarray(NO_BIAS), H_ONLY)[0]


DEC_B = 8
NEW_ROWS = SUBLANES


def _paged_grid_spec(grid, in_specs, out_specs, scratch_shapes):
    return pltpu.PrefetchScalarGridSpec(num_scalar_prefetch=1, grid=grid, in_specs=in_specs,
                                        out_specs=out_specs, scratch_shapes=scratch_shapes)


def _feat_major_pages(cache):
    n, p = cache.shape[:2]
    return jnp.moveaxis(cache, 1, -1).reshape(n, -1, p)


def _page_specs(npg, rows):
    return [pl.BlockSpec((None, rows, PAGE_SIZE), functools.partial(lambda b, pt, i: (pt[b * npg + i], 0, 0), i=i))
            for i in range(npg)]


def _cmp_dec_kernel(npg, pt_ref, *refs):
    pages, (w_ref, o_ref, lhs_scr) = refs[:npg], refs[npg:]
    groups = 2 * NSA_KV_HEADS
    for d in range(HEAD_DIM):
        for i, page in enumerate(pages):
            lhs_scr[i * groups:(i + 1) * groups, d * PAGE_SIZE:(d + 1) * PAGE_SIZE] = page[pl.ds(d, groups, stride=HEAD_DIM), :]
    both = _bdot(lhs_scr[...].astype(BF16), w_ref[...])
    half = o_ref.shape[1]
    is_key = (lax.broadcasted_iota(jnp.int32, o_ref.shape, 0) % groups) < NSA_KV_HEADS
    o_ref[...] = jnp.where(is_key, both[:, 0:half], both[:, half:2 * half]).astype(o_ref.dtype)


def _cmp_dec_weight(w):
    w3 = w.astype(BF16).reshape(NSA_BLOCK, HEAD_DIM, HEAD_DIM)
    halves = PAGE_SIZE // NSA_BLOCK
    eye = jnp.eye(halves, dtype=BF16)
    return jnp.einsum("rde,hg->dhrge", w3, eye).reshape(HEAD_DIM * PAGE_SIZE, halves * HEAD_DIM)


def _cmp_decode(page_table, cache_t, w_ck, w_cv):
    ns, npg = page_table.shape
    halves = PAGE_SIZE // NSA_BLOCK
    groups = 2 * NSA_KV_HEADS
    wspec = pl.BlockSpec((HEAD_DIM * PAGE_SIZE, 2 * halves * HEAD_DIM), lambda b, pt: (0, 0))
    w_both = jnp.concatenate([_cmp_dec_weight(w_ck), _cmp_dec_weight(w_cv)], axis=1)
    out = pl.pallas_call(
        functools.partial(_cmp_dec_kernel, npg),
        grid_spec=_paged_grid_spec(
            (ns,), _page_specs(npg, 2 * KVW) + [wspec],
            pl.BlockSpec((None, npg * groups, halves * HEAD_DIM), lambda b, pt: (b, 0, 0)),
            [pltpu.VMEM((npg * groups, HEAD_DIM * PAGE_SIZE), F32)]),
        out_shape=jax.ShapeDtypeStruct((ns, npg * groups, halves * HEAD_DIM), BF16),
        compiler_params=_cparams("parallel"),
        name="nsa_decode_compress",
    )(page_table.reshape(-1), *([cache_t] * npg), w_both)
    out = out.reshape(ns, npg, 2, NSA_KV_HEADS, halves, HEAD_DIM).transpose(0, 1, 4, 2, 3, 5)
    return out.reshape(ns, npg * halves, 2 * KVW)


def _dec_select_kernel(qpos, nblk, qbd_ref, kcv_ref, slope_ref, oc_ref, selb_ref, imp_scr):
    lane_b = lax.broadcasted_iota(jnp.int32, (N_HEADS, nblk), 1)
    ends = lane_b * NSA_BLOCK + (NSA_BLOCK - 1)
    mask = ends <= qpos
    bias = slope_ref[...] * (qpos - ends).astype(F32)
    imp_scr[...] = jnp.zeros(imp_scr.shape, F32)
    for bi in range(DEC_B):
        s = _dot_nt(qbd_ref[bi], kcv_ref[bi, :, 0:KVW]) - bias
        s = jnp.where(mask, s, NEG)
        e = jnp.exp(s - jnp.max(s, axis=-1, keepdims=True))
        p = jnp.where(mask, e / jnp.sum(e, axis=-1, keepdims=True), 0.0)
        oc_ref[bi] = _bdot(p.astype(BF16), kcv_ref[bi, :, KVW:2 * KVW])
        imp = p[0:NSA_KV_HEADS]
        for g in range(1, NSA_GROUP):
            imp = imp + p[g * NSA_KV_HEADS:(g + 1) * NSA_KV_HEADS]
        imp_scr[bi * NSA_KV_HEADS:(bi + 1) * NSA_KV_HEADS, 0:nblk] = imp

    rows = DEC_B * NSA_KV_HEADS
    n2 = lax.broadcasted_iota(jnp.int32, (rows, LANES), 1)
    cur = qpos // NSA_BLOCK
    forced = (n2 == 0) | (n2 == cur) | (n2 == cur - 1)
    score = jnp.where(forced, BIG, jnp.where(n2 <= cur, imp_scr[...], NEG))
    score = jnp.where(n2 <= cur, score, -jnp.inf)

    def pick_next(_, carry):
        sc, sel = carry
        m = jnp.max(sc, axis=-1, keepdims=True)
        first = jnp.min(jnp.where(sc == m, n2, LANES), axis=-1, keepdims=True)
        pick = n2 == first
        sel = jnp.where(pick & (m > 0.5 * NEG), 1.0, sel)
        return jnp.where(pick, -jnp.inf, sc), sel

    _, sel = lax.fori_loop(0, min(NSA_TOPN, cur + 1), pick_next, (score, jnp.zeros((rows, LANES), F32)), unroll=True)
    selb = jnp.where(sel > 0.0, 0.0, NEG).astype(BF16)
    for bi in range(DEC_B):
        one = selb[bi * NSA_KV_HEADS:(bi + 1) * NSA_KV_HEADS]
        selb_ref[bi] = jnp.concatenate([one] * NSA_GROUP, axis=0)


def _dec_slopes():
    return jnp.asarray(_alibi_slopes().T.reshape(N_HEADS, 1))


def _nsa_decode_select(qbd, kcv, qpos):
    ns, nblk = kcv.shape[0], kcv.shape[1]
    assert qpos // NSA_BLOCK < LANES and ns % DEC_B == 0
    blk = lambda shape: pl.BlockSpec((DEC_B,) + shape, lambda i: (i, 0, 0))
    return pl.pallas_call(
        functools.partial(_dec_select_kernel, qpos, nblk),
        grid=(ns // DEC_B,),
        in_specs=[blk((N_HEADS, KVW)), blk((nblk, 2 * KVW)), pl.BlockSpec((N_HEADS, 1), lambda i: (0, 0))],
        out_specs=[blk((N_HEADS, KVW)), blk((N_HEADS, LANES))],
        out_shape=[jax.ShapeDtypeStruct((ns, N_HEADS, KVW), F32), jax.ShapeDtypeStruct((ns, N_HEADS, LANES), BF16)],
        scratch_shapes=[pltpu.VMEM((DEC_B * NSA_KV_HEADS, LANES), F32)],
        compiler_params=_cparams("parallel"),
        name="nsa_decode_select",
    )(qbd, kcv, _dec_slopes())


def _new_key_tile(row):
    r = lax.broadcasted_iota(jnp.int32, (NEW_ROWS, row.shape[1]), 0)
    return jnp.where(r == 0, jnp.broadcast_to(row, (NEW_ROWS, row.shape[1])), 0.0).astype(BF16)


def _new_key_mask(s):
    return jnp.where(lax.broadcasted_iota(jnp.int32, s.shape, 1) == 0, s, NEG)


def _softmax_with_new_key(s, s_new):
    m = jnp.maximum(jnp.max(s, axis=-1, keepdims=True), jnp.max(s_new, axis=-1, keepdims=True))
    p, p_new = jnp.exp(s - m), jnp.exp(s_new - m)
    return p, p_new, jnp.sum(p, axis=-1, keepdims=True) + jnp.sum(p_new, axis=-1, keepdims=True)


def _dec_attend_kernel(npg, qpos, wlen, pt_ref, *refs):
    pages = refs[:npg]
    (qbd_ref, selb_ref, win_ref, snew_ref, wnew_ref, wcol_ref, oc_ref, gate_ref, slope_ref,
     o_ref, wout_ref) = refs[npg:]
    slope = slope_ref[...]
    qbd = qbd_ref[...]

    qaug = jnp.concatenate([qbd, selb_ref[...]], axis=1)
    blk_row = lax.broadcasted_iota(jnp.int32, (LANES, PAGE_SIZE), 0)
    blk_of_lane = lax.broadcasted_iota(jnp.int32, (LANES, PAGE_SIZE), 1) // NSA_BLOCK
    scores = []
    for i, page in enumerate(pages):
        onehot_t = (blk_row == (PAGE_SIZE // NSA_BLOCK) * i + blk_of_lane).astype(BF16)
        kaug_t = jnp.concatenate([page[0:KVW, :].astype(BF16), onehot_t], axis=0)
        scores.append(_bdot(qaug, kaug_t))
    kpos = lax.broadcasted_iota(jnp.int32, (N_HEADS, npg * PAGE_SIZE), 1)
    s = jnp.concatenate(scores, axis=1) - slope * (qpos - kpos).astype(F32)
    snew = snew_ref[...]
    s_new = _new_key_mask(_dot_nt(qbd, _new_key_tile(snew[:, 0:KVW])))
    p, p_new, l = _softmax_with_new_key(s, s_new)
    p = p.astype(BF16)
    acc = _bdot(p_new.astype(BF16), _new_key_tile(snew[:, KVW:2 * KVW]))
    for i, page in enumerate(pages):
        acc = acc + _dot_nt(p[:, i * PAGE_SIZE:(i + 1) * PAGE_SIZE], page[KVW:2 * KVW, :].astype(BF16))
    o_s = acc / l

    win = win_ref[...]
    wpos = qpos - wlen + lax.broadcasted_iota(jnp.int32, (N_HEADS, wlen), 1)
    s = _bdot(qbd, win[0:KVW, :].astype(BF16)) - slope * (qpos - wpos).astype(F32)
    s = jnp.where(wpos > qpos - NSA_WINDOW, s, NEG)
    wnew = wnew_ref[...]
    s_new = _new_key_mask(_dot_nt(qbd, _new_key_tile(wnew[:, 0:KVW])))
    p, p_new, l = _softmax_with_new_key(s, s_new)
    o_w = (_dot_nt(p.astype(BF16), win[KVW:2 * KVW, :].astype(BF16))
           + _bdot(p_new.astype(BF16), _new_key_tile(wnew[:, KVW:2 * KVW]))) / l

    gates = gate_ref[...]
    o_ref[...] = gates[:, 0:1] * oc_ref[...] + gates[:, 1:2] * o_s + gates[:, 2:3] * o_w
    lane = lax.broadcasted_iota(jnp.int32, win.shape, 1)
    wout_ref[...] = jnp.where(lane == wlen - 1, wcol_ref[...], pltpu.roll(win, wlen - 1, axis=1))


def _nsa_decode_attend(page_table, qbd, selb, cache_t, win_t, kvs_new, kvw_new, oc, gates, qpos):
    ns, npg = page_table.shape
    wlen = win_t.shape[2]
    assert qpos == npg * PAGE_SIZE
    per_b = lambda shape: pl.BlockSpec((None,) + shape, lambda b, pt: (b, 0, 0))
    return pl.pallas_call(
        functools.partial(_dec_attend_kernel, npg, qpos, wlen),
        grid_spec=_paged_grid_spec(
            (ns,),
            _page_specs(npg, 2 * KVW)
            + [per_b((N_HEADS, KVW)), per_b((N_HEADS, LANES)), per_b((2 * KVW, wlen)), per_b((1, 2 * KVW)),
               per_b((1, 2 * KVW)), per_b((2 * KVW, 1)), per_b((N_HEADS, KVW)), per_b((N_HEADS, LANES)),
               pl.BlockSpec((N_HEADS, 1), lambda b, pt: (0, 0))],
            [per_b((N_HEADS, KVW)), per_b((2 * KVW, wlen))],
            []),
        out_shape=[jax.ShapeDtypeStruct((ns, N_HEADS, KVW), F32), jax.ShapeDtypeStruct(win_t.shape, F32)],
        compiler_params=_cparams("parallel"),
        name="nsa_decode_attend",
    )(page_table.reshape(-1), *([cache_t] * npg), qbd, selb, win_t, kvs_new.reshape(ns, 1, 2 * KVW),
      kvw_new.reshape(ns, 1, 2 * KVW), kvw_new.reshape(ns, 2 * KVW, 1), oc, gates, _dec_slopes())


def _nsa_sample(rows, x, g, mod, w_in, w_ck, w_cv, cache_c, cache_s, win_buf, page_table):
    ns = rows.nseq
    qpos = page_table.shape[1] * PAGE_SIZE
    q, kvc, kvs, kvw, gates = _norm_proj(
        rows, x, g, mod, (0, 1), _nsa_weights(w_in), jnp.asarray(NO_BIAS), _nsa_proj_outs((F32,)))
    kcv = _cmp_decode(page_table, _feat_major_pages(cache_c), w_ck, w_cv)
    q4 = q.reshape(ns, NSA_KV_HEADS, NSA_GROUP, HEAD_DIM)
    eye = jnp.eye(NSA_KV_HEADS, dtype=q.dtype)
    qbd = jnp.einsum("bkgd,kj->bgkjd", q4, eye).reshape(ns, N_HEADS, KVW)
    oc, selb = _nsa_decode_select(qbd, kcv, qpos)
    g4 = gates.reshape(ns, NSA_KV_HEADS, LANES)[:, :, :NSA_GROUP * 3].reshape(ns, NSA_KV_HEADS, NSA_GROUP, 3)
    g_rows = jnp.pad(g4.transpose(0, 2, 1, 3).reshape(ns, N_HEADS, 3), ((0, 0), (0, 0), (0, LANES - 3)))
    o_bd, win_out = _nsa_decode_attend(
        page_table, qbd, selb, _feat_major_pages(cache_s), _feat_major_pages(win_buf), kvs, kvw, oc, g_rows, qpos)
    o5 = o_bd.reshape(ns, NSA_GROUP, NSA_KV_HEADS, NSA_KV_HEADS, HEAD_DIM)
    o = jnp.einsum("bgkkd->bkgd", o5).reshape(ns, D_MODEL).astype(BF16)
    win_out = jnp.moveaxis(win_out.reshape(win_buf.shape[:1] + win_buf.shape[2:] + win_buf.shape[1:2]), -1, 1)
    return o, kvc, kvs, win_out


def _fox_dec_kernel(npg, pt_ref, *refs):
    kv_pages, lf_pages = refs[:npg], refs[npg:2 * npg]
    q_ref, kvn_ref, lfn_ref, o_ref = refs[2 * npg:]
    own = (lax.broadcasted_iota(jnp.int32, (FOX_HEADS, D_MODEL), 1) // HEAD_DIM
           == lax.broadcasted_iota(jnp.int32, (FOX_HEADS, D_MODEL), 0))
    q = jnp.broadcast_to(q_ref[...].astype(F32), (FOX_HEADS, D_MODEL))
    qbd = jnp.where(own, q, 0.0).astype(BF16)
    upper = (lax.broadcasted_iota(jnp.int32, (PAGE_SIZE, PAGE_SIZE), 0)
             <= lax.broadcasted_iota(jnp.int32, (PAGE_SIZE, PAGE_SIZE), 1)).astype(BF16)
    carry = jnp.zeros((FOX_HEADS, 1), F32)
    cums, scores = [], []
    for kv_page, lf_page in zip(kv_pages, lf_pages):
        a, b, c = _split3(lf_page[...])
        cum = carry + (_bdot(a, upper) + _bdot(b, upper) + _bdot(c, upper))
        carry = cum[:, PAGE_SIZE - 1:PAGE_SIZE]
        cums.append(cum)
        scores.append(_bdot(qbd, kv_page[0:D_MODEL, :].astype(BF16)))
    cum_new = carry + lfn_ref[...]
    s = jnp.concatenate(scores, axis=1) + (cum_new - jnp.concatenate(cums, axis=1))
    kvn = kvn_ref[...]
    s_new = _new_key_mask(_dot_nt(qbd, _new_key_tile(kvn[:, 0:D_MODEL])))
    p, p_new, l = _softmax_with_new_key(s, s_new)
    p = p.astype(BF16)
    acc = _bdot(p_new.astype(BF16), _new_key_tile(kvn[:, D_MODEL:2 * D_MODEL]))
    for i, kv_page in enumerate(kv_pages):
        acc = acc + _dot_nt(p[:, i * PAGE_SIZE:(i + 1) * PAGE_SIZE], kv_page[D_MODEL:2 * D_MODEL, :].astype(BF16))
    o_ref[...] = jnp.sum(jnp.where(own, acc / l, 0.0), axis=0, keepdims=True).astype(o_ref.dtype)


def _fox_decode(page_table, q, cache_kv_t, cache_logf_t, kv_new, logf_new):
    ns, npg = page_table.shape
    per_b = lambda shape: pl.BlockSpec((None,) + shape, lambda b, pt: (b, 0, 0))
    out = pl.pallas_call(
        functools.partial(_fox_dec_kernel, npg),
        grid_spec=_paged_grid_spec(
            (ns,),
            _page_specs(npg, 2 * D_MODEL) + _page_specs(npg, FOX_HEADS)
            + [per_b((1, D_MODEL)), per_b((1, 2 * D_MODEL)), per_b((FOX_HEADS, 1))],
            per_b((1, D_MODEL)), []),
        out_shape=jax.ShapeDtypeStruct((ns, 1, D_MODEL), BF16),
        compiler_params=_cparams("parallel"),
        name="fox_decode",
    )(page_table.reshape(-1), *([cache_kv_t] * npg), *([cache_logf_t] * npg), q.reshape(ns, 1, D_MODEL),
      kv_new.reshape(ns, 1, 2 * D_MODEL), logf_new.reshape(ns, FOX_HEADS, 1))
    return out.reshape(ns, D_MODEL)


def _fox_sample(rows, x, g, mod, w_in, b_f, cache_kv, cache_logf, page_table):
    w, bias = _fox_weights(w_in, b_f)
    q, kv, logf = _norm_proj(rows, x, g, mod, (0, 1), w, bias, _fox_proj_outs((F32,), "scale"))
    logf = logf[:, :FOX_HEADS]
    o = _fox_decode(page_table, q, _feat_major_pages(cache_kv), _feat_major_pages(cache_logf), kv, logf)
    return o, kv, logf


PROMPT_TM = 512
FFN_TM = 1024
POOL_TM = 256


def kernel(x_prompt, x_sample, cache_l0_cmp_kv, cache_l0_sel_kv, state_l0_win_kv, cache_l1_kv, cache_l1_logf, state_l2_pool, cache_l3_cmp_kv, cache_l3_sel_kv, state_l3_win_kv, state_ffn_conv, page_table, c_prompt, c_sample, mod_w, mod_b, norm_g, l0_nsa_w_in, l0_nsa_w_ck, l0_nsa_w_cv, l0_nsa_w_o, l1_fox_w_in, l1_fox_b_f, l1_fox_w_o, l2_pool_w, l2_pool_b, l2_pool_scale, l3_nsa_w_in, l3_nsa_w_ck, l3_nsa_w_cv, l3_nsa_w_o, ffn_w_gu, ffn_conv_w, ffn_conv_b, ffn_w_d):
    b, t, _ = x_prompt.shape
    ns = x_sample.shape[0]
    past_len = page_table.shape[1] * PAGE_SIZE
    rp = _Rows(b, t, min(PROMPT_TM, t))
    rp_ffn = _Rows(b, t, min(FFN_TM, t))
    rs = _Rows(ns, 1, ns)
    nsa = {0: (cache_l0_cmp_kv, cache_l0_sel_kv, state_l0_win_kv, l0_nsa_w_in, l0_nsa_w_ck, l0_nsa_w_cv, l0_nsa_w_o),
           3: (cache_l3_cmp_kv, cache_l3_sel_kv, state_l3_win_kv, l3_nsa_w_in, l3_nsa_w_ck, l3_nsa_w_cv, l3_nsa_w_o)}

    c_all = jnp.concatenate([c_prompt, c_sample], axis=0)
    c_all = jnp.pad(c_all, ((0, -c_all.shape[0] % SUBLANES), (0, 0)))
    mod = _modulation(c_all, mod_w, mod_b)

    xp = x_prompt.reshape(b * t, D_MODEL)
    xs = x_sample.reshape(ns, D_MODEL)
    st = {}
    conv_p, conv_s = [], []
    kv5 = lambda a, n: a.reshape(n, -1, 2, NSA_KV_HEADS, HEAD_DIM)
    for i in range(DEPTH):
        mp, ms = mod[i, :b], mod[i, b:b + ns]
        g = norm_g[i]
        kind = i % 3
        if kind == 0:
            c_c, c_s, s_w, w_in, w_ck, w_cv, w_o = nsa[i]
            op, kvc_t, kvs_t, kvw_t = _nsa_prompt(rp, xp, g[0], mp, w_in, w_ck, w_cv)
            os_, kvc_s, kvs_s, win_s = _nsa_sample(rs, xs, g[0], ms, w_in, w_ck, w_cv, c_c, c_s, s_w, page_table)
            leaf = lambda a: _leaf_from_feat_major(a, (2, NSA_KV_HEADS, HEAD_DIM))
            st[i] = (leaf(kvc_t), kv5(kvc_s, ns), leaf(kvs_t), kv5(kvs_s, ns),
                     leaf(kvw_t[:, :, -min(NSA_WINDOW, t):]), win_s)
            w_ob = w_o.astype(BF16)
            xp = _out_proj_residual(rp, op, w_ob, xp, g[1], mp, 2)
            xs = _out_proj_residual(rs, os_, w_ob, xs, g[1], ms, 2)
        elif kind == 1:
            op, kv_p, lf_p = _fox_prompt(rp, xp, g[0], mp, l1_fox_w_in, l1_fox_b_f)
            os_, kv_s, lf_s = _fox_sample(rs, xs, g[0], ms, l1_fox_w_in, l1_fox_b_f, cache_l1_kv, cache_l1_logf, page_table)
            st[i] = (_leaf_from_feat_major(kv_p, (2, FOX_HEADS, HEAD_DIM)), kv_s.reshape(ns, 1, 2, FOX_HEADS, HEAD_DIM),
                     lf_p.reshape(b, t, FOX_HEADS), lf_s.reshape(ns, 1, FOX_HEADS))
            w_ob = l1_fox_w_o.astype(BF16)
            xp = _out_proj_residual(rp, op, w_ob, xp, g[1], mp, 2)
            xs = _out_proj_residual(rs, os_, w_ob, xs, g[1], ms, 2)
        else:
            hp = _norm_only(rp, xp, g[0], mp, (0, 1))
            hs = _norm_only(rs, xs, g[0], ms, (0, 1))
            yp = _pool_mix(hp, b, t, min(POOL_TM, t), 0, l2_pool_w, l2_pool_b.reshape(-1), l2_pool_scale)
            ext = jnp.concatenate([state_l2_pool, hs[:, None, :]], axis=1)
            n_ext = POOL_STATE + 1
            ys = _pool_mix(ext.reshape(ns * n_ext, D_MODEL), ns, n_ext, n_ext, past_len - POOL_STATE,
                           l2_pool_w, l2_pool_b.reshape(-1), l2_pool_scale).reshape(ns, n_ext, D_MODEL)[:, -1]
            st[i] = (hp.reshape(b, t, D_MODEL)[:, -POOL_STATE:], ext[:, -POOL_STATE:])
            xp = _gated_residual(rp, yp, xp, g[1], mp, 2)
            xs = _gated_residual(rs, ys, xs, g[1], ms, 2)
        w_gu, w_d = ffn_w_gu[i].astype(BF16), ffn_w_d[i].astype(BF16)
        xp, cp = _conv_ffn(rp_ffn, xp, g[2], g[3], mp, w_gu, ffn_conv_w[i], ffn_conv_b[i], w_d, None)
        xs, cs = _conv_ffn(rs, xs, g[2], g[3], ms, w_gu, ffn_conv_w[i], ffn_conv_b[i], w_d, state_ffn_conv[i])
        conv_p.append(cp)
        conv_s.append(cs)
    return (xp.reshape(b, t, D_MODEL), xs.reshape(ns, 1, D_MODEL),
            *st[0], *st[1], *st[2], *st[3],
            jnp.stack(conv_p), jnp.stack(conv_s))
```

```python
import functools

import jax
import jax.numpy as jnp
import numpy as np
from jax import lax
from jax.experimental import pallas as pl
from jax.experimental.pallas import tpu as pltpu

D_MODEL = 1024
DEPTH = 4
PAGE_SIZE = 128
HEAD_DIM = 64
N_HEADS = D_MODEL // HEAD_DIM
NSA_KV_HEADS = 4
NSA_GROUP = N_HEADS // NSA_KV_HEADS
NSA_BLOCK = 64
NSA_TOPN = 16
NSA_WINDOW = 512
FOX_HEADS = D_MODEL // HEAD_DIM
POOL_WINDOWS = (2, 4, 8, 16)
POOL_GROUP_DIM = D_MODEL // len(POOL_WINDOWS)
POOL_STATE = max(POOL_WINDOWS) - 1
D_FF = 2816
CONV_W = 3
N_MOD = 6
RMS_EPS = 1e-6
NEG = -1e30
BIG = 1e30
ATTN_SCALE = HEAD_DIM ** -0.5
LOG2E = float(np.log2(np.e))

LANES = 128
SUBLANES = 8
VMEM_LIMIT = 56 * 1024 * 1024

KVW = NSA_KV_HEADS * HEAD_DIM
NSA_Q0, NSA_C0, NSA_S0, NSA_W0, NSA_G0 = 0, D_MODEL, D_MODEL + 2 * KVW, D_MODEL + 4 * KVW, D_MODEL + 6 * KVW
NSA_COLS = NSA_G0 + NSA_KV_HEADS * LANES
FOX_LF0 = 3 * D_MODEL
FOX_COLS = FOX_LF0 + LANES

F32 = jnp.float32
BF16 = jnp.bfloat16


def _cparams(*sem):
    return pltpu.CompilerParams(dimension_semantics=sem, vmem_limit_bytes=VMEM_LIMIT)


def _bdot(a, b):
    return jnp.dot(a, b, preferred_element_type=F32)


def _dot_nt(a, b):
    return lax.dot_general(a, b, (((1,), (1,)), ((), ())), preferred_element_type=F32)


def _rms_rows(x, g):
    return x * lax.rsqrt(jnp.mean(x * x, axis=-1, keepdims=True) + RMS_EPS) * g


def _split3(x):
    a = x.astype(BF16)
    r = x - a.astype(F32)
    b = r.astype(BF16)
    c = (r - b.astype(F32)).astype(BF16)
    return a, b, c


def _mod_kernel(c_ref, w_ref, b_ref, o_ref):
    c = c_ref[...]
    a = (c * jax.nn.sigmoid(c)).astype(BF16)
    o_ref[...] = _bdot(a, w_ref[...].astype(BF16)) + b_ref[...]


def _modulation(c_all, mod_w, mod_b):
    rows = c_all.shape[0]
    n = N_MOD * D_MODEL
    tn = D_MODEL
    return pl.pallas_call(
        _mod_kernel,
        grid=(DEPTH, n // tn),
        in_specs=[pl.BlockSpec((rows, D_MODEL), lambda i, j: (0, 0)),
                  pl.BlockSpec((None, D_MODEL, tn), lambda i, j: (i, 0, j)),
                  pl.BlockSpec((None, 1, tn), lambda i, j: (i, 0, j))],
        out_specs=pl.BlockSpec((None, rows, tn), lambda i, j: (i, 0, j)),
        out_shape=jax.ShapeDtypeStruct((DEPTH, rows, n), F32),
        compiler_params=_cparams("parallel", "parallel"),
        name="modulation",
    )(c_all, mod_w, mod_b.reshape(DEPTH, 1, n))


def _norm_h(x, g, shift, scale):
    return _rms_rows(x, g) * (1.0 + scale) + shift


class _Rows:
    def __init__(self, nseq, t, tm):
        assert t % tm == 0 or t == 1
        self.nseq, self.t = nseq, t
        self.decode = t == 1
        self.tm = nseq if self.decode else tm
        self.m = nseq * t
        self.tiles_per_seq = 1 if self.decode else t // tm
        self.ntiles = self.m // self.tm

    def mod_spec(self, chunk):
        if self.decode:
            return pl.BlockSpec((self.tm, D_MODEL), lambda i, *_: (0, chunk))
        tps = self.tiles_per_seq
        return pl.BlockSpec((None, 1, D_MODEL), lambda i, *_: (i // tps, 0, chunk))

    def mod_arr(self, mod):
        return mod if self.decode else mod.reshape(self.nseq, 1, N_MOD * D_MODEL)

    def row_spec(self, width, col=0):
        return pl.BlockSpec((self.tm, width), lambda i, *_: (i, col))


def _proj_kernel(outs, x_ref, g_ref, sh_ref, sc_ref, w_ref, bias_ref, *o_refs):
    h = _norm_h(x_ref[...], g_ref[...], sh_ref[...], sc_ref[...])
    hb = h.astype(BF16)
    refs = iter(o_refs)
    for c0, c1, kind, dtypes in outs:
        if kind == "h":
            p = h
        else:
            p = _bdot(hb, w_ref[:, c0:c1])
        if kind == "qscales":
            next(refs)[...] = (p * ATTN_SCALE).astype(BF16)
            next(refs)[...] = (p * (ATTN_SCALE * LOG2E)).astype(BF16)
            continue
        if kind == "scale":
            p = p * ATTN_SCALE
        elif kind == "scale2":
            p = p * (ATTN_SCALE * LOG2E)
        elif kind == "sigmoid":
            p = jax.nn.sigmoid(p)
        elif kind == "logsigmoid":
            p = jax.nn.log_sigmoid(p + bias_ref[...])
        for dt in dtypes:
            if dt == FEAT_MAJOR:
                next(refs)[...] = p.T
            else:
                next(refs)[...] = p.astype(dt)


FEAT_MAJOR = "feature-major f32"


def _norm_proj(rows, x, g, mod, chunks, w, bias, outs):
    ncols = w.shape[1]
    kern = functools.partial(_proj_kernel, outs)
    full = lambda shape: pl.BlockSpec(shape, lambda i: (0,) * len(shape))
    marr = rows.mod_arr(mod)
    flat = [(c1 - c0, dt) for c0, c1, _, dts in outs for dt in dts]
    tps = rows.tiles_per_seq

    def spec(wd, dt):
        if dt == FEAT_MAJOR:
            return pl.BlockSpec((None, wd, rows.tm), lambda i: (i // tps, 0, i % tps))
        return rows.row_spec(wd)

    def shape(wd, dt):
        if dt == FEAT_MAJOR:
            return jax.ShapeDtypeStruct((rows.nseq, wd, rows.t), F32)
        return jax.ShapeDtypeStruct((rows.m, wd), dt)

    return pl.pallas_call(
        kern,
        grid=(rows.ntiles,),
        in_specs=[rows.row_spec(D_MODEL), full((1, D_MODEL)),
                  rows.mod_spec(chunks[0]), rows.mod_spec(chunks[1]),
                  full((D_MODEL, ncols)), full((1, LANES))],
        out_specs=[spec(wd, dt) for wd, dt in flat],
        out_shape=[shape(wd, dt) for wd, dt in flat],
        compiler_params=_cparams("parallel"),
        name="norm_proj",
    )(x, g.reshape(1, D_MODEL), marr, marr, w, bias)


def _oproj_kernel(o_ref, w_ref, x_ref, g_ref, gate_ref, y_ref):
    y = _bdot(o_ref[...], w_ref[...])
    y_ref[...] = x_ref[...] + gate_ref[...] * _rms_rows(y, g_ref[...])


def _out_proj_residual(rows, o, w, x, g, mod, gate_chunk):
    k = w.shape[0]
    full = lambda shape: pl.BlockSpec(shape, lambda i: (0,) * len(shape))
    return pl.pallas_call(
        _oproj_kernel,
        grid=(rows.ntiles,),
        in_specs=[rows.row_spec(k), full((k, D_MODEL)), rows.row_spec(D_MODEL),
                  full((1, D_MODEL)), rows.mod_spec(gate_chunk)],
        out_specs=rows.row_spec(D_MODEL),
        out_shape=jax.ShapeDtypeStruct((rows.m, D_MODEL), F32),
        compiler_params=_cparams("parallel"),
        name="out_proj_residual",
    )(o, w, x, g.reshape(1, D_MODEL), rows.mod_arr(mod))


FFN_TF = 256
FFN_NJ = D_FF // FFN_TF
HALO = SUBLANES


def _ffn_kernel(decode, tiles_per_seq, x_ref, g2_ref, sh_ref, sc_ref, wg_ref, wu_ref, cw_ref, cb_ref,
                wd_ref, g3_ref, gate_ref, p0_ref, p1_ref, y_ref, st_ref, h_scr, act_scr, gs_scr, carry_scr):
    i, j = pl.program_id(0), pl.program_id(1)
    tm = x_ref.shape[0]
    tf = wg_ref.shape[1]

    @pl.when(j == 0)
    def _():
        h_scr[...] = _norm_h(x_ref[...], g2_ref[...], sh_ref[...], sc_ref[...]).astype(BF16)

    h = h_scr[...]
    gcol = _bdot(h, wg_ref[...])
    ucol = _bdot(h, wu_ref[...])
    cw = cw_ref[...]
    if decode:
        a = cb_ref[...] + cw[0:1] * p0_ref[...] + cw[1:2] * p1_ref[...] + cw[2:3] * gcol
        st_ref[...] = gcol
    else:
        first = (i % tiles_per_seq) == 0

        @pl.when(first)
        def _():
            gs_scr[0:HALO, :] = jnp.zeros((HALO, gs_scr.shape[1]), F32)

        @pl.when(jnp.logical_not(first))
        def _():
            gs_scr[0:HALO, :] = carry_scr[j]

        gs_scr[HALO:HALO + tm, :] = gcol
        a = (cb_ref[...] + cw[0:1] * gs_scr[HALO - 2:HALO - 2 + tm, :]
             + cw[1:2] * gs_scr[HALO - 1:HALO - 1 + tm, :] + cw[2:3] * gcol)
        tail = gcol[tm - HALO:tm, :]
        carry_scr[j] = tail
        st_ref[...] = tail
    act_scr[:, pl.ds(pl.multiple_of(j * tf, tf), tf)] = (a * jax.nn.sigmoid(a) * ucol).astype(BF16)

    @pl.when(j == pl.num_programs(1) - 1)
    def _():
        y = _bdot(act_scr[...], wd_ref[...])
        y_ref[...] = x_ref[...] + gate_ref[...] * _rms_rows(y, g3_ref[...])


def _conv_ffn(rows, x, g2, g3, mod, w_gu, conv_w, conv_b, w_d, prev):
    tm, tf = rows.tm, FFN_TF
    decode = rows.decode
    tps = rows.tiles_per_seq
    kern = functools.partial(_ffn_kernel, decode, tps)
    full = lambda shape: pl.BlockSpec(shape, lambda i, j: (0,) * len(shape))
    col = lambda r: pl.BlockSpec((r, tf), lambda i, j: (0, j))
    if decode:
        p0, p1 = prev[:, 0], prev[:, 1]
        pspec = pl.BlockSpec((tm, tf), lambda i, j: (0, j))
        st_spec = pl.BlockSpec((tm, tf), lambda i, j: (0, j))
        st_shape = jax.ShapeDtypeStruct((rows.m, D_FF), F32)
    else:
        p0 = p1 = jnp.zeros((SUBLANES, LANES), F32)
        pspec = full((SUBLANES, LANES))
        st_spec = pl.BlockSpec((None, HALO, tf), lambda i, j: (i, 0, j))
        st_shape = jax.ShapeDtypeStruct((rows.ntiles, HALO, D_FF), F32)
    marr = rows.mod_arr(mod)
    y, st = pl.pallas_call(
        kern,
        grid=(rows.ntiles, FFN_NJ),
        in_specs=[rows.row_spec(D_MODEL), full((1, D_MODEL)), rows.mod_spec(3), rows.mod_spec(4),
                  pl.BlockSpec((D_MODEL, tf), lambda i, j: (0, j)),
                  pl.BlockSpec((D_MODEL, tf), lambda i, j: (0, FFN_NJ + j)),
                  col(CONV_W), col(1),
                  full((D_FF, D_MODEL)),
                  full((1, D_MODEL)), rows.mod_spec(5), pspec, pspec],
        out_specs=[rows.row_spec(D_MODEL), st_spec],
        out_shape=[jax.ShapeDtypeStruct((rows.m, D_MODEL), F32), st_shape],
        scratch_shapes=[pltpu.VMEM((tm, D_MODEL), BF16), pltpu.VMEM((tm, D_FF), BF16),
                        pltpu.VMEM((HALO + tm, tf), F32), pltpu.VMEM((FFN_NJ, HALO, tf), F32)],
        compiler_params=_cparams("arbitrary", "arbitrary"),
        name="conv_ffn",
    )(x, g2.reshape(1, D_MODEL), marr, marr, w_gu, w_gu, conv_w, conv_b.reshape(1, D_FF),
      w_d, g3.reshape(1, D_MODEL), marr, p0, p1)
    if decode:
        return y, jnp.stack([prev[:, 1], st], axis=1)
    last = st.reshape(rows.nseq, tps, HALO, D_FF)[:, -1]
    return y, last[:, HALO - (CONV_W - 1):]


def _mm_kernel(a_ref, w_ref, o_ref):
    o_ref[...] = _bdot(a_ref[...], w_ref[...])


def _matmul(a, w, tm):
    m, k = a.shape
    n = w.shape[1]
    return pl.pallas_call(
        _mm_kernel,
        grid=(m // tm,),
        in_specs=[pl.BlockSpec((tm, k), lambda i: (i, 0)), pl.BlockSpec((k, n), lambda i: (0, 0))],
        out_specs=pl.BlockSpec((tm, n), lambda i: (i, 0)),
        out_shape=jax.ShapeDtypeStruct((m, n), F32),
        compiler_params=_cparams("parallel"),
        name="matmul",
    )(a, w)


Q_TILE = 128
KEY_CHUNK = 128
GQ = NSA_GROUP * Q_TILE
SEL_UNROLL = 4
SCAN_UNROLL = 4
M_INIT = -3e38


def _alibi_slopes():
    h = np.arange(1, N_HEADS + 1, dtype=np.float32)
    return np.exp2(-8.0 * h / N_HEADS).astype(np.float32).reshape(NSA_KV_HEADS, NSA_GROUP)


def _half_masks():
    lane = lax.broadcasted_iota(jnp.int32, (Q_TILE, LANES), 1)
    return lane < HEAD_DIM


def _stack_group_heads(q, lo, par=None):
    tiles = []
    for g in range(NSA_GROUP):
        t = q[:, (g // 2) * LANES:(g // 2 + 1) * LANES]
        keep = lo if g % 2 == 0 else jnp.logical_not(lo)
        t = jnp.where(keep, t, jnp.zeros_like(t))
        if par is not None:
            rolled = pltpu.roll(t.astype(F32), HEAD_DIM, axis=1).astype(BF16)
            t = jnp.where(par == g % 2, t, rolled)
        tiles.append(t)
    return jnp.concatenate(tiles, axis=0)


def _nsa_cmp_kernel(nb, q_ref, kcc_ref, vcc_ref, slope_ref, oc_ref, selb_ref, any_ref):
    j = pl.program_id(2)
    q0 = j * Q_TILE
    lo = _half_masks()
    qs = _stack_group_heads(q_ref[...], lo)
    st = _dot_nt(kcc_ref[...], qs)
    n_i = lax.broadcasted_iota(jnp.int32, (nb, GQ), 0)
    qpos = q0 + (lax.broadcasted_iota(jnp.int32, (nb, GQ), 1) & (Q_TILE - 1))
    ends = n_i * NSA_BLOCK + (NSA_BLOCK - 1)
    mask = ends <= qpos
    s = st - slope_ref[...] * (qpos - ends).astype(F32)
    s = jnp.where(mask, s, NEG)
    e = jnp.exp(s - jnp.max(s, axis=0, keepdims=True))
    p = jnp.where(mask, e / jnp.sum(e, axis=0, keepdims=True), 0.0)
    acc = _bdot(p.T.astype(BF16), vcc_ref[...])
    for t in range(NSA_GROUP // 2):
        a0 = acc[(2 * t) * Q_TILE:(2 * t + 1) * Q_TILE]
        a1 = acc[(2 * t + 1) * Q_TILE:(2 * t + 2) * Q_TILE]
        oc_ref[:, t * LANES:(t + 1) * LANES] = jnp.where(lo, a0, a1)

    imp = p[:, 0:Q_TILE]
    for g in range(1, NSA_GROUP):
        imp = imp + p[:, g * Q_TILE:(g + 1) * Q_TILE]
    n2 = lax.broadcasted_iota(jnp.int32, (nb, Q_TILE), 0)
    cur = (q0 + lax.broadcasted_iota(jnp.int32, (nb, Q_TILE), 1)) // NSA_BLOCK
    forced = (n2 == 0) | (n2 == cur) | (n2 == cur - 1)
    cur_row = cur[0:1]
    budget = min(NSA_TOPN, nb) - (1 + (cur_row >= 1).astype(jnp.int32) + (cur_row >= 2).astype(jnp.int32))
    score = jnp.where(forced, -jnp.inf, jnp.where(n2 <= cur, imp, NEG))

    def pick_next(r, carry):
        sc, sel = carry
        m = jnp.max(sc, axis=0, keepdims=True)
        first = jnp.min(jnp.where(sc == m, n2, nb), axis=0, keepdims=True)
        pick = n2 == first
        sel = jnp.where(pick & (m > 0.5 * NEG) & (r < budget), 1.0, sel)
        return jnp.where(pick, -jnp.inf, sc), sel

    common = max(min(NSA_TOPN, nb) - 3, 0)
    carry = lax.fori_loop(0, common, pick_next, (score, forced.astype(F32)), unroll=True)
    _, sel = lax.cond(j == 0, lambda c: lax.fori_loop(common, common + 2, pick_next, c, unroll=True), lambda c: c, carry)
    sel_t = sel.T
    if nb < LANES:
        sel_t = jnp.concatenate([sel_t, jnp.zeros((Q_TILE, LANES - nb), F32)], axis=1)
    selb_ref[...] = jnp.where(sel_t > 0.0, 0.0, NEG).astype(BF16)
    any_ref[...] = jnp.max(sel_t, axis=0, keepdims=True)


def _nsa_compressed(b, t, q, kcc, vcc):
    nb = t // NSA_BLOCK
    nq = t // Q_TILE
    slopes = jnp.asarray(np.repeat(_alibi_slopes(), Q_TILE, axis=1).reshape(NSA_KV_HEADS, 1, GQ))
    return pl.pallas_call(
        functools.partial(_nsa_cmp_kernel, nb),
        grid=(b, NSA_KV_HEADS, nq),
        in_specs=[pl.BlockSpec((Q_TILE, KVW), lambda bi, k, j: (bi * nq + j, k)),
                  pl.BlockSpec((None, None, nb, LANES), lambda bi, k, j: (bi, k, 0, 0)),
                  pl.BlockSpec((None, None, nb, LANES), lambda bi, k, j: (bi, k, 0, 0)),
                  pl.BlockSpec((None, 1, GQ), lambda bi, k, j: (k, 0, 0))],
        out_specs=[pl.BlockSpec((Q_TILE, KVW), lambda bi, k, j: (bi * nq + j, k)),
                   pl.BlockSpec((None, None, Q_TILE, LANES), lambda bi, k, j: (bi, k, j, 0)),
                   pl.BlockSpec((None, None, None, 1, LANES), lambda bi, k, j: (bi, k, j, 0, 0))],
        out_shape=[jax.ShapeDtypeStruct((b * t, D_MODEL), F32),
                   jax.ShapeDtypeStruct((b, NSA_KV_HEADS, t, LANES), BF16),
                   jax.ShapeDtypeStruct((b, NSA_KV_HEADS, nq, 1, LANES), F32)],
        compiler_params=_cparams("parallel", "parallel", "parallel"),
        name="nsa_compressed",
    )(q, kcc, vcc, slopes)


def _nsa_sw_kernel(nq, flags_ref, q_ref, selb_ref, ks_ref, vst_ref, kw_ref, vwt_ref, posx_ref, onehot_ref, qx_ref,
                   oc_ref, gate_ref, slope_ref, o_ref, ksa_scr, kwa_scr, qaug_scr, ms_scr, as_scr, mw_scr, aw_scr,
                   sta_scr, stb_scr, vsa_scr, vwa_scr, list_scr):
    k, j = pl.program_id(1), pl.program_id(2)
    par = k % 2
    lane_b = lax.broadcasted_iota(jnp.int32, (KEY_CHUNK, LANES), 1)

    @pl.when(j == 0)
    def _():
        def fill(c, carry):
            r0 = pl.multiple_of(c * KEY_CHUNK, KEY_CHUNK)
            own = (lane_b // HEAD_DIM) == par
            px = posx_ref[...]
            ksa_scr[pl.ds(r0, KEY_CHUNK), 0:LANES] = onehot_ref[pl.ds(r0, KEY_CHUNK), :]
            ksa_scr[pl.ds(r0, KEY_CHUNK), LANES:2 * LANES] = jnp.where(own, ks_ref[pl.ds(r0, KEY_CHUNK), :], px)
            kwa_scr[pl.ds(r0, KEY_CHUNK), :] = jnp.where(own, kw_ref[pl.ds(r0, KEY_CHUNK), :], px)
            own_rows = (lax.broadcasted_iota(jnp.int32, (LANES, KEY_CHUNK), 0) // HEAD_DIM) == par
            vsa_scr[:, pl.ds(r0, KEY_CHUNK)] = jnp.where(own_rows, vst_ref[:, pl.ds(r0, KEY_CHUNK)], 1.0).astype(BF16)
            vwa_scr[:, pl.ds(r0, KEY_CHUNK)] = jnp.where(own_rows, vwt_ref[:, pl.ds(r0, KEY_CHUNK)], 1.0).astype(BF16)
            return carry

        lax.fori_loop(0, nq, fill, 0)

    lo = _half_masks()
    qq = _stack_group_heads(q_ref[...], lo, par)
    own_q = (lax.broadcasted_iota(jnp.int32, (GQ, LANES), 1) // HEAD_DIM) == par
    qaug_scr[:, 0:LANES] = jnp.concatenate([selb_ref[...]] * NSA_GROUP, axis=0)
    qaug_scr[:, LANES:2 * LANES] = jnp.where(own_q, qq, qx_ref[...])
    slope = slope_ref[...]
    for m_ref, a_ref in ((ms_scr, as_scr), (mw_scr, aw_scr)):
        m_ref[...] = jnp.full(m_ref.shape, M_INIT, F32)
        a_ref[...] = jnp.zeros(a_ref.shape, F32)

    key_l = lax.broadcasted_iota(jnp.int32, (KEY_CHUNK, GQ), 0)
    q_l = lax.broadcasted_iota(jnp.int32, (KEY_CHUNK, GQ), 1) & (Q_TILE - 1)

    def step(pieces, m_ref, a_ref):
        shifts = [jnp.where(live, slope * ((c - j) * KEY_CHUNK).astype(F32), -jnp.inf) for _, c, live, _ in pieces]
        m_prev = m_ref[...]
        m_new = m_prev
        for (st, _, _, _), shift in zip(pieces, shifts):
            m_new = jnp.maximum(m_new, jnp.max(st, axis=0, keepdims=True) + shift)
        p = jnp.concatenate([jnp.exp2(st - (m_new - shift)).astype(BF16)
                             for (st, _, _, _), shift in zip(pieces, shifts)], axis=0)
        vo = jnp.concatenate([vt for _, _, _, vt in pieces], axis=1)
        a_ref[...] = jnp.exp2(m_prev - m_new) * a_ref[...] + _bdot(vo, p)
        m_ref[...] = m_new

    def sel_piece(c, live, diag=False):
        k0 = pl.multiple_of(c * KEY_CHUNK, KEY_CHUNK)
        st = _dot_nt(ksa_scr[pl.ds(k0, KEY_CHUNK), :], qaug_scr[...])
        if diag:
            st = jnp.where(key_l <= q_l, st, NEG)
        return st, c, live, vsa_scr[:, pl.ds(k0, KEY_CHUNK)]

    word0 = ((pl.program_id(0) * NSA_KV_HEADS + k) * nq + j) * _flag_words(nq)

    def scan(i, n):
        for u in range(SCAN_UNROLL):
            c = jnp.minimum(i * SCAN_UNROLL + u, nq - 1)
            bit = (flags_ref[word0 + c // 32] >> (c % 32)) & 1
            list_scr[n] = c
            n = n + jnp.where(i * SCAN_UNROLL + u < j, bit, 0)
        return n

    n_act = lax.fori_loop(0, (j + SCAN_UNROLL - 1) // SCAN_UNROLL, scan, 0)

    n_groups = (n_act + SEL_UNROLL - 1) // SEL_UNROLL

    def group_chunks(g):
        out = []
        for u in range(SEL_UNROLL):
            idx = g * SEL_UNROLL + u
            c = list_scr[jnp.minimum(idx, n_act - 1)]
            out.append((c, idx < n_act, pl.multiple_of(c * KEY_CHUNK, KEY_CHUNK)))
        return out

    def sel_scores(g, st_ref):
        for u, (_, _, k0) in enumerate(group_chunks(g)):
            st_ref[u] = _dot_nt(ksa_scr[pl.ds(k0, KEY_CHUNK), :], qaug_scr[...])

    def sel_update(g, st_ref):
        step([(st_ref[u], c, live, vsa_scr[:, pl.ds(k0, KEY_CHUNK)])
              for u, (c, live, k0) in enumerate(group_chunks(g))], ms_scr, as_scr)

    @pl.when(n_groups > 0)
    def _():
        sel_scores(0, sta_scr)

    def sel_body(i, carry):
        sel_scores(2 * i + 1, stb_scr)
        sel_update(2 * i, sta_scr)
        sel_scores(2 * i + 2, sta_scr)
        sel_update(2 * i + 1, stb_scr)
        return carry

    lax.fori_loop(0, (n_groups + 1) // 2, sel_body, 0)
    step([sel_piece(j, True, diag=True)], ms_scr, as_scr)

    n_win = NSA_WINDOW // KEY_CHUNK
    pieces = []
    for dc in range(n_win + 1):
        c = j - n_win + dc
        cc = jnp.maximum(c, 0)
        k0 = pl.multiple_of(cc * KEY_CHUNK, KEY_CHUNK)
        st = _dot_nt(kwa_scr[pl.ds(k0, KEY_CHUNK), :], qaug_scr[:, LANES:2 * LANES])
        if dc == 0:
            st = jnp.where(key_l > q_l, st, NEG)
        elif dc == n_win:
            st = jnp.where(key_l <= q_l, st, NEG)
        pieces.append((st, cc, c >= 0, vwa_scr[:, pl.ds(k0, KEY_CHUNK)]))
    step(pieces, mw_scr, aw_scr)

    own0 = pl.multiple_of(par * HEAD_DIM, HEAD_DIM)
    oth0 = pl.multiple_of((1 - par) * HEAD_DIM, HEAD_DIM)
    o_s = as_scr[pl.ds(own0, HEAD_DIM), :] / as_scr[pl.ds(oth0, 1), :]
    o_w = aw_scr[pl.ds(own0, HEAD_DIM), :] / aw_scr[pl.ds(oth0, 1), :]
    gates_t = gate_ref[...].T
    oc_t = oc_ref[...].T
    mix = []
    for g in range(NSA_GROUP):
        cols = slice(g * Q_TILE, (g + 1) * Q_TILE)
        mix.append(gates_t[3 * g:3 * g + 1] * oc_t[g * HEAD_DIM:(g + 1) * HEAD_DIM]
                   + gates_t[3 * g + 1:3 * g + 2] * o_s[:, cols] + gates_t[3 * g + 2:3 * g + 3] * o_w[:, cols])
    o_ref[...] = jnp.concatenate(mix, axis=0).T.astype(o_ref.dtype)


def _flag_words(nchunks):
    return -(-nchunks // 32)


def _bf16_pieces(x):
    def rnd(v):
        return np.asarray(v, np.float32).astype(jnp.bfloat16).astype(np.float32)
    a = rnd(x)
    b = rnd(x - a)
    return a, b, rnd(x - a - b)


def _nsa_fold_constants(t):
    posx = np.zeros((KEY_CHUNK, LANES), np.float32)
    loc = np.arange(KEY_CHUNK, dtype=np.float32)
    slopes = _alibi_slopes()
    qx = np.zeros((NSA_KV_HEADS, GQ, LANES), np.float32)
    ql = np.tile(np.arange(Q_TILE, dtype=np.float32), NSA_GROUP)
    for k in range(NSA_KV_HEADS):
        srow = np.repeat(slopes[k], Q_TILE)
        srow = (srow * np.float32(LOG2E)).astype(np.float32)
        pieces = _bf16_pieces(srow) + _bf16_pieces(-srow * ql)
        for base in (0, HEAD_DIM):
            for i, pc in enumerate(pieces):
                qx[k, :, base + i] = pc
    for base in (0, HEAD_DIM):
        posx[:, base:base + 3] = loc[:, None]
        posx[:, base + 3:base + 6] = 1.0
    onehot = (np.arange(t)[:, None] // NSA_BLOCK == np.arange(LANES)[None, :]).astype(np.float32)
    return jnp.asarray(posx, BF16), jnp.asarray(onehot, BF16), jnp.asarray(qx, BF16)


def _pack_chunk_flags(any_sel, nchunks):
    blocks_per_chunk = KEY_CHUNK // NSA_BLOCK
    f = any_sel[:, :, :, 0, :nchunks * blocks_per_chunk]
    f = f.reshape(f.shape[:3] + (nchunks, blocks_per_chunk)).max(axis=-1) > 0.0
    words = _flag_words(nchunks)
    f = jnp.pad(f, ((0, 0), (0, 0), (0, 0), (0, words * 32 - nchunks)))
    bits = f.reshape(f.shape[:3] + (words, 32)).astype(jnp.uint32) << jnp.arange(32, dtype=jnp.uint32)
    return lax.bitcast_convert_type(bits.sum(axis=-1, dtype=jnp.uint32), jnp.int32).reshape(-1)


def _nsa_selected_window(b, t, q, selb, any_sel, kvs, kvs_t, kvw, kvw_t, oc, gates):
    assert Q_TILE == KEY_CHUNK
    nq = t // Q_TILE
    slopes2 = (_alibi_slopes() * np.float32(LOG2E)).astype(np.float32)
    slopes = jnp.asarray(np.repeat(slopes2, Q_TILE, axis=1).reshape(NSA_KV_HEADS, 1, GQ))
    posx, onehot, qx = _nsa_fold_constants(t)
    flags = _pack_chunk_flags(any_sel, nq)
    pairs = KVW // LANES
    qspec = pl.BlockSpec((Q_TILE, KVW), lambda bi, k, j, fl: (bi * nq + j, k))
    kspec = pl.BlockSpec((None, t, LANES), lambda bi, k, j, fl: (bi, 0, k // 2))
    vspec = pl.BlockSpec((None, LANES, t), lambda bi, k, j, fl: (bi, pairs + k // 2, 0))
    const = lambda shape: pl.BlockSpec(shape, lambda bi, k, j, fl: (0,) * len(shape))
    return pl.pallas_call(
        functools.partial(_nsa_sw_kernel, nq),
        grid_spec=pltpu.PrefetchScalarGridSpec(
            num_scalar_prefetch=1,
            grid=(b, NSA_KV_HEADS, nq),
            in_specs=[qspec,
                      pl.BlockSpec((None, None, Q_TILE, LANES), lambda bi, k, j, fl: (bi, k, j, 0)),
                      kspec, vspec, kspec, vspec,
                      const((KEY_CHUNK, LANES)), const((t, LANES)),
                      pl.BlockSpec((None, GQ, LANES), lambda bi, k, j, fl: (k, 0, 0)),
                      qspec,
                      pl.BlockSpec((Q_TILE, LANES), lambda bi, k, j, fl: (bi * nq + j, k)),
                      pl.BlockSpec((None, 1, GQ), lambda bi, k, j, fl: (k, 0, 0))],
            out_specs=qspec,
            scratch_shapes=[pltpu.VMEM((t, 2 * LANES), BF16), pltpu.VMEM((t, LANES), BF16),
                            pltpu.VMEM((GQ, 2 * LANES), BF16),
                            pltpu.VMEM((1, GQ), F32), pltpu.VMEM((LANES, GQ), F32),
                            pltpu.VMEM((1, GQ), F32), pltpu.VMEM((LANES, GQ), F32),
                            pltpu.VMEM((SEL_UNROLL, KEY_CHUNK, GQ), F32), pltpu.VMEM((SEL_UNROLL, KEY_CHUNK, GQ), F32),
                            pltpu.VMEM((LANES, t), BF16), pltpu.VMEM((LANES, t), BF16),
                            pltpu.SMEM((nq,), jnp.int32)]),
        out_shape=jax.ShapeDtypeStruct((b * t, D_MODEL), BF16),
        compiler_params=_cparams("parallel", "arbitrary", "arbitrary"),
        name="nsa_selected_window",
    )(flags, q, selb, kvs, kvs_t, kvw, kvw_t, posx, onehot, qx, oc, gates, slopes)


def _nsa_weights(w_in):
    w = w_in.astype(BF16)
    ng = NSA_GROUP * 3
    gcols = [jnp.pad(w[:, NSA_G0 + k * ng:NSA_G0 + (k + 1) * ng], ((0, 0), (0, LANES - ng)))
             for k in range(NSA_KV_HEADS)]
    return jnp.concatenate([w[:, :NSA_G0]] + gcols, axis=1)


def _nsa_proj_outs(kv_dtypes, exp2_queries=False):
    return ((NSA_Q0, NSA_C0, "qscales", (BF16, BF16)) if exp2_queries else (NSA_Q0, NSA_C0, "scale", (BF16,)),
            (NSA_C0, NSA_S0, "", kv_dtypes), (NSA_S0, NSA_W0, "", kv_dtypes), (NSA_W0, NSA_G0, "", kv_dtypes),
            (NSA_G0, NSA_COLS, "sigmoid", (F32,)))


NO_BIAS = np.zeros((1, LANES), np.float32)


def _leaf_from_feat_major(a, feat_shape):
    return jnp.moveaxis(a.reshape(a.shape[:1] + tuple(feat_shape) + a.shape[2:]), -1, 1)


def _compress_blocks(kvc, nseq, nb, w_ck, w_cv):
    blocks = kvc.reshape(nseq, nb, NSA_BLOCK, 2, NSA_KV_HEADS, HEAD_DIM).transpose(3, 0, 1, 4, 2, 5)
    blocks = blocks.reshape(2, nseq * nb * NSA_KV_HEADS, NSA_BLOCK * HEAD_DIM)
    tm = min(256, blocks.shape[1])
    out = []
    for a, w in ((blocks[0], w_ck), (blocks[1], w_cv)):
        c = _matmul(a, w.astype(BF16), tm).reshape(nseq, nb, NSA_KV_HEADS, HEAD_DIM).transpose(0, 2, 1, 3)
        out.append(jnp.concatenate([c, c], axis=-1).astype(BF16))
    return out


def _nsa_prompt(rows, x, g, mod, w_in, w_ck, w_cv):
    b, t = rows.nseq, rows.t
    q, q2, kvc_t, kvc_b, kvs_t, kvs_b, kvw_t, kvw_b, gates = _norm_proj(
        rows, x, g, mod, (0, 1), _nsa_weights(w_in), jnp.asarray(NO_BIAS), _nsa_proj_outs((FEAT_MAJOR, BF16), True))
    kcc, vcc = _compress_blocks(kvc_b, b, t // NSA_BLOCK, w_ck, w_cv)
    oc, selb, any_sel = _nsa_compressed(b, t, q, kcc, vcc)
    o = _nsa_selected_window(b, t, q2, selb, any_sel, kvs_b.reshape(b, t, 2 * KVW), kvs_t,
                             kvw_b.reshape(b, t, 2 * KVW), kvw_t, oc, gates)
    return o, kvc_t, kvs_t, kvw_t


FOX_TQ = 512
FOX_TK = 512
CUM_TILE = 256
FOX_PAIRS = FOX_HEADS // 2

def _fox_proj_outs(kv_dtypes, q_kind):
    return ((0, D_MODEL, q_kind, (BF16,)), (D_MODEL, FOX_LF0, "", kv_dtypes), (FOX_LF0, FOX_COLS, "logsigmoid", (F32,)))


def _fox_weights(w_in, b_f):
    w = jnp.pad(w_in.astype(BF16), ((0, 0), (0, FOX_COLS - w_in.shape[1])))
    bias = jnp.pad(b_f.astype(F32), (0, LANES - FOX_HEADS)).reshape(1, LANES)
    return w, bias


def _tri_cumsum(x):
    n = x.shape[0]
    tri = (lax.broadcasted_iota(jnp.int32, (n, n), 1) <= lax.broadcasted_iota(jnp.int32, (n, n), 0)).astype(BF16)
    a, b, c = _split3(x)
    return _bdot(tri, a) + _bdot(tri, b) + _bdot(tri, c)


def _cumsum_kernel(x_ref, o_ref, carry_scr):
    @pl.when(pl.program_id(1) == 0)
    def _():
        carry_scr[...] = jnp.zeros_like(carry_scr)

    cum = _tri_cumsum(x_ref[...]) + carry_scr[0:1, :]
    o_ref[...] = cum
    carry_scr[...] = jnp.broadcast_to(cum[-1:, :], carry_scr.shape)


def _cumsum_rows(x, nseq, t):
    nt = t // CUM_TILE
    return pl.pallas_call(
        _cumsum_kernel,
        grid=(nseq, nt),
        in_specs=[pl.BlockSpec((CUM_TILE, LANES), lambda s, i: (s * nt + i, 0))],
        out_specs=pl.BlockSpec((CUM_TILE, LANES), lambda s, i: (s * nt + i, 0)),
        out_shape=jax.ShapeDtypeStruct(x.shape, F32),
        scratch_shapes=[pltpu.VMEM((SUBLANES, LANES), F32)],
        compiler_params=_cparams("parallel", "arbitrary"),
        name="cumsum_rows",
    )(x)


def _fox_kernel(nk, q_ref, k_ref, vt_ref, ccol_ref, crow_ref, o_ref, ka_scr, base_scr, qa_scr, m_scr, acc_scr,
                sta_scr, stb_scr, va_scr):
    qi = pl.program_id(2)
    tq = q_ref.shape[0]
    lane = lax.broadcasted_iota(jnp.int32, (FOX_TK, LANES), 1)
    row = lax.broadcasted_iota(jnp.int32, (LANES, FOX_TK), 0)

    def own(e):
        return (lane // HEAD_DIM) == e

    def other_lane(e, i):
        return lane == (1 - e) * HEAD_DIM + i

    @pl.when(qi == 0)
    def _():
        def fill(c, carry):
            k0 = pl.multiple_of(c * FOX_TK, FOX_TK)
            kc = k_ref[pl.ds(k0, FOX_TK), :].astype(F32)
            for e in range(2):
                col = ccol_ref[pl.ds(k0, FOX_TK), e:e + 1]
                base = col[0:1, :]
                ext = jnp.zeros((FOX_TK, LANES), F32)
                for i, piece in enumerate(_split3((base - col) * LOG2E)):
                    ext = jnp.where(other_lane(e, i), piece.astype(F32), ext)
                ka_scr[e, pl.ds(k0, FOX_TK), :] = jnp.where(own(e), kc, ext).astype(BF16)
                base_scr[e, pl.ds(c, 1), :] = jnp.broadcast_to(base, (1, LANES))
                va_scr[e, :, pl.ds(k0, FOX_TK)] = jnp.where((row // HEAD_DIM) == e, vt_ref[:, pl.ds(k0, FOX_TK)],
                                                            1.0).astype(BF16)
            return carry

        lax.fori_loop(0, nk, fill, 0)

    q = q_ref[...].astype(F32)
    ones3 = jnp.zeros((tq, LANES), F32)
    for e in range(2):
        ext = ones3
        for i in range(3):
            ext = jnp.where(other_lane(e, i), 1.0, ext)
        qa_scr[e] = jnp.where(own(e), q, ext).astype(BF16)
    m_scr[...] = jnp.full(m_scr.shape, M_INIT, F32)
    acc_scr[...] = jnp.zeros(acc_scr.shape, F32)
    q0 = pl.multiple_of(qi * tq, tq)

    def scores(c, st_ref):
        k0 = pl.multiple_of(c * FOX_TK, FOX_TK)
        for e in range(2):
            st_ref[e] = _dot_nt(ka_scr[e, pl.ds(k0, FOX_TK), :], qa_scr[e])

    def update(c, live, st_ref, diag=False):
        k0 = pl.multiple_of(c * FOX_TK, FOX_TK)
        for e in range(2):
            st = st_ref[e]
            if diag:
                st = jnp.where(lax.broadcasted_iota(jnp.int32, st.shape, 0)
                               <= lax.broadcasted_iota(jnp.int32, st.shape, 1), st, NEG)
            base = base_scr[e, pl.ds(c, 1), :]
            shift = (crow_ref[e:e + 1, pl.ds(q0, tq)] - jnp.concatenate([base] * (tq // LANES), axis=1)) * LOG2E
            shift = jnp.where(live, shift, -jnp.inf)
            m_prev = m_scr[e]
            m_new = jnp.maximum(m_prev, jnp.max(st, axis=0, keepdims=True) + shift)
            p = jnp.exp2(st - (m_new - shift)).astype(BF16)
            acc_scr[e] = jnp.exp2(m_prev - m_new) * acc_scr[e] + _bdot(va_scr[e, :, pl.ds(k0, FOX_TK)], p)
            m_scr[e] = m_new

    scores(0, sta_scr)

    def body(i, carry):
        c = 2 * i
        scores(jnp.minimum(c + 1, qi), stb_scr)
        update(c, True, sta_scr)
        scores(jnp.minimum(c + 2, qi), sta_scr)
        update(jnp.minimum(c + 1, qi), c + 1 < qi, stb_scr)
        return carry

    lax.fori_loop(0, (qi + 1) // 2, body, 0)
    update(qi, True, sta_scr, diag=True)
    a0, a1 = acc_scr[0], acc_scr[1]
    r = lax.broadcasted_iota(jnp.int32, a0.shape, 0)
    o_t = jnp.where(r < HEAD_DIM, a0 / a0[HEAD_DIM:HEAD_DIM + 1], a1 / a1[0:1])
    o_ref[...] = o_t.T.astype(o_ref.dtype)


def _fox_attention(b, t, q, k, kv_t, cum_col, cum_row):
    assert FOX_TQ == FOX_TK and t % FOX_TQ == 0
    tq = FOX_TQ
    nq = t // tq
    vrow0 = D_MODEL // LANES
    return pl.pallas_call(
        functools.partial(_fox_kernel, nq),
        grid=(b, FOX_PAIRS, nq),
        in_specs=[pl.BlockSpec((tq, LANES), lambda bi, hp, qi: (bi * nq + qi, hp)),
                  pl.BlockSpec((None, t, LANES), lambda bi, hp, qi: (bi, 0, hp)),
                  pl.BlockSpec((None, LANES, t), lambda bi, hp, qi: (bi, vrow0 + hp, 0)),
                  pl.BlockSpec((None, None, t, 2), lambda bi, hp, qi: (bi, hp, 0, 0)),
                  pl.BlockSpec((None, None, 2, t), lambda bi, hp, qi: (bi, hp, 0, 0))],
        out_specs=pl.BlockSpec((tq, LANES), lambda bi, hp, qi: (bi * nq + qi, hp)),
        out_shape=jax.ShapeDtypeStruct((b * t, D_MODEL), BF16),
        scratch_shapes=[pltpu.VMEM((2, t, LANES), BF16), pltpu.VMEM((2, max(nq, SUBLANES), LANES), F32),
                        pltpu.VMEM((2, tq, LANES), BF16), pltpu.VMEM((2, 1, tq), F32),
                        pltpu.VMEM((2, LANES, tq), F32),
                        pltpu.VMEM((2, FOX_TK, tq), F32), pltpu.VMEM((2, FOX_TK, tq), F32),
                        pltpu.VMEM((2, LANES, t), BF16)],
        compiler_params=_cparams("parallel", "arbitrary", "arbitrary"),
        name="fox_attention",
    )(q, k, kv_t, cum_col, cum_row)


def _fox_prompt(rows, x, g, mod, w_in, b_f):
    b, t = rows.nseq, rows.t
    w, bias = _fox_weights(w_in, b_f)
    q, kv_t, kv_b, logf = _norm_proj(rows, x, g, mod, (0, 1), w, bias, _fox_proj_outs((FEAT_MAJOR, BF16), "scale2"))
    cum = _cumsum_rows(logf, b, t)[:, :FOX_HEADS].reshape(b, t, FOX_PAIRS, 2)
    o = _fox_attention(b, t, q, kv_b.reshape(b, t, 2 * D_MODEL), kv_t,
                       cum.transpose(0, 2, 1, 3), cum.transpose(0, 2, 3, 1))
    return o, kv_t, logf[:, :FOX_HEADS]


def _pool_kernel(tps, pos0, h_ref, w_ref, b_ref, scale_ref, o_ref, ext_scr):
    i = pl.program_id(0)
    tm = h_ref.shape[0]
    halo = 2 * SUBLANES
    first = (i % tps) == 0

    @pl.when(first)
    def _():
        ext_scr[0:halo, :] = jnp.zeros((halo, D_MODEL), F32)

    @pl.when(jnp.logical_not(first))
    def _():
        ext_scr[0:halo, :] = ext_scr[tm:tm + halo, :]

    ext_scr[halo:halo + tm, :] = h_ref[...]
    pos = pos0 + (i % tps) * tm + lax.broadcasted_iota(jnp.int32, (tm, 1), 0)
    for gi, w in enumerate(POOL_WINDOWS):
        c0, c1 = gi * POOL_GROUP_DIM, (gi + 1) * POOL_GROUP_DIM
        win = ext_scr[halo:halo + tm, c0:c1]
        for back in range(1, w):
            win = win + ext_scr[halo - back:halo - back + tm, c0:c1]
        cnt = jnp.minimum(w, pos + 1).astype(F32)
        mixed = win / cnt - h_ref[:, c0:c1]
        y = _bdot(mixed.astype(BF16), w_ref[gi]) + b_ref[:, c0:c1]
        o_ref[:, c0:c1] = y * scale_ref[:, c0:c1]


def _pool_mix(h_ext, nseq, t, tm, pos0, w_g, b_g, scale):
    assert POOL_STATE < 2 * SUBLANES
    tps = t // tm
    full = lambda shape: pl.BlockSpec(shape, lambda i: (0,) * len(shape))
    return pl.pallas_call(
        functools.partial(_pool_kernel, tps, pos0),
        grid=(nseq * tps,),
        in_specs=[pl.BlockSpec((tm, D_MODEL), lambda i: (i, 0)),
                  full((len(POOL_WINDOWS), POOL_GROUP_DIM, POOL_GROUP_DIM)), full((1, D_MODEL)), full((1, D_MODEL))],
        out_specs=pl.BlockSpec((tm, D_MODEL), lambda i: (i, 0)),
        out_shape=jax.ShapeDtypeStruct((nseq * t, D_MODEL), F32),
        scratch_shapes=[pltpu.VMEM((tm + 2 * SUBLANES, D_MODEL), F32)],
        compiler_params=_cparams("arbitrary"),
        name="pool_mix",
    )(h_ext, w_g.astype(BF16), b_g.reshape(1, D_MODEL), scale.reshape(1, D_MODEL))


def _residual_kernel(o_ref, x_ref, g_ref, gate_ref, y_ref):
    y_ref[...] = x_ref[...] + gate_ref[...] * _rms_rows(o_ref[...], g_ref[...])


def _gated_residual(rows, o, x, g, mod, gate_chunk):
    full = lambda shape: pl.BlockSpec(shape, lambda i: (0,) * len(shape))
    return pl.pallas_call(
        _residual_kernel,
        grid=(rows.ntiles,),
        in_specs=[rows.row_spec(D_MODEL), rows.row_spec(D_MODEL), full((1, D_MODEL)), rows.mod_spec(gate_chunk)],
        out_specs=rows.row_spec(D_MODEL),
        out_shape=jax.ShapeDtypeStruct((rows.m, D_MODEL), F32),
        compiler_params=_cparams("parallel"),
        name="gated_residual",
    )(o, x, g.reshape(1, D_MODEL), rows.mod_arr(mod))


H_ONLY = ((0, D_MODEL, "h", (F32,)),)


def _norm_only(rows, x, g, mod, chunks):
    dummy = jnp.zeros((D_MODEL, LANES), BF16)
    return _norm_proj(rows, x, g, mod, chunks, dummy, jnp.asarray(NO_BIAS), H_ONLY)[0]


DEC_B = 8
NEW_ROWS = SUBLANES


def _paged_grid_spec(grid, in_specs, out_specs, scratch_shapes):
    return pltpu.PrefetchScalarGridSpec(num_scalar_prefetch=1, grid=grid, in_specs=in_specs,
                                        out_specs=out_specs, scratch_shapes=scratch_shapes)


def _feat_major_pages(cache):
    n, p = cache.shape[:2]
    return jnp.moveaxis(cache, 1, -1).reshape(n, -1, p)


def _page_specs(npg, rows):
    return [pl.BlockSpec((None, rows, PAGE_SIZE), functools.partial(lambda b, pt, i: (pt[b * npg + i], 0, 0), i=i))
            for i in range(npg)]


def _cmp_dec_kernel(npg, pt_ref, *refs):
    pages, (w_ref, o_ref, lhs_scr) = refs[:npg], refs[npg:]
    groups = 2 * NSA_KV_HEADS
    for d in range(HEAD_DIM):
        for i, page in enumerate(pages):
            lhs_scr[i * groups:(i + 1) * groups, d * PAGE_SIZE:(d + 1) * PAGE_SIZE] = page[pl.ds(d, groups, stride=HEAD_DIM), :]
    both = _bdot(lhs_scr[...].astype(BF16), w_ref[...])
    half = o_ref.shape[1]
    is_key = (lax.broadcasted_iota(jnp.int32, o_ref.shape, 0) % groups) < NSA_KV_HEADS
    o_ref[...] = jnp.where(is_key, both[:, 0:half], both[:, half:2 * half]).astype(o_ref.dtype)


def _cmp_dec_weight(w):
    w3 = w.astype(BF16).reshape(NSA_BLOCK, HEAD_DIM, HEAD_DIM)
    halves = PAGE_SIZE // NSA_BLOCK
    eye = jnp.eye(halves, dtype=BF16)
    return jnp.einsum("rde,hg->dhrge", w3, eye).reshape(HEAD_DIM * PAGE_SIZE, halves * HEAD_DIM)


def _cmp_decode(page_table, cache_t, w_ck, w_cv):
    ns, npg = page_table.shape
    halves = PAGE_SIZE // NSA_BLOCK
    groups = 2 * NSA_KV_HEADS
    wspec = pl.BlockSpec((HEAD_DIM * PAGE_SIZE, 2 * halves * HEAD_DIM), lambda b, pt: (0, 0))
    w_both = jnp.concatenate([_cmp_dec_weight(w_ck), _cmp_dec_weight(w_cv)], axis=1)
    out = pl.pallas_call(
        functools.partial(_cmp_dec_kernel, npg),
        grid_spec=_paged_grid_spec(
            (ns,), _page_specs(npg, 2 * KVW) + [wspec],
            pl.BlockSpec((None, npg * groups, halves * HEAD_DIM), lambda b, pt: (b, 0, 0)),
            [pltpu.VMEM((npg * groups, HEAD_DIM * PAGE_SIZE), F32)]),
        out_shape=jax.ShapeDtypeStruct((ns, npg * groups, halves * HEAD_DIM), BF16),
        compiler_params=_cparams("parallel"),
        name="nsa_decode_compress",
    )(page_table.reshape(-1), *([cache_t] * npg), w_both)
    out = out.reshape(ns, npg, 2, NSA_KV_HEADS, halves, HEAD_DIM).transpose(0, 1, 4, 2, 3, 5)
    return out.reshape(ns, npg * halves, 2 * KVW)


def _dec_select_kernel(qpos, nblk, qbd_ref, kcv_ref, slope_ref, oc_ref, selb_ref, imp_scr):
    lane_b = lax.broadcasted_iota(jnp.int32, (N_HEADS, nblk), 1)
    ends = lane_b * NSA_BLOCK + (NSA_BLOCK - 1)
    mask = ends <= qpos
    bias = slope_ref[...] * (qpos - ends).astype(F32)
    imp_scr[...] = jnp.zeros(imp_scr.shape, F32)
    for bi in range(DEC_B):
        s = _dot_nt(qbd_ref[bi], kcv_ref[bi, :, 0:KVW]) - bias
        s = jnp.where(mask, s, NEG)
        e = jnp.exp(s - jnp.max(s, axis=-1, keepdims=True))
        p = jnp.where(mask, e / jnp.sum(e, axis=-1, keepdims=True), 0.0)
        oc_ref[bi] = _bdot(p.astype(BF16), kcv_ref[bi, :, KVW:2 * KVW])
        imp = p[0:NSA_KV_HEADS]
        for g in range(1, NSA_GROUP):
            imp = imp + p[g * NSA_KV_HEADS:(g + 1) * NSA_KV_HEADS]
        imp_scr[bi * NSA_KV_HEADS:(bi + 1) * NSA_KV_HEADS, 0:nblk] = imp

    rows = DEC_B * NSA_KV_HEADS
    n2 = lax.broadcasted_iota(jnp.int32, (rows, LANES), 1)
    cur = qpos // NSA_BLOCK
    forced = (n2 == 0) | (n2 == cur) | (n2 == cur - 1)
    score = jnp.where(forced, BIG, jnp.where(n2 <= cur, imp_scr[...], NEG))
    score = jnp.where(n2 <= cur, score, -jnp.inf)

    def pick_next(_, carry):
        sc, sel = carry
        m = jnp.max(sc, axis=-1, keepdims=True)
        first = jnp.min(jnp.where(sc == m, n2, LANES), axis=-1, keepdims=True)
        pick = n2 == first
        sel = jnp.where(pick & (m > 0.5 * NEG), 1.0, sel)
        return jnp.where(pick, -jnp.inf, sc), sel

    _, sel = lax.fori_loop(0, min(NSA_TOPN, cur + 1), pick_next, (score, jnp.zeros((rows, LANES), F32)), unroll=True)
    selb = jnp.where(sel > 0.0, 0.0, NEG).astype(BF16)
    for bi in range(DEC_B):
        one = selb[bi * NSA_KV_HEADS:(bi + 1) * NSA_KV_HEADS]
        selb_ref[bi] = jnp.concatenate([one] * NSA_GROUP, axis=0)


def _dec_slopes():
    return jnp.asarray(_alibi_slopes().T.reshape(N_HEADS, 1))


def _nsa_decode_select(qbd, kcv, qpos):
    ns, nblk = kcv.shape[0], kcv.shape[1]
    assert qpos // NSA_BLOCK < LANES and ns % DEC_B == 0
    blk = lambda shape: pl.BlockSpec((DEC_B,) + shape, lambda i: (i, 0, 0))
    return pl.pallas_call(
        functools.partial(_dec_select_kernel, qpos, nblk),
        grid=(ns // DEC_B,),
        in_specs=[blk((N_HEADS, KVW)), blk((nblk, 2 * KVW)), pl.BlockSpec((N_HEADS, 1), lambda i: (0, 0))],
        out_specs=[blk((N_HEADS, KVW)), blk((N_HEADS, LANES))],
        out_shape=[jax.ShapeDtypeStruct((ns, N_HEADS, KVW), F32), jax.ShapeDtypeStruct((ns, N_HEADS, LANES), BF16)],
        scratch_shapes=[pltpu.VMEM((DEC_B * NSA_KV_HEADS, LANES), F32)],
        compiler_params=_cparams("parallel"),
        name="nsa_decode_select",
    )(qbd, kcv, _dec_slopes())


def _new_key_tile(row):
    r = lax.broadcasted_iota(jnp.int32, (NEW_ROWS, row.shape[1]), 0)
    return jnp.where(r == 0, jnp.broadcast_to(row, (NEW_ROWS, row.shape[1])), 0.0).astype(BF16)


def _new_key_mask(s):
    return jnp.where(lax.broadcasted_iota(jnp.int32, s.shape, 1) == 0, s, NEG)


def _softmax_with_new_key(s, s_new):
    m = jnp.maximum(jnp.max(s, axis=-1, keepdims=True), jnp.max(s_new, axis=-1, keepdims=True))
    p, p_new = jnp.exp(s - m), jnp.exp(s_new - m)
    return p, p_new, jnp.sum(p, axis=-1, keepdims=True) + jnp.sum(p_new, axis=-1, keepdims=True)


def _dec_attend_kernel(npg, qpos, wlen, pt_ref, *refs):
    pages = refs[:npg]
    (qbd_ref, selb_ref, win_ref, snew_ref, wnew_ref, wcol_ref, oc_ref, gate_ref, slope_ref,
     o_ref, wout_ref) = refs[npg:]
    slope = slope_ref[...]
    qbd = qbd_ref[...]

    qaug = jnp.concatenate([qbd, selb_ref[...]], axis=1)
    blk_row = lax.broadcasted_iota(jnp.int32, (LANES, PAGE_SIZE), 0)
    blk_of_lane = lax.broadcasted_iota(jnp.int32, (LANES, PAGE_SIZE), 1) // NSA_BLOCK
    scores = []
    for i, page in enumerate(pages):
        onehot_t = (blk_row == (PAGE_SIZE // NSA_BLOCK) * i + blk_of_lane).astype(BF16)
        kaug_t = jnp.concatenate([page[0:KVW, :].astype(BF16), onehot_t], axis=0)
        scores.append(_bdot(qaug, kaug_t))
    kpos = lax.broadcasted_iota(jnp.int32, (N_HEADS, npg * PAGE_SIZE), 1)
    s = jnp.concatenate(scores, axis=1) - slope * (qpos - kpos).astype(F32)
    snew = snew_ref[...]
    s_new = _new_key_mask(_dot_nt(qbd, _new_key_tile(snew[:, 0:KVW])))
    p, p_new, l = _softmax_with_new_key(s, s_new)
    p = p.astype(BF16)
    acc = _bdot(p_new.astype(BF16), _new_key_tile(snew[:, KVW:2 * KVW]))
    for i, page in enumerate(pages):
        acc = acc + _dot_nt(p[:, i * PAGE_SIZE:(i + 1) * PAGE_SIZE], page[KVW:2 * KVW, :].astype(BF16))
    o_s = acc / l

    win = win_ref[...]
    wpos = qpos - wlen + lax.broadcasted_iota(jnp.int32, (N_HEADS, wlen), 1)
    s = _bdot(qbd, win[0:KVW, :].astype(BF16)) - slope * (qpos - wpos).astype(F32)
    s = jnp.where(wpos > qpos - NSA_WINDOW, s, NEG)
    wnew = wnew_ref[...]
    s_new = _new_key_mask(_dot_nt(qbd, _new_key_tile(wnew[:, 0:KVW])))
    p, p_new, l = _softmax_with_new_key(s, s_new)
    o_w = (_dot_nt(p.astype(BF16), win[KVW:2 * KVW, :].astype(BF16))
           + _bdot(p_new.astype(BF16), _new_key_tile(wnew[:, KVW:2 * KVW]))) / l

    gates = gate_ref[...]
    o_ref[...] = gates[:, 0:1] * oc_ref[...] + gates[:, 1:2] * o_s + gates[:, 2:3] * o_w
    lane = lax.broadcasted_iota(jnp.int32, win.shape, 1)
    wout_ref[...] = jnp.where(lane == wlen - 1, wcol_ref[...], pltpu.roll(win, wlen - 1, axis=1))


def _nsa_decode_attend(page_table, qbd, selb, cache_t, win_t, kvs_new, kvw_new, oc, gates, qpos):
    ns, npg = page_table.shape
    wlen = win_t.shape[2]
    assert qpos == npg * PAGE_SIZE
    per_b = lambda shape: pl.BlockSpec((None,) + shape, lambda b, pt: (b, 0, 0))
    return pl.pallas_call(
        functools.partial(_dec_attend_kernel, npg, qpos, wlen),
        grid_spec=_paged_grid_spec(
            (ns,),
            _page_specs(npg, 2 * KVW)
            + [per_b((N_HEADS, KVW)), per_b((N_HEADS, LANES)), per_b((2 * KVW, wlen)), per_b((1, 2 * KVW)),
               per_b((1, 2 * KVW)), per_b((2 * KVW, 1)), per_b((N_HEADS, KVW)), per_b((N_HEADS, LANES)),
               pl.BlockSpec((N_HEADS, 1), lambda b, pt: (0, 0))],
            [per_b((N_HEADS, KVW)), per_b((2 * KVW, wlen))],
            []),
        out_shape=[jax.ShapeDtypeStruct((ns, N_HEADS, KVW), F32), jax.ShapeDtypeStruct(win_t.shape, F32)],
        compiler_params=_cparams("parallel"),
        name="nsa_decode_attend",
    )(page_table.reshape(-1), *([cache_t] * npg), qbd, selb, win_t, kvs_new.reshape(ns, 1, 2 * KVW),
      kvw_new.reshape(ns, 1, 2 * KVW), kvw_new.reshape(ns, 2 * KVW, 1), oc, gates, _dec_slopes())


def _nsa_sample(rows, x, g, mod, w_in, w_ck, w_cv, cache_c, cache_s, win_buf, page_table):
    ns = rows.nseq
    qpos = page_table.shape[1] * PAGE_SIZE
    q, kvc, kvs, kvw, gates = _norm_proj(
        rows, x, g, mod, (0, 1), _nsa_weights(w_in), jnp.asarray(NO_BIAS), _nsa_proj_outs((F32,)))
    kcv = _cmp_decode(page_table, _feat_major_pages(cache_c), w_ck, w_cv)
    q4 = q.reshape(ns, NSA_KV_HEADS, NSA_GROUP, HEAD_DIM)
    eye = jnp.eye(NSA_KV_HEADS, dtype=q.dtype)
    qbd = jnp.einsum("bkgd,kj->bgkjd", q4, eye).reshape(ns, N_HEADS, KVW)
    oc, selb = _nsa_decode_select(qbd, kcv, qpos)
    g4 = gates.reshape(ns, NSA_KV_HEADS, LANES)[:, :, :NSA_GROUP * 3].reshape(ns, NSA_KV_HEADS, NSA_GROUP, 3)
    g_rows = jnp.pad(g4.transpose(0, 2, 1, 3).reshape(ns, N_HEADS, 3), ((0, 0), (0, 0), (0, LANES - 3)))
    o_bd, win_out = _nsa_decode_attend(
        page_table, qbd, selb, _feat_major_pages(cache_s), _feat_major_pages(win_buf), kvs, kvw, oc, g_rows, qpos)
    o5 = o_bd.reshape(ns, NSA_GROUP, NSA_KV_HEADS, NSA_KV_HEADS, HEAD_DIM)
    o = jnp.einsum("bgkkd->bkgd", o5).reshape(ns, D_MODEL).astype(BF16)
    win_out = jnp.moveaxis(win_out.reshape(win_buf.shape[:1] + win_buf.shape[2:] + win_buf.shape[1:2]), -1, 1)
    return o, kvc, kvs, win_out


def _fox_dec_kernel(npg, pt_ref, *refs):
    kv_pages, lf_pages = refs[:npg], refs[npg:2 * npg]
    q_ref, kvn_ref, lfn_ref, o_ref = refs[2 * npg:]
    own = (lax.broadcasted_iota(jnp.int32, (FOX_HEADS, D_MODEL), 1) // HEAD_DIM
           == lax.broadcasted_iota(jnp.int32, (FOX_HEADS, D_MODEL), 0))
    q = jnp.broadcast_to(q_ref[...].astype(F32), (FOX_HEADS, D_MODEL))
    qbd = jnp.where(own, q, 0.0).astype(BF16)
    upper = (lax.broadcasted_iota(jnp.int32, (PAGE_SIZE, PAGE_SIZE), 0)
             <= lax.broadcasted_iota(jnp.int32, (PAGE_SIZE, PAGE_SIZE), 1)).astype(BF16)
    carry = jnp.zeros((FOX_HEADS, 1), F32)
    cums, scores = [], []
    for kv_page, lf_page in zip(kv_pages, lf_pages):
        a, b, c = _split3(lf_page[...])
        cum = carry + (_bdot(a, upper) + _bdot(b, upper) + _bdot(c, upper))
        carry = cum[:, PAGE_SIZE - 1:PAGE_SIZE]
        cums.append(cum)
        scores.append(_bdot(qbd, kv_page[0:D_MODEL, :].astype(BF16)))
    cum_new = carry + lfn_ref[...]
    s = jnp.concatenate(scores, axis=1) + (cum_new - jnp.concatenate(cums, axis=1))
    kvn = kvn_ref[...]
    s_new = _new_key_mask(_dot_nt(qbd, _new_key_tile(kvn[:, 0:D_MODEL])))
    p, p_new, l = _softmax_with_new_key(s, s_new)
    p = p.astype(BF16)
    acc = _bdot(p_new.astype(BF16), _new_key_tile(kvn[:, D_MODEL:2 * D_MODEL]))
    for i, kv_page in enumerate(kv_pages):
        acc = acc + _dot_nt(p[:, i * PAGE_SIZE:(i + 1) * PAGE_SIZE], kv_page[D_MODEL:2 * D_MODEL, :].astype(BF16))
    o_ref[...] = jnp.sum(jnp.where(own, acc / l, 0.0), axis=0, keepdims=True).astype(o_ref.dtype)


def _fox_decode(page_table, q, cache_kv_t, cache_logf_t, kv_new, logf_new):
    ns, npg = page_table.shape
    per_b = lambda shape: pl.BlockSpec((None,) + shape, lambda b, pt: (b, 0, 0))
    out = pl.pallas_call(
        functools.partial(_fox_dec_kernel, npg),
        grid_spec=_paged_grid_spec(
            (ns,),
            _page_specs(npg, 2 * D_MODEL) + _page_specs(npg, FOX_HEADS)
            + [per_b((1, D_MODEL)), per_b((1, 2 * D_MODEL)), per_b((FOX_HEADS, 1))],
            per_b((1, D_MODEL)), []),
        out_shape=jax.ShapeDtypeStruct((ns, 1, D_MODEL), BF16),
        compiler_params=_cparams("parallel"),
        name="fox_decode",
    )(page_table.reshape(-1), *([cache_kv_t] * npg), *([cache_logf_t] * npg), q.reshape(ns, 1, D_MODEL),
      kv_new.reshape(ns, 1, 2 * D_MODEL), logf_new.reshape(ns, FOX_HEADS, 1))
    return out.reshape(ns, D_MODEL)


def _fox_sample(rows, x, g, mod, w_in, b_f, cache_kv, cache_logf, page_table):
    w, bias = _fox_weights(w_in, b_f)
    q, kv, logf = _norm_proj(rows, x, g, mod, (0, 1), w, bias, _fox_proj_outs((F32,), "scale"))
    logf = logf[:, :FOX_HEADS]
    o = _fox_decode(page_table, q, _feat_major_pages(cache_kv), _feat_major_pages(cache_logf), kv, logf)
    return o, kv, logf


PROMPT_TM = 512
FFN_TM = 1024
POOL_TM = 256


def kernel(x_prompt, x_sample, cache_l0_cmp_kv, cache_l0_sel_kv, state_l0_win_kv, cache_l1_kv, cache_l1_logf, state_l2_pool, cache_l3_cmp_kv, cache_l3_sel_kv, state_l3_win_kv, state_ffn_conv, page_table, c_prompt, c_sample, mod_w, mod_b, norm_g, l0_nsa_w_in, l0_nsa_w_ck, l0_nsa_w_cv, l0_nsa_w_o, l1_fox_w_in, l1_fox_b_f, l1_fox_w_o, l2_pool_w, l2_pool_b, l2_pool_scale, l3_nsa_w_in, l3_nsa_w_ck, l3_nsa_w_cv, l3_nsa_w_o, ffn_w_gu, ffn_conv_w, ffn_conv_b, ffn_w_d):
    b, t, _ = x_prompt.shape
    ns = x_sample.shape[0]
    past_len = page_table.shape[1] * PAGE_SIZE
    rp = _Rows(b, t, min(PROMPT_TM, t))
    rp_ffn = _Rows(b, t, min(FFN_TM, t))
    rs = _Rows(ns, 1, ns)
    nsa = {0: (cache_l0_cmp_kv, cache_l0_sel_kv, state_l0_win_kv, l0_nsa_w_in, l0_nsa_w_ck, l0_nsa_w_cv, l0_nsa_w_o),
           3: (cache_l3_cmp_kv, cache_l3_sel_kv, state_l3_win_kv, l3_nsa_w_in, l3_nsa_w_ck, l3_nsa_w_cv, l3_nsa_w_o)}

    c_all = jnp.concatenate([c_prompt, c_sample], axis=0)
    c_all = jnp.pad(c_all, ((0, -c_all.shape[0] % SUBLANES), (0, 0)))
    mod = _modulation(c_all, mod_w, mod_b)

    xp = x_prompt.reshape(b * t, D_MODEL)
    xs = x_sample.reshape(ns, D_MODEL)
    st = {}
    conv_p, conv_s = [], []
    kv5 = lambda a, n: a.reshape(n, -1, 2, NSA_KV_HEADS, HEAD_DIM)
    for i in range(DEPTH):
        mp, ms = mod[i, :b], mod[i, b:b + ns]
        g = norm_g[i]
        kind = i % 3
        if kind == 0:
            c_c, c_s, s_w, w_in, w_ck, w_cv, w_o = nsa[i]
            op, kvc_t, kvs_t, kvw_t = _nsa_prompt(rp, xp, g[0], mp, w_in, w_ck, w_cv)
            os_, kvc_s, kvs_s, win_s = _nsa_sample(rs, xs, g[0], ms, w_in, w_ck, w_cv, c_c, c_s, s_w, page_table)
            leaf = lambda a: _leaf_from_feat_major(a, (2, NSA_KV_HEADS, HEAD_DIM))
            st[i] = (leaf(kvc_t), kv5(kvc_s, ns), leaf(kvs_t), kv5(kvs_s, ns),
                     leaf(kvw_t[:, :, -min(NSA_WINDOW, t):]), win_s)
            w_ob = w_o.astype(BF16)
            xp = _out_proj_residual(rp, op, w_ob, xp, g[1], mp, 2)
            xs = _out_proj_residual(rs, os_, w_ob, xs, g[1], ms, 2)
        elif kind == 1:
            op, kv_p, lf_p = _fox_prompt(rp, xp, g[0], mp, l1_fox_w_in, l1_fox_b_f)
            os_, kv_s, lf_s = _fox_sample(rs, xs, g[0], ms, l1_fox_w_in, l1_fox_b_f, cache_l1_kv, cache_l1_logf, page_table)
            st[i] = (_leaf_from_feat_major(kv_p, (2, FOX_HEADS, HEAD_DIM)), kv_s.reshape(ns, 1, 2, FOX_HEADS, HEAD_DIM),
                     lf_p.reshape(b, t, FOX_HEADS), lf_s.reshape(ns, 1, FOX_HEADS))
            w_ob = l1_fox_w_o.astype(BF16)
            xp = _out_proj_residual(rp, op, w_ob, xp, g[1], mp, 2)
            xs = _out_proj_residual(rs, os_, w_ob, xs, g[1], ms, 2)
        else:
            hp = _norm_only(rp, xp, g[0], mp, (0, 1))
            hs = _norm_only(rs, xs, g[0], ms, (0, 1))
            yp = _pool_mix(hp, b, t, min(POOL_TM, t), 0, l2_pool_w, l2_pool_b.reshape(-1), l2_pool_scale)
            ext = jnp.concatenate([state_l2_pool, hs[:, None, :]], axis=1)
            n_ext = POOL_STATE + 1
            ys = _pool_mix(ext.reshape(ns * n_ext, D_MODEL), ns, n_ext, n_ext, past_len - POOL_STATE,
                           l2_pool_w, l2_pool_b.reshape(-1), l2_pool_scale).reshape(ns, n_ext, D_MODEL)[:, -1]
            st[i] = (hp.reshape(b, t, D_MODEL)[:, -POOL_STATE:], ext[:, -POOL_STATE:])
            xp = _gated_residual(rp, yp, xp, g[1], mp, 2)
            xs = _gated_residual(rs, ys, xs, g[1], ms, 2)
        w_gu, w_d = ffn_w_gu[i].astype(BF16), ffn_w_d[i].astype(BF16)
        xp, cp = _conv_ffn(rp_ffn, xp, g[2], g[3], mp, w_gu, ffn_conv_w[i], ffn_conv_b[i], w_d, None)
        xs, cs = _conv_ffn(rs, xs, g[2], g[3], ms, w_gu, ffn_conv_w[i], ffn_conv_b[i], w_d, state_ffn_conv[i])
        conv_p.append(cp)
        conv_s.append(cs)
    return (xp.reshape(b, t, D_MODEL), xs.reshape(ns, 1, D_MODEL),
            *st[0], *st[1], *st[2], *st[3],
            jnp.stack(conv_p), jnp.stack(conv_s))
```
